```python
import math
import jax, jax.numpy as jnp
from jax import lax
import numpy as np

D_MODEL = 1024
BATCH = 2
SEQ = 8192
DEPTH = 2
DEC_BATCH = 128
DEC_SEQ = 4
PAST_LEN = 2048
PAGE_SIZE = 128

CONV_DIM = 512
CONV_WIDTH = 31
N_HEADS = 8
N_KV_HEADS = 2
HEAD_DIM = 64
IDX_HEADS = 4
IDX_DIM = 64
TOPK_MAX = 256
Q_BLOCK = 128
SSM_HEADS = 16
SSM_HEAD_DIM = 64
SSM_INNER = SSM_HEADS * SSM_HEAD_DIM
SSM_GROUPS = 2
SSM_STATE = 128
SSM_CONV = 4
SSM_CHUNK = 128
SSM_XBC = SSM_INNER + 2 * SSM_GROUPS * SSM_STATE
N_MEM = 256
MEM_HEADS = 4
MEM_HEAD_DIM = 128
N_GROUPS = 4
EXPERTS_PER_GROUP = 4
N_EXPERTS = N_GROUPS * EXPERTS_PER_GROUP
TOPK_IN_GROUP = 2
D_EXPERT = 512
DN_ALPHA = (2 * DEPTH) ** 0.25
DN_BETA = (8 * DEPTH) ** -0.25
LN_EPS = 1e-5

SPLIT_SIZES = (2 * CONV_DIM, N_HEADS * HEAD_DIM, N_KV_HEADS * HEAD_DIM, N_KV_HEADS * HEAD_DIM,
               IDX_HEADS * IDX_DIM, IDX_DIM, IDX_HEADS, SSM_INNER, SSM_XBC, SSM_HEADS, 3 * D_MODEL)
SPLIT_POINTS = tuple(sum(SPLIT_SIZES[:i + 1]) for i in range(len(SPLIT_SIZES) - 1))
D_IN = sum(SPLIT_SIZES)

kernel_name = 'hybrid_conv_dsa_ssd_hmoe_step'


def layer_norm(x, g, b):
    xf = x.astype(jnp.float32)
    mu = jnp.mean(xf, -1, keepdims=True)
    var = jnp.mean(jnp.square(xf - mu), -1, keepdims=True)
    return ((xf - mu) * lax.rsqrt(var + LN_EPS) * g + b).astype(x.dtype)


def rms_norm(x, g):
    xf = x.astype(jnp.float32)
    return (xf * lax.rsqrt(jnp.mean(jnp.square(xf), -1, keepdims=True) + LN_EPS) * g).astype(x.dtype)


def causal_dwconv(x_pad, w):
    return lax.conv_general_dilated(x_pad, w[:, None, :], window_strides=(1,), padding='VALID',
                                    dimension_numbers=('NWC', 'WIO', 'NWC'),
                                    feature_group_count=x_pad.shape[-1])


def dsa_attend(q, qi, wi, k, v, ki, q_pos, k_sel):
    f32 = jnp.float32
    idx = jnp.einsum('bthd,bsd->bths', qi, ki, preferred_element_type=f32)
    idx = jnp.einsum('bths,bth->bts', jax.nn.relu(idx), wi.astype(f32))
    k_pos = jnp.arange(k.shape[1])
    idx = jnp.where(k_pos[None, None, :] <= q_pos[None, :, None], idx, -jnp.inf)
    _, sel = lax.top_k(idx, k_sel)
    valid = sel <= q_pos[None, :, None]
    gather = jax.vmap(lambda rows, ids: rows[ids])
    kg = gather(k, sel)
    vg = gather(v, sel)
    s = jnp.einsum('btjgd,btsjd->btjgs', q, kg, preferred_element_type=f32) / math.sqrt(HEAD_DIM)
    s = jnp.where(valid[:, :, None, None, :], s, -jnp.inf)
    p = jax.nn.softmax(s, axis=-1).astype(vg.dtype)
    return jnp.einsum('btjgs,btsjd->btjgd', p, vg)


def ssd_scan(x, dt, a, bm, cm, h0):
    f32 = jnp.float32
    bsz, l, nh, hp = x.shape
    rep = nh // bm.shape[2]
    chunk = min(SSM_CHUNK, l)
    pad = (-l) % chunk
    xdt = x.astype(f32) * dt[..., None]
    la = dt * a
    bh = jnp.repeat(bm.astype(f32), rep, axis=2)
    ch = jnp.repeat(cm.astype(f32), rep, axis=2)
    if pad:
        pw = ((0, 0), (0, pad), (0, 0), (0, 0))
        xdt, bh, ch = jnp.pad(xdt, pw), jnp.pad(bh, pw), jnp.pad(ch, pw)
        la = jnp.pad(la, ((0, 0), (0, pad), (0, 0)))
    nc = (l + pad) // chunk
    ns = bh.shape[-1]
    xdt = xdt.reshape(bsz, nc, chunk, nh, hp)
    la = la.reshape(bsz, nc, chunk, nh)
    bh = bh.reshape(bsz, nc, chunk, nh, ns)
    ch = ch.reshape(bsz, nc, chunk, nh, ns)
    cs = jnp.cumsum(la, axis=2)
    causal = jnp.tril(jnp.ones((chunk, chunk), bool))
    seg = cs[:, :, :, None, :] - cs[:, :, None, :, :]
    decay = jnp.exp(jnp.where(causal[None, None, :, :, None], seg, -jnp.inf))
    att = jnp.einsum('bcqhn,bckhn->bcqkh', ch, bh) * decay
    y_diag = jnp.einsum('bcqkh,bckhp->bcqhp', att, xdt)
    to_end = jnp.exp(cs[:, :, -1:, :] - cs)
    states = jnp.einsum('bckhn,bckh,bckhp->bchpn', bh, to_end, xdt)
    chunk_decay = jnp.exp(cs[:, :, -1, :])

    def step(hc, inp):
        dc, st = inp
        return hc * dc[:, :, None, None] + st, hc

    h_last, h_prev = lax.scan(step, h0.astype(f32), (chunk_decay.swapaxes(0, 1), states.swapaxes(0, 1)))
    h_prev = h_prev.swapaxes(0, 1)
    y_off = jnp.einsum('bcqhn,bchpn,bcqh->bcqhp', ch, h_prev, jnp.exp(cs))
    y = (y_diag + y_off).reshape(bsz, nc * chunk, nh, hp)[:, :l]
    return y, h_last


def token_mixer(x, p, conv_buf, ssm_buf, ssm_h0, past):
    (w_in, conv_dw, ln_conv_g, ln_conv_b, w_conv_out, w_attn_out, ssm_conv_w, ssm_conv_b,
     ssm_dt_bias, ssm_a_log, ssm_d, ssm_norm_g, w_ssm_out, w_mix_out) = p
    b, t, _ = x.shape
    (glu_in, q, k, v, qi, ki, wi, z, xbc, dt, gates) = jnp.split(x @ w_in, SPLIT_POINTS, axis=-1)
    glu = glu_in[..., :CONV_DIM] * jax.nn.sigmoid(glu_in[..., CONV_DIM:])
    glu_pad = jnp.concatenate([conv_buf.astype(glu.dtype), glu], axis=1)
    ca = layer_norm(causal_dwconv(glu_pad, conv_dw), ln_conv_g, ln_conv_b)
    y_a = jax.nn.silu(ca) @ w_conv_out
    q = q.reshape(b, t, N_KV_HEADS, N_HEADS // N_KV_HEADS, HEAD_DIM)
    k = k.reshape(b, t, N_KV_HEADS, HEAD_DIM)
    v = v.reshape(b, t, N_KV_HEADS, HEAD_DIM)
    qi = qi.reshape(b, t, IDX_HEADS, IDX_DIM)
    if past is None:
        k_sel = min(TOPK_MAX, t // 4)
        nb = t // Q_BLOCK

        def blocks(arr):
            return arr.reshape((b, nb, Q_BLOCK) + arr.shape[2:]).swapaxes(0, 1)

        def one_block(args):
            qb, qib, wib, posb = args
            return dsa_attend(qb, qib, wib, k, v, ki, posb, k_sel)

        o = lax.map(one_block, (blocks(q), blocks(qi), blocks(wi), jnp.arange(t).reshape(nb, Q_BLOCK)))
        o = o.swapaxes(0, 1)
    else:
        pk, pv, pik, pos0 = past
        k_all = jnp.concatenate([pk, k], axis=1)
        v_all = jnp.concatenate([pv, v], axis=1)
        ki_all = jnp.concatenate([pik, ki], axis=1)
        k_sel = min(TOPK_MAX, k_all.shape[1] // 4)
        o = dsa_attend(q, qi, wi, k_all, v_all, ki_all, pos0 + jnp.arange(t), k_sel)
    y_b = o.reshape(b, t, N_HEADS * HEAD_DIM) @ w_attn_out
    xbc_pad = jnp.concatenate([ssm_buf.astype(xbc.dtype), xbc], axis=1)
    xbc_c = jax.nn.silu(causal_dwconv(xbc_pad, ssm_conv_w) + ssm_conv_b)
    xs, bm, cm = jnp.split(xbc_c, [SSM_INNER, SSM_INNER + SSM_GROUPS * SSM_STATE], axis=-1)
    xs = xs.reshape(b, t, SSM_HEADS, SSM_HEAD_DIM)
    dtf = jax.nn.softplus(dt.astype(jnp.float32) + ssm_dt_bias)
    a = -jnp.exp(ssm_a_log.astype(jnp.float32))
    y, h_last = ssd_scan(xs, dtf, a, bm.reshape(b, t, SSM_GROUPS, SSM_STATE),
                         cm.reshape(b, t, SSM_GROUPS, SSM_STATE), ssm_h0)
    y = (y + ssm_d[:, None] * xs.astype(jnp.float32)).astype(x.dtype).reshape(b, t, SSM_INNER)
    y_c = rms_norm(y * jax.nn.silu(z), ssm_norm_g) @ w_ssm_out
    g_a, g_b, g_c = jnp.split(jax.nn.sigmoid(gates), 3, axis=-1)
    out = (g_a * y_a + g_b * y_b + g_c * y_c) @ w_mix_out
    return out, k, v, ki, glu_pad[:, -(CONV_WIDTH - 1):], xbc_pad[:, -(SSM_CONV - 1):], h_last


def mem_attention(x, mk, mv, w_mq, w_mo):
    b, t, _ = x.shape
    q = (x @ w_mq).reshape(b, t, MEM_HEADS, MEM_HEAD_DIM)
    s = jnp.einsum('bthd,bmhd->bhtm', q, mk, preferred_element_type=jnp.float32) / math.sqrt(MEM_HEAD_DIM)
    p = jax.nn.softmax(s, axis=-1).astype(mv.dtype)
    o = jnp.einsum('bhtm,bmhd->bthd', p, mv)
    return o.reshape(b, t, MEM_HEADS * MEM_HEAD_DIM) @ w_mo


def hier_moe(x, w_group, b_group, w_router, b_router, w_e_gate, w_e_up, w_e_down):
    b, t, d = x.shape
    xf = x.reshape(b * t, d)
    pg = jax.nn.softmax((xf @ w_group + b_group).astype(jnp.float32), axis=-1)
    g_prob, g_idx = lax.top_k(pg, 1)
    le = (xf @ w_router + b_router).astype(jnp.float32).reshape(-1, N_GROUPS, EXPERTS_PER_GROUP)
    le = jnp.take_along_axis(le, g_idx[:, :, None], axis=1)[:, 0]
    pe = jax.nn.softmax(le, axis=-1)
    e_prob, e_idx = lax.top_k(pe, TOPK_IN_GROUP)
    wts = g_prob * e_prob / jnp.sum(e_prob, -1, keepdims=True)
    expert_id = g_idx * EXPERTS_PER_GROUP + e_idx
    combine = jnp.sum(jax.nn.one_hot(expert_id, N_EXPERTS, dtype=jnp.float32) * wts[..., None], axis=1)
    hg = jnp.einsum('nd,edf->nef', xf, w_e_gate)
    hu = jnp.einsum('nd,edf->nef', xf, w_e_up)
    hid = jax.nn.silu(hg) * hu * combine[:, :, None].astype(x.dtype)
    return jnp.einsum('nef,efd->nd', hid, w_e_down).reshape(b, t, d)


def setup_inputs(seed: int = 0) -> dict:
    key = jax.random.key(seed)
    ks = iter(jax.random.split(key, 64))

    def nrm(shape, scale=1.0):
        return jax.random.normal(next(ks), shape, jnp.float32) * scale

    n_pages = PAST_LEN // PAGE_SIZE
    n_phys = (DEC_BATCH * n_pages * 5) // 4
    fd = D_MODEL ** -0.5
    perm = jax.random.permutation(next(ks), n_phys)
    page_table = perm[:DEC_BATCH * n_pages].reshape(DEC_BATCH, n_pages).astype(jnp.int32)
    dt0 = jnp.exp(jax.random.uniform(next(ks), (DEPTH, SSM_HEADS), jnp.float32,
                                     minval=math.log(1e-3), maxval=math.log(1e-1)))
    ssm_dt_bias = dt0 + jnp.log(-jnp.expm1(-dt0))
    ssm_a_log = jnp.log(jax.random.uniform(next(ks), (DEPTH, SSM_HEADS), jnp.float32, minval=1.0, maxval=16.0))
    return {
        'x_prompt': nrm((BATCH, SEQ, D_MODEL)),
        'x_sample': nrm((DEC_BATCH, DEC_SEQ, D_MODEL)),
        'mem_prompt': nrm((BATCH, N_MEM, D_MODEL)),
        'cache_k': nrm((n_phys, DEPTH, PAGE_SIZE, N_KV_HEADS, HEAD_DIM)),
        'cache_v': nrm((n_phys, DEPTH, PAGE_SIZE, N_KV_HEADS, HEAD_DIM)),
        'cache_ik': nrm((n_phys, DEPTH, PAGE_SIZE, IDX_DIM)),
        'cache_mem_k': nrm((DEC_BATCH, DEPTH, N_MEM, MEM_HEADS, MEM_HEAD_DIM)),
        'cache_mem_v': nrm((DEC_BATCH, DEPTH, N_MEM, MEM_HEADS, MEM_HEAD_DIM), DN_BETA),
        'state_conv': nrm((DEC_BATCH, DEPTH, CONV_WIDTH - 1, CONV_DIM), 0.5),
        'state_ssm_conv': nrm((DEC_BATCH, DEPTH, SSM_CONV - 1, SSM_XBC)),
        'state_ssm': nrm((DEC_BATCH, DEPTH, SSM_HEADS, SSM_HEAD_DIM, SSM_STATE), 0.1),
        'page_table': page_table,
        'w_in': nrm((DEPTH, D_MODEL, D_IN), fd),
        'conv_dw': nrm((DEPTH, CONV_WIDTH, CONV_DIM), CONV_WIDTH ** -0.5),
        'ln_conv_g': 1.0 + nrm((DEPTH, CONV_DIM), 0.02),
        'ln_conv_b': nrm((DEPTH, CONV_DIM), 0.02),
        'w_conv_out': nrm((DEPTH, CONV_DIM, D_MODEL), CONV_DIM ** -0.5),
        'w_attn_out': nrm((DEPTH, N_HEADS * HEAD_DIM, D_MODEL), (N_HEADS * HEAD_DIM) ** -0.5),
        'ssm_conv_w': nrm((DEPTH, SSM_CONV, SSM_XBC), SSM_CONV ** -0.5),
        'ssm_conv_b': nrm((DEPTH, SSM_XBC), 0.02),
        'ssm_dt_bias': ssm_dt_bias,
        'ssm_a_log': ssm_a_log,
        'ssm_d': 1.0 + nrm((DEPTH, SSM_HEADS), 0.1),
        'ssm_norm_g': 1.0 + nrm((DEPTH, SSM_INNER), 0.02),
        'w_ssm_out': nrm((DEPTH, SSM_INNER, D_MODEL), SSM_INNER ** -0.5),
        'w_mix_out': nrm((DEPTH, D_MODEL, D_MODEL), fd * DN_BETA),
        'ln_mix_g': 1.0 + nrm((DEPTH, D_MODEL), 0.02),
        'ln_mix_b': nrm((DEPTH, D_MODEL), 0.02),
        'w_mq': nrm((DEPTH, D_MODEL, MEM_HEADS * MEM_HEAD_DIM), fd),
        'w_mk': nrm((DEPTH, D_MODEL, MEM_HEADS * MEM_HEAD_DIM), fd),
        'w_mv': nrm((DEPTH, D_MODEL, MEM_HEADS * MEM_HEAD_DIM), fd * DN_BETA),
        'w_mo': nrm((DEPTH, MEM_HEADS * MEM_HEAD_DIM, D_MODEL), (MEM_HEADS * MEM_HEAD_DIM) ** -0.5 * DN_BETA),
        'ln_mem_g': 1.0 + nrm((DEPTH, D_MODEL), 0.02),
        'ln_mem_b': nrm((DEPTH, D_MODEL), 0.02),
        'w_group': nrm((DEPTH, D_MODEL, N_GROUPS), fd),
        'b_group': nrm((DEPTH, N_GROUPS), 0.01),
        'w_router': nrm((DEPTH, D_MODEL, N_EXPERTS), fd),
        'b_router': nrm((DEPTH, N_EXPERTS), 0.01),
        'w_e_gate': nrm((DEPTH, N_EXPERTS, D_MODEL, D_EXPERT), fd),
        'w_e_up': nrm((DEPTH, N_EXPERTS, D_MODEL, D_EXPERT), fd),
        'w_e_down': nrm((DEPTH, N_EXPERTS, D_EXPERT, D_MODEL), D_EXPERT ** -0.5 * DN_BETA),
        'ln_ffn_g': 1.0 + nrm((DEPTH, D_MODEL), 0.02),
        'ln_ffn_b': nrm((DEPTH, D_MODEL), 0.02),
    }


def reference(x_prompt, x_sample, mem_prompt, cache_k, cache_v, cache_ik, cache_mem_k, cache_mem_v,
              state_conv, state_ssm_conv, state_ssm, page_table,
              w_in, conv_dw, ln_conv_g, ln_conv_b, w_conv_out, w_attn_out, ssm_conv_w, ssm_conv_b,
              ssm_dt_bias, ssm_a_log, ssm_d, ssm_norm_g, w_ssm_out, w_mix_out, ln_mix_g, ln_mix_b,
              w_mq, w_mk, w_mv, w_mo, ln_mem_g, ln_mem_b,
              w_group, b_group, w_router, b_router, w_e_gate, w_e_up, w_e_down, ln_ffn_g, ln_ffn_b):
    bp = x_prompt.shape[0]
    bs = x_sample.shape[0]
    past_len = page_table.shape[1] * PAGE_SIZE
    xp, xs = x_prompt, x_sample
    outs_p = [[] for _ in range(8)]
    outs_s = [[] for _ in range(6)]
    for l in range(DEPTH):
        mix_p = (w_in[l], conv_dw[l], ln_conv_g[l], ln_conv_b[l], w_conv_out[l], w_attn_out[l],
                 ssm_conv_w[l], ssm_conv_b[l], ssm_dt_bias[l], ssm_a_log[l], ssm_d[l], ssm_norm_g[l],
                 w_ssm_out[l], w_mix_out[l])
        moe_p = (w_group[l], b_group[l], w_router[l], b_router[l], w_e_gate[l], w_e_up[l], w_e_down[l])
        mp, kp, vp, kip, cbp, sbp, hp = token_mixer(
            xp, mix_p,
            jnp.zeros((bp, CONV_WIDTH - 1, CONV_DIM), xp.dtype),
            jnp.zeros((bp, SSM_CONV - 1, SSM_XBC), xp.dtype),
            jnp.zeros((bp, SSM_HEADS, SSM_HEAD_DIM, SSM_STATE), jnp.float32),
            None)
        xp = layer_norm(DN_ALPHA * xp + mp, ln_mix_g[l], ln_mix_b[l])
        mkp = (mem_prompt @ w_mk[l]).reshape(bp, N_MEM, MEM_HEADS, MEM_HEAD_DIM)
        mvp = (mem_prompt @ w_mv[l]).reshape(bp, N_MEM, MEM_HEADS, MEM_HEAD_DIM)
        xp = layer_norm(DN_ALPHA * xp + mem_attention(xp, mkp, mvp, w_mq[l], w_mo[l]), ln_mem_g[l], ln_mem_b[l])
        xp = layer_norm(DN_ALPHA * xp + hier_moe(xp, *moe_p), ln_ffn_g[l], ln_ffn_b[l])
        pk = cache_k[page_table, l].reshape(bs, past_len, N_KV_HEADS, HEAD_DIM)
        pv = cache_v[page_table, l].reshape(bs, past_len, N_KV_HEADS, HEAD_DIM)
        pik = cache_ik[page_table, l].reshape(bs, past_len, IDX_DIM)
        ms, ks_new, vs_new, kis, cbs, sbs, hs = token_mixer(
            xs, mix_p, state_conv[:, l], state_ssm_conv[:, l], state_ssm[:, l], (pk, pv, pik, past_len))
        xs = layer_norm(DN_ALPHA * xs + ms, ln_mix_g[l], ln_mix_b[l])
        xs = layer_norm(DN_ALPHA * xs + mem_attention(xs, cache_mem_k[:, l], cache_mem_v[:, l], w_mq[l], w_mo[l]),
                        ln_mem_g[l], ln_mem_b[l])
        xs = layer_norm(DN_ALPHA * xs + hier_moe(xs, *moe_p), ln_ffn_g[l], ln_ffn_b[l])
        for lst, arr in zip(outs_p, (kp, vp, kip, mkp, mvp, cbp, sbp, hp)):
            lst.append(arr)
        for lst, arr in zip(outs_s, (ks_new, vs_new, kis, cbs, sbs, hs)):
            lst.append(arr)
    p_k, p_v, p_ik, p_mem_k, p_mem_v, p_conv, p_ssm_conv, p_ssm = [jnp.stack(a, axis=1) for a in outs_p]
    s_k, s_v, s_ik, s_conv, s_ssm_conv, s_ssm = [jnp.stack(a, axis=1) for a in outs_s]
    return (xp, xs, p_k, p_v, p_ik, p_mem_k, p_mem_v, p_conv, p_ssm_conv, p_ssm,
            s_k, s_v, s_ik, s_conv, s_ssm_conv, s_ssm)
```

```python
import functools
import math

import jax
import jax.numpy as jnp
from jax import lax
from jax.experimental import pallas as pl
from jax.experimental.pallas import tpu as pltpu

D_MODEL = 1024
DEPTH = 2
PAGE_SIZE = 128
CONV_DIM = 512
CONV_WIDTH = 31
N_HEADS = 8
N_KV_HEADS = 2
HEAD_DIM = 64
IDX_HEADS = 4
IDX_DIM = 64
TOPK_MAX = 256
Q_BLOCK = 128
SSM_HEADS = 16
SSM_HEAD_DIM = 64
SSM_INNER = SSM_HEADS * SSM_HEAD_DIM
SSM_GROUPS = 2
SSM_STATE = 128
SSM_CONV = 4
SSM_CHUNK = 128
SSM_XBC = SSM_INNER + 2 * SSM_GROUPS * SSM_STATE
N_MEM = 256
MEM_HEADS = 4
MEM_HEAD_DIM = 128
N_GROUPS = 4
EXPERTS_PER_GROUP = 4
N_EXPERTS = N_GROUPS * EXPERTS_PER_GROUP
TOPK_IN_GROUP = 2
D_EXPERT = 512
DN_ALPHA = (2 * DEPTH) ** 0.25
LN_EPS = 1e-5

SPLIT_SIZES = (2 * CONV_DIM, N_HEADS * HEAD_DIM, N_KV_HEADS * HEAD_DIM, N_KV_HEADS * HEAD_DIM,
               IDX_HEADS * IDX_DIM, IDX_DIM, IDX_HEADS, SSM_INNER, SSM_XBC, SSM_HEADS, 3 * D_MODEL)
SPLIT_POINTS = tuple(sum(SPLIT_SIZES[:i + 1]) for i in range(len(SPLIT_SIZES) - 1))

VMEM_LIMIT_BYTES = 56 * 1024 * 1024
MM_COL_CHUNK = 512


def _mm_kernel(x_ref, w_ref, o_ref):
    xb = x_ref[...].astype(jnp.bfloat16)
    n = o_ref.shape[1]
    for c0 in range(0, n, MM_COL_CHUNK):
        c1 = min(n, c0 + MM_COL_CHUNK)
        o_ref[:, c0:c1] = jnp.dot(xb, w_ref[:, c0:c1], preferred_element_type=jnp.float32)


def mm(x, w, tm=256):
    m, k = x.shape
    n = w.shape[1]
    tm = min(tm, m)
    assert m % tm == 0 and n % 128 == 0 and k % 128 == 0
    return pl.pallas_call(
        _mm_kernel,
        grid=(m // tm,),
        in_specs=[pl.BlockSpec((tm, k), lambda i: (i, 0)),
                  pl.BlockSpec((k, n), lambda i: (0, 0), pipeline_mode=pl.Buffered(1))],
        out_specs=pl.BlockSpec((tm, n), lambda i: (i, 0)),
        out_shape=jax.ShapeDtypeStruct((m, n), jnp.float32),
        compiler_params=pltpu.CompilerParams(dimension_semantics=("parallel",),
                                             vmem_limit_bytes=VMEM_LIMIT_BYTES),
    )(x, w)


def mm3(x, w, tm=256):
    b, t, d = x.shape
    return mm(x.reshape(b * t, d), w, tm).reshape(b, t, w.shape[1])


def layer_norm(x, g, b):
    mu = jnp.mean(x, -1, keepdims=True)
    var = jnp.mean(jnp.square(x - mu), -1, keepdims=True)
    return (x - mu) * lax.rsqrt(var + LN_EPS) * g + b


def rms_norm(x, g):
    return x * lax.rsqrt(jnp.mean(jnp.square(x), -1, keepdims=True) + LN_EPS) * g


def causal_dwconv(x_pad, w):
    return lax.conv_general_dilated(x_pad, w[:, None, :], window_strides=(1,), padding='VALID',
                                    dimension_numbers=('NWC', 'WIO', 'NWC'),
                                    feature_group_count=x_pad.shape[-1])


def dsa_attend(q, qi, wi, k, v, ki, q_pos, k_sel):
    f32 = jnp.float32
    idx = jnp.einsum('bthd,bsd->bths', qi, ki, preferred_element_type=f32)
    idx = jnp.einsum('bths,bth->bts', jax.nn.relu(idx), wi.astype(f32))
    k_pos = jnp.arange(k.shape[1])
    idx = jnp.where(k_pos[None, None, :] <= q_pos[None, :, None], idx, -jnp.inf)
    _, sel = lax.top_k(idx, k_sel)
    valid = sel <= q_pos[None, :, None]
    gather = jax.vmap(lambda rows, ids: rows[ids])
    kg = gather(k, sel)
    vg = gather(v, sel)
    s = jnp.einsum('btjgd,btsjd->btjgs', q, kg, preferred_element_type=f32) / math.sqrt(HEAD_DIM)
    s = jnp.where(valid[:, :, None, None, :], s, -jnp.inf)
    p = jax.nn.softmax(s, axis=-1).astype(vg.dtype)
    return jnp.einsum('btjgs,btsjd->btjgd', p, vg)


def ssd_scan(x, dt, a, bm, cm, h0):
    f32 = jnp.float32
    bsz, l, nh, hp = x.shape
    rep = nh // bm.shape[2]
    chunk = min(SSM_CHUNK, l)
    assert l % chunk == 0
    xdt = x.astype(f32) * dt[..., None]
    la = dt * a
    bh = jnp.repeat(bm.astype(f32), rep, axis=2)
    ch = jnp.repeat(cm.astype(f32), rep, axis=2)
    nc = l // chunk
    ns = bh.shape[-1]
    xdt = xdt.reshape(bsz, nc, chunk, nh, hp)
    la = la.reshape(bsz, nc, chunk, nh)
    bh = bh.reshape(bsz, nc, chunk, nh, ns)
    ch = ch.reshape(bsz, nc, chunk, nh, ns)
    cs = jnp.cumsum(la, axis=2)
    causal = jnp.tril(jnp.ones((chunk, chunk), bool))
    seg = cs[:, :, :, None, :] - cs[:, :, None, :, :]
    decay = jnp.exp(jnp.where(causal[None, None, :, :, None], seg, -jnp.inf))
    att = jnp.einsum('bcqhn,bckhn->bcqkh', ch, bh) * decay
    y_diag = jnp.einsum('bcqkh,bckhp->bcqhp', att, xdt)
    to_end = jnp.exp(cs[:, :, -1:, :] - cs)
    states = jnp.einsum('bckhn,bckh,bckhp->bchpn', bh, to_end, xdt)
    chunk_decay = jnp.exp(cs[:, :, -1, :])

    def step(hc, inp):
        dc, st = inp
        return hc * dc[:, :, None, None] + st, hc

    h_last, h_prev = lax.scan(step, h0.astype(f32), (chunk_decay.swapaxes(0, 1), states.swapaxes(0, 1)))
    h_prev = h_prev.swapaxes(0, 1)
    y_off = jnp.einsum('bcqhn,bchpn,bcqh->bcqhp', ch, h_prev, jnp.exp(cs))
    y = (y_diag + y_off).reshape(bsz, nc * chunk, nh, hp)
    return y, h_last


def token_mixer(x, p, conv_buf, ssm_buf, ssm_h0, past):
    (w_in, conv_dw, ln_conv_g, ln_conv_b, w_conv_out, w_attn_out, ssm_conv_w, ssm_conv_b,
     ssm_dt_bias, ssm_a_log, ssm_d, ssm_norm_g, w_ssm_out, w_mix_out) = p
    b, t, _ = x.shape
    proj = mm3(x, w_in)
    glu_in = proj[..., 0:1024]
    q = proj[..., 1024:1536]
    k = proj[..., 1536:1664]
    v = proj[..., 1664:1792]
    qi = proj[..., 1792:2048]
    ki = proj[..., 2048:2112]
    wi = proj[..., 2112:2116]
    dt = proj[..., 2116:2132]
    z = proj[..., 2176:3200]
    xbc = proj[..., 3200:4736]
    gates = proj[..., 4736:7808]
    glu = glu_in[..., :CONV_DIM] * jax.nn.sigmoid(glu_in[..., CONV_DIM:])
    glu_pad = jnp.concatenate([conv_buf, glu], axis=1)
    ca = layer_norm(causal_dwconv(glu_pad, conv_dw), ln_conv_g, ln_conv_b)
    y_a = mm3(jax.nn.silu(ca), w_conv_out)
    q = q.reshape(b, t, N_KV_HEADS, N_HEADS // N_KV_HEADS, HEAD_DIM)
    k = k.reshape(b, t, N_KV_HEADS, HEAD_DIM)
    v = v.reshape(b, t, N_KV_HEADS, HEAD_DIM)
    qi = qi.reshape(b, t, IDX_HEADS, IDX_DIM)
    if past is None:
        k_sel = min(TOPK_MAX, t // 4)
        nb = t // Q_BLOCK

        def blocks(arr):
            return arr.reshape((b, nb, Q_BLOCK) + arr.shape[2:]).swapaxes(0, 1)

        def one_block(args):
            qb, qib, wib, posb = args
            return dsa_attend(qb, qib, wib, k, v, ki, posb, k_sel)

        o = lax.map(one_block, (blocks(q), blocks(qi), blocks(wi), jnp.arange(t).reshape(nb, Q_BLOCK)))
        o = o.swapaxes(0, 1)
    else:
        pk, pv, pik, pos0 = past
        k_all = jnp.concatenate([pk, k], axis=1)
        v_all = jnp.concatenate([pv, v], axis=1)
        ki_all = jnp.concatenate([pik, ki], axis=1)
        k_sel = min(TOPK_MAX, k_all.shape[1] // 4)
        o = dsa_attend(q, qi, wi, k_all, v_all, ki_all, pos0 + jnp.arange(t), k_sel)
    y_b = mm3(o.reshape(b, t, N_HEADS * HEAD_DIM), w_attn_out)
    xbc_pad = jnp.concatenate([ssm_buf, xbc], axis=1)
    xbc_c = jax.nn.silu(causal_dwconv(xbc_pad, ssm_conv_w) + ssm_conv_b)
    xs, bm, cm = jnp.split(xbc_c, [SSM_INNER, SSM_INNER + SSM_GROUPS * SSM_STATE], axis=-1)
    xs = xs.reshape(b, t, SSM_HEADS, SSM_HEAD_DIM)
    dtf = jax.nn.softplus(dt + ssm_dt_bias)
    a = -jnp.exp(ssm_a_log)
    y, h_last = ssd_scan(xs, dtf, a, bm.reshape(b, t, SSM_GROUPS, SSM_STATE),
                         cm.reshape(b, t, SSM_GROUPS, SSM_STATE), ssm_h0)
    y = (y + ssm_d[:, None] * xs).reshape(b, t, SSM_INNER)
    y_c = mm3(rms_norm(y * jax.nn.silu(z), ssm_norm_g), w_ssm_out)
    g_a, g_b, g_c = jnp.split(jax.nn.sigmoid(gates), 3, axis=-1)
    out = mm3(g_a * y_a + g_b * y_b + g_c * y_c, w_mix_out)
    return out, k, v, ki, glu_pad[:, -(CONV_WIDTH - 1):], xbc_pad[:, -(SSM_CONV - 1):], h_last


def mem_attention(x, mk, mv, w_mq, w_mo):
    b, t, _ = x.shape
    q = mm3(x, w_mq).reshape(b, t, MEM_HEADS, MEM_HEAD_DIM)
    s = jnp.einsum('bthd,bmhd->bhtm', q, mk, preferred_element_type=jnp.float32) / math.sqrt(MEM_HEAD_DIM)
    p = jax.nn.softmax(s, axis=-1)
    o = jnp.einsum('bhtm,bmhd->bthd', p, mv)
    return mm3(o.reshape(b, t, MEM_HEADS * MEM_HEAD_DIM), w_mo)


def hier_moe(x, w_group, b_group, w_router, b_router, w_e_gate, w_e_up, w_e_down):
    b, t, d = x.shape
    xf = x.reshape(b * t, d)
    pg = jax.nn.softmax(xf @ w_group + b_group, axis=-1)
    g_prob, g_idx = lax.top_k(pg, 1)
    le = (xf @ w_router + b_router).reshape(-1, N_GROUPS, EXPERTS_PER_GROUP)
    le = jnp.take_along_axis(le, g_idx[:, :, None], axis=1)[:, 0]
    pe = jax.nn.softmax(le, axis=-1)
    e_prob, e_idx = lax.top_k(pe, TOPK_IN_GROUP)
    wts = g_prob * e_prob / jnp.sum(e_prob, -1, keepdims=True)
    expert_id = g_idx * EXPERTS_PER_GROUP + e_idx
    combine = jnp.sum(jax.nn.one_hot(expert_id, N_EXPERTS, dtype=jnp.float32) * wts[..., None], axis=1)
    hg = jnp.einsum('nd,edf->nef', xf, w_e_gate)
    hu = jnp.einsum('nd,edf->nef', xf, w_e_up)
    hid = jax.nn.silu(hg) * hu * combine[:, :, None]
    return jnp.einsum('nef,efd->nd', hid, w_e_down).reshape(b, t, d)


def _pad_w_in(w):
    sp = (0,) + SPLIT_POINTS + (w.shape[1],)
    seg = [w[:, sp[i]:sp[i + 1]] for i in range(len(SPLIT_SIZES))]
    glu, q, k, v, qi, ki, wi, z, xbc, dt, gates = seg
    pad = jnp.zeros((w.shape[0], 128 - IDX_DIM - IDX_HEADS - SSM_HEADS), w.dtype)
    return jnp.concatenate([glu, q, k, v, qi, ki, wi, dt, pad, z, xbc, gates], axis=1)


def kernel(x_prompt, x_sample, mem_prompt, cache_k, cache_v, cache_ik, cache_mem_k, cache_mem_v, state_conv, state_ssm_conv, state_ssm, page_table, w_in, conv_dw, ln_conv_g, ln_conv_b, w_conv_out, w_attn_out, ssm_conv_w, ssm_conv_b, ssm_dt_bias, ssm_a_log, ssm_d, ssm_norm_g, w_ssm_out, w_mix_out, ln_mix_g, ln_mix_b, w_mq, w_mk, w_mv, w_mo, ln_mem_g, ln_mem_b, w_group, b_group, w_router, b_router, w_e_gate, w_e_up, w_e_down, ln_ffn_g, ln_ffn_b):
    bf16 = jnp.bfloat16
    bp = x_prompt.shape[0]
    bs = x_sample.shape[0]
    past_len = page_table.shape[1] * PAGE_SIZE
    xp, xs = x_prompt, x_sample
    outs_p = [[] for _ in range(8)]
    outs_s = [[] for _ in range(6)]
    for l in range(DEPTH):
        mix_p = (_pad_w_in(w_in[l]).astype(bf16), conv_dw[l], ln_conv_g[l], ln_conv_b[l],
                 w_conv_out[l].astype(bf16), w_attn_out[l].astype(bf16),
                 ssm_conv_w[l], ssm_conv_b[l], ssm_dt_bias[l], ssm_a_log[l], ssm_d[l], ssm_norm_g[l],
                 w_ssm_out[l].astype(bf16), w_mix_out[l].astype(bf16))
        moe_p = (w_group[l], b_group[l], w_router[l], b_router[l], w_e_gate[l], w_e_up[l], w_e_down[l])
        w_mq_l, w_mo_l = w_mq[l].astype(bf16), w_mo[l].astype(bf16)
        mp, kp, vp, kip, cbp, sbp, hp = token_mixer(
            xp, mix_p,
            jnp.zeros((bp, CONV_WIDTH - 1, CONV_DIM), xp.dtype),
            jnp.zeros((bp, SSM_CONV - 1, SSM_XBC), xp.dtype),
            jnp.zeros((bp, SSM_HEADS, SSM_HEAD_DIM, SSM_STATE), jnp.float32),
            None)
        xp = layer_norm(DN_ALPHA * xp + mp, ln_mix_g[l], ln_mix_b[l])
        mkp = (mem_prompt @ w_mk[l]).reshape(bp, N_MEM, MEM_HEADS, MEM_HEAD_DIM)
        mvp = (mem_prompt @ w_mv[l]).reshape(bp, N_MEM, MEM_HEADS, MEM_HEAD_DIM)
        xp = layer_norm(DN_ALPHA * xp + mem_attention(xp, mkp, mvp, w_mq_l, w_mo_l), ln_mem_g[l], ln_mem_b[l])
        xp = layer_norm(DN_ALPHA * xp + hier_moe(xp, *moe_p), ln_ffn_g[l], ln_ffn_b[l])
        pk = cache_k[page_table, l].reshape(bs, past_len, N_KV_HEADS, HEAD_DIM)
        pv = cache_v[page_table, l].reshape(bs, past_len, N_KV_HEADS, HEAD_DIM)
        pik = cache_ik[page_table, l].reshape(bs, past_len, IDX_DIM)
        ms, ks_new, vs_new, kis, cbs, sbs, hs = token_mixer(
            xs, mix_p, state_conv[:, l], state_ssm_conv[:, l], state_ssm[:, l], (pk, pv, pik, past_len))
        xs = layer_norm(DN_ALPHA * xs + ms, ln_mix_g[l], ln_mix_b[l])
        xs = layer_norm(DN_ALPHA * xs + mem_attention(xs, cache_mem_k[:, l], cache_mem_v[:, l], w_mq_l, w_mo_l),
                        ln_mem_g[l], ln_mem_b[l])
        xs = layer_norm(DN_ALPHA * xs + hier_moe(xs, *moe_p), ln_ffn_g[l], ln_ffn_b[l])
        for lst, arr in zip(outs_p, (kp, vp, kip, mkp, mvp, cbp, sbp, hp)):
            lst.append(arr)
        for lst, arr in zip(outs_s, (ks_new, vs_new, kis, cbs, sbs, hs)):
            lst.append(arr)
    p_k, p_v, p_ik, p_mem_k, p_mem_v, p_conv, p_ssm_conv, p_ssm = [jnp.stack(a, axis=1) for a in outs_p]
    s_k, s_v, s_ik, s_conv, s_ssm_conv, s_ssm = [jnp.stack(a, axis=1) for a in outs_s]
    return (xp, xs, p_k, p_v, p_ik, p_mem_k, p_mem_v, p_conv, p_ssm_conv, p_ssm,
            s_k, s_v, s_ik, s_conv, s_ssm_conv, s_ssm)
```

```python
import functools
import math

import jax
import jax.numpy as jnp
from jax import lax
from jax.experimental import pallas as pl
from jax.experimental.pallas import tpu as pltpu

D_MODEL = 1024
DEPTH = 2
PAGE_SIZE = 128
CONV_DIM = 512
CONV_WIDTH = 31
N_HEADS = 8
N_KV_HEADS = 2
HEAD_DIM = 64
IDX_HEADS = 4
IDX_DIM = 64
TOPK_MAX = 256
Q_BLOCK = 128
SSM_HEADS = 16
SSM_HEAD_DIM = 64
SSM_INNER = SSM_HEADS * SSM_HEAD_DIM
SSM_GROUPS = 2
SSM_STATE = 128
SSM_CONV = 4
SSM_CHUNK = 128
SSM_XBC = SSM_INNER + 2 * SSM_GROUPS * SSM_STATE
N_MEM = 256
MEM_HEADS = 4
MEM_HEAD_DIM = 128
N_GROUPS = 4
EXPERTS_PER_GROUP = 4
N_EXPERTS = N_GROUPS * EXPERTS_PER_GROUP
TOPK_IN_GROUP = 2
D_EXPERT = 512
DN_ALPHA = (2 * DEPTH) ** 0.25
LN_EPS = 1e-5

SPLIT_SIZES = (2 * CONV_DIM, N_HEADS * HEAD_DIM, N_KV_HEADS * HEAD_DIM, N_KV_HEADS * HEAD_DIM,
               IDX_HEADS * IDX_DIM, IDX_DIM, IDX_HEADS, SSM_INNER, SSM_XBC, SSM_HEADS, 3 * D_MODEL)
SPLIT_POINTS = tuple(sum(SPLIT_SIZES[:i + 1]) for i in range(len(SPLIT_SIZES) - 1))

VMEM_LIMIT_BYTES = 56 * 1024 * 1024
MM_COL_CHUNK = 512


def _mm_kernel(x_ref, w_ref, o_ref):
    xb = x_ref[...].astype(jnp.bfloat16)
    n = o_ref.shape[1]
    for c0 in range(0, n, MM_COL_CHUNK):
        c1 = min(n, c0 + MM_COL_CHUNK)
        o_ref[:, c0:c1] = jnp.dot(xb, w_ref[:, c0:c1], preferred_element_type=jnp.float32)


def mm(x, w, tm=256):
    m, k = x.shape
    n = w.shape[1]
    tm = min(tm, m)
    assert m % tm == 0 and n % 128 == 0 and k % 128 == 0
    return pl.pallas_call(
        _mm_kernel,
        grid=(m // tm,),
        in_specs=[pl.BlockSpec((tm, k), lambda i: (i, 0)),
                  pl.BlockSpec((k, n), lambda i: (0, 0), pipeline_mode=pl.Buffered(1))],
        out_specs=pl.BlockSpec((tm, n), lambda i: (i, 0)),
        out_shape=jax.ShapeDtypeStruct((m, n), jnp.float32),
        compiler_params=pltpu.CompilerParams(dimension_semantics=("parallel",),
                                             vmem_limit_bytes=VMEM_LIMIT_BYTES),
    )(x, w)


def mm3(x, w, tm=256):
    b, t, d = x.shape
    return mm(x.reshape(b * t, d), w, tm).reshape(b, t, w.shape[1])


INT32_MIN = -2 ** 31
MASK_BIAS = -1e30
DSA_KEY_CHUNK = 512
LANES = 128


def _nt_dot(a, b):
    return lax.dot_general(a, b, (((1,), (1,)), ((), ())), preferred_element_type=jnp.float32)


def _dsa_prompt_kernel(q_ref, qi_ref, sm_ref, k_ref, v_ref, ki_ref, o_ref,
                       key_scr, s_scr, p_scr, qpad_scr, qis_scr, m_scr, l_scr, acc_scr, *, k_sel):
    f32, i32, bf16 = jnp.float32, jnp.int32, jnp.bfloat16
    qb = q_ref.shape[1]
    kc_w = DSA_KEY_CHUNK
    sub = kc_w // LANES
    i = pl.program_id(1)
    n_chunks = (i * qb + qb + kc_w - 1) // kc_w
    lane = lax.broadcasted_iota(i32, (qb, LANES), 1)
    half = [lane < HEAD_DIM, lane >= HEAD_DIM]

    for h in range(IDX_HEADS):
        src = qi_ref[0, :, (h // 2) * LANES:(h // 2 + 1) * LANES]
        qis_scr[h * qb:(h + 1) * qb, :] = jnp.where(half[h % 2], src, 0.0).astype(bf16)
    w_idx = [sm_ref[0, :, IDX_DIM + h:IDX_DIM + h + 1] for h in range(IDX_HEADS)]
    row_pos = i * qb + lax.broadcasted_iota(i32, (qb, kc_w), 0)
    col_iota = lax.broadcasted_iota(i32, (qb, kc_w), 1)

    def score_body(c, carry):
        off = pl.multiple_of(c * kc_w, kc_w)
        d = _nt_dot(qis_scr[...], ki_ref[0, pl.ds(off, kc_w), :])
        idx = w_idx[0] * jnp.maximum(d[0:qb], 0.0)
        for h in range(1, IDX_HEADS):
            idx = idx + w_idx[h] * jnp.maximum(d[h * qb:(h + 1) * qb], 0.0)
        idx = jnp.where(idx == 0.0, 0.0, idx)
        bits = lax.bitcast_convert_type(idx, i32)
        key = bits ^ ((bits >> 31) & 0x7FFFFFFF)
        key_scr[c] = jnp.where(off + col_iota <= row_pos, key, INT32_MIN)
        return carry

    lax.fori_loop(0, n_chunks, score_body, 0)

    def count(pred):
        def body(c, acc):
            for g in range(sub):
                acc = acc + jnp.where(pred(key_scr[c, :, g * LANES:(g + 1) * LANES]), 1.0, 0.0)
            return acc
        acc = lax.fori_loop(0, n_chunks, body, jnp.zeros((qb, LANES), f32))
        return jnp.sum(acc, axis=1, keepdims=True)

    def search_body(it, lo):
        cand = lo + lax.shift_left(jnp.int32(1), 31 - it)
        cand_b = jnp.broadcast_to(cand, (qb, LANES))
        cnt = count(lambda kk: kk >= cand_b)
        return jnp.where(cnt >= float(k_sel), cand, lo)

    thr = lax.fori_loop(0, 32, search_body, jnp.full((qb, 1), INT32_MIN, i32))
    thr_b = jnp.broadcast_to(thr, (qb, LANES))
    need_b = jnp.broadcast_to(float(k_sel) - count(lambda kk: kk > thr_b), (qb, LANES))

    tri = jnp.where(lax.broadcasted_iota(i32, (LANES, LANES), 0) <= lax.broadcasted_iota(i32, (LANES, LANES), 1),
                    1.0, 0.0).astype(bf16)
    zero_bits = jnp.zeros((qb, LANES), i32)
    mask_bits = lax.bitcast_convert_type(jnp.full((qb, LANES), MASK_BIAS, f32), i32)

    def select_body(c, seen):
        for g in range(sub):
            kk = key_scr[c, :, g * LANES:(g + 1) * LANES]
            eq = kk == thr_b
            rank = seen + jnp.dot(jnp.where(eq, 1.0, 0.0).astype(bf16), tri, preferred_element_type=f32)
            sel = ((kk > thr_b) | (eq & (rank <= need_b))) & (kk > INT32_MIN)
            key_scr[c, :, g * LANES:(g + 1) * LANES] = jnp.where(sel, zero_bits, mask_bits)
            seen = jnp.broadcast_to(rank[:, LANES - 1:LANES], (qb, LANES))
        return seen

    lax.fori_loop(0, n_chunks, select_body, jnp.zeros((qb, LANES), f32))

    n_blk = N_HEADS
    scale = 1.0 / math.sqrt(HEAD_DIM)
    for g in range(N_HEADS // N_KV_HEADS):
        src = q_ref[0, :, g * LANES:(g + 1) * LANES] * scale
        for j in range(N_KV_HEADS):
            r = g * N_KV_HEADS + j
            qpad_scr[r * qb:(r + 1) * qb, :] = jnp.where(half[j], src, 0.0).astype(bf16)
    m_scr[...] = jnp.full(m_scr.shape, -jnp.inf, f32)
    l_scr[...] = jnp.zeros(l_scr.shape, f32)
    acc_scr[...] = jnp.zeros(acc_scr.shape, f32)

    def attend_body(c, carry):
        off = pl.multiple_of(c * kc_w, kc_w)
        s_scr[...] = _nt_dot(qpad_scr[...], k_ref[0, pl.ds(off, kc_w), :])
        bias = lax.bitcast_convert_type(key_scr[c], f32)
        for r in range(n_blk):
            rows = slice(r * qb, (r + 1) * qb)
            s = s_scr[rows, :] + bias
            m_old = m_scr[rows, :]
            m_new = jnp.maximum(m_old, jnp.broadcast_to(jnp.max(s, axis=1, keepdims=True), (qb, LANES)))
            alpha = jnp.exp(m_old - m_new)
            p = jnp.exp(s - jnp.tile(m_new, (1, sub)))
            l_scr[rows, :] = alpha * l_scr[rows, :] + jnp.broadcast_to(jnp.sum(p, axis=1, keepdims=True), (qb, LANES))
            acc_scr[rows, :] = alpha * acc_scr[rows, :]
            p_scr[rows, :] = p.astype(bf16)
            m_scr[rows, :] = m_new
        acc_scr[...] += jnp.dot(p_scr[...], v_ref[0, pl.ds(off, kc_w), :], preferred_element_type=f32)
        return carry

    lax.fori_loop(0, n_chunks, attend_body, 0)

    for g in range(N_HEADS // N_KV_HEADS):
        outs = []
        for j in range(N_KV_HEADS):
            rows = slice((g * N_KV_HEADS + j) * qb, (g * N_KV_HEADS + j + 1) * qb)
            outs.append(acc_scr[rows, :] / l_scr[rows, :])
        o_ref[0, :, g * LANES:(g + 1) * LANES] = jnp.where(half[0], outs[0], outs[1])


def dsa_prompt(q, qi, small, k, v, ki2):
    b, t, _ = q.shape
    qb = Q_BLOCK
    assert t % DSA_KEY_CHUNK == 0 and t % qb == 0
    k_sel = min(TOPK_MAX, t // 4)
    n_rows = N_HEADS * qb
    seq_spec = pl.BlockSpec((1, t, LANES), lambda bi, i: (bi, 0, 0))
    return pl.pallas_call(
        functools.partial(_dsa_prompt_kernel, k_sel=k_sel),
        grid=(b, t // qb),
        in_specs=[pl.BlockSpec((1, qb, N_HEADS * HEAD_DIM), lambda bi, i: (bi, i, 0)),
                  pl.BlockSpec((1, qb, IDX_HEADS * IDX_DIM), lambda bi, i: (bi, i, 0)),
                  pl.BlockSpec((1, qb, LANES), lambda bi, i: (bi, i, 0)),
                  seq_spec, seq_spec, seq_spec],
        out_specs=pl.BlockSpec((1, qb, N_HEADS * HEAD_DIM), lambda bi, i: (bi, i, 0)),
        out_shape=jax.ShapeDtypeStruct((b, t, N_HEADS * HEAD_DIM), jnp.float32),
        scratch_shapes=[pltpu.VMEM((t // DSA_KEY_CHUNK, qb, DSA_KEY_CHUNK), jnp.int32),
                        pltpu.VMEM((n_rows, DSA_KEY_CHUNK), jnp.float32),
                        pltpu.VMEM((n_rows, DSA_KEY_CHUNK), jnp.bfloat16),
                        pltpu.VMEM((n_rows, LANES), jnp.bfloat16),
                        pltpu.VMEM((IDX_HEADS * qb, LANES), jnp.bfloat16),
                        pltpu.VMEM((n_rows, LANES), jnp.float32),
                        pltpu.VMEM((n_rows, LANES), jnp.float32),
                        pltpu.VMEM((n_rows, LANES), jnp.float32)],
        compiler_params=pltpu.CompilerParams(dimension_semantics=("parallel", "arbitrary"),
                                             vmem_limit_bytes=VMEM_LIMIT_BYTES),
        name="dsa_prompt",
    )(q, qi, small, k, v, ki2)


def layer_norm(x, g, b):
    mu = jnp.mean(x, -1, keepdims=True)
    var = jnp.mean(jnp.square(x - mu), -1, keepdims=True)
    return (x - mu) * lax.rsqrt(var + LN_EPS) * g + b


def rms_norm(x, g):
    return x * lax.rsqrt(jnp.mean(jnp.square(x), -1, keepdims=True) + LN_EPS) * g


def causal_dwconv(x_pad, w):
    return lax.conv_general_dilated(x_pad, w[:, None, :], window_strides=(1,), padding='VALID',
                                    dimension_numbers=('NWC', 'WIO', 'NWC'),
                                    feature_group_count=x_pad.shape[-1])


def dsa_attend(q, qi, wi, k, v, ki, q_pos, k_sel):
    f32 = jnp.float32
    idx = jnp.einsum('bthd,bsd->bths', qi, ki, preferred_element_type=f32)
    idx = jnp.einsum('bths,bth->bts', jax.nn.relu(idx), wi.astype(f32))
    k_pos = jnp.arange(k.shape[1])
    idx = jnp.where(k_pos[None, None, :] <= q_pos[None, :, None], idx, -jnp.inf)
    _, sel = lax.top_k(idx, k_sel)
    valid = sel <= q_pos[None, :, None]
    gather = jax.vmap(lambda rows, ids: rows[ids])
    kg = gather(k, sel)
    vg = gather(v, sel)
    s = jnp.einsum('btjgd,btsjd->btjgs', q, kg, preferred_element_type=f32) / math.sqrt(HEAD_DIM)
    s = jnp.where(valid[:, :, None, None, :], s, -jnp.inf)
    p = jax.nn.softmax(s, axis=-1).astype(vg.dtype)
    return jnp.einsum('btjgs,btsjd->btjgd', p, vg)


def ssd_scan(x, dt, a, bm, cm, h0):
    f32 = jnp.float32
    bsz, l, nh, hp = x.shape
    rep = nh // bm.shape[2]
    chunk = min(SSM_CHUNK, l)
    assert l % chunk == 0
    xdt = x.astype(f32) * dt[..., None]
    la = dt * a
    bh = jnp.repeat(bm.astype(f32), rep, axis=2)
    ch = jnp.repeat(cm.astype(f32), rep, axis=2)
    nc = l // chunk
    ns = bh.shape[-1]
    xdt = xdt.reshape(bsz, nc, chunk, nh, hp)
    la = la.reshape(bsz, nc, chunk, nh)
    bh = bh.reshape(bsz, nc, chunk, nh, ns)
    ch = ch.reshape(bsz, nc, chunk, nh, ns)
    cs = jnp.cumsum(la, axis=2)
    causal = jnp.tril(jnp.ones((chunk, chunk), bool))
    seg = cs[:, :, :, None, :] - cs[:, :, None, :, :]
    decay = jnp.exp(jnp.where(causal[None, None, :, :, None], seg, -jnp.inf))
    att = jnp.einsum('bcqhn,bckhn->bcqkh', ch, bh) * decay
    y_diag = jnp.einsum('bcqkh,bckhp->bcqhp', att, xdt)
    to_end = jnp.exp(cs[:, :, -1:, :] - cs)
    states = jnp.einsum('bckhn,bckh,bckhp->bchpn', bh, to_end, xdt)
    chunk_decay = jnp.exp(cs[:, :, -1, :])

    def step(hc, inp):
        dc, st = inp
        return hc * dc[:, :, None, None] + st, hc

    h_last, h_prev = lax.scan(step, h0.astype(f32), (chunk_decay.swapaxes(0, 1), states.swapaxes(0, 1)))
    h_prev = h_prev.swapaxes(0, 1)
    y_off = jnp.einsum('bcqhn,bchpn,bcqh->bcqhp', ch, h_prev, jnp.exp(cs))
    y = (y_diag + y_off).reshape(bsz, nc * chunk, nh, hp)
    return y, h_last


def token_mixer(x, p, conv_buf, ssm_buf, ssm_h0, past):
    (w_in, conv_dw, ln_conv_g, ln_conv_b, w_conv_out, w_attn_out, ssm_conv_w, ssm_conv_b,
     ssm_dt_bias, ssm_a_log, ssm_d, ssm_norm_g, w_ssm_out, w_mix_out) = p
    b, t, _ = x.shape
    proj = mm3(x, w_in)
    glu_in = proj[..., 0:1024]
    q = proj[..., 1024:1536]
    k = proj[..., 1536:1664]
    v = proj[..., 1664:1792]
    qi = proj[..., 1792:2048]
    ki = proj[..., 2048:2112]
    wi = proj[..., 2112:2116]
    dt = proj[..., 2116:2132]
    z = proj[..., 2176:3200]
    xbc = proj[..., 3200:4736]
    gates = proj[..., 4736:7808]
    glu = glu_in[..., :CONV_DIM] * jax.nn.sigmoid(glu_in[..., CONV_DIM:])
    glu_pad = jnp.concatenate([conv_buf, glu], axis=1)
    ca = layer_norm(causal_dwconv(glu_pad, conv_dw), ln_conv_g, ln_conv_b)
    y_a = mm3(jax.nn.silu(ca), w_conv_out)
    n_rep = N_HEADS // N_KV_HEADS
    if past is None:
        bf16 = jnp.bfloat16
        o = dsa_prompt(q, qi, proj[..., 2048:2176], k.astype(bf16), v.astype(bf16),
                       jnp.concatenate([ki, ki], axis=-1).astype(bf16))
        k = k.reshape(b, t, N_KV_HEADS, HEAD_DIM)
        v = v.reshape(b, t, N_KV_HEADS, HEAD_DIM)
    else:
        q = q.reshape(b, t, n_rep, N_KV_HEADS, HEAD_DIM).swapaxes(2, 3)
        k = k.reshape(b, t, N_KV_HEADS, HEAD_DIM)
        v = v.reshape(b, t, N_KV_HEADS, HEAD_DIM)
        qi = qi.reshape(b, t, IDX_HEADS, IDX_DIM)
        pk, pv, pik, pos0 = past
        k_all = jnp.concatenate([pk, k], axis=1)
        v_all = jnp.concatenate([pv, v], axis=1)
        ki_all = jnp.concatenate([pik, ki], axis=1)
        k_sel = min(TOPK_MAX, k_all.shape[1] // 4)
        o = dsa_attend(q, qi, wi, k_all, v_all, ki_all, pos0 + jnp.arange(t), k_sel)
        o = o.swapaxes(2, 3).reshape(b, t, N_HEADS * HEAD_DIM)
    y_b = mm3(o, w_attn_out)
    xbc_pad = jnp.concatenate([ssm_buf, xbc], axis=1)
    xbc_c = jax.nn.silu(causal_dwconv(xbc_pad, ssm_conv_w) + ssm_conv_b)
    xs, bm, cm = jnp.split(xbc_c, [SSM_INNER, SSM_INNER + SSM_GROUPS * SSM_STATE], axis=-1)
    xs = xs.reshape(b, t, SSM_HEADS, SSM_HEAD_DIM)
    dtf = jax.nn.softplus(dt + ssm_dt_bias)
    a = -jnp.exp(ssm_a_log)
    y, h_last = ssd_scan(xs, dtf, a, bm.reshape(b, t, SSM_GROUPS, SSM_STATE),
                         cm.reshape(b, t, SSM_GROUPS, SSM_STATE), ssm_h0)
    y = (y + ssm_d[:, None] * xs).reshape(b, t, SSM_INNER)
    y_c = mm3(rms_norm(y * jax.nn.silu(z), ssm_norm_g), w_ssm_out)
    g_a, g_b, g_c = jnp.split(jax.nn.sigmoid(gates), 3, axis=-1)
    out = mm3(g_a * y_a + g_b * y_b + g_c * y_c, w_mix_out)
    return out, k, v, ki, glu_pad[:, -(CONV_WIDTH - 1):], xbc_pad[:, -(SSM_CONV - 1):], h_last


def mem_attention(x, mk, mv, w_mq, w_mo):
    b, t, _ = x.shape
    q = mm3(x, w_mq).reshape(b, t, MEM_HEADS, MEM_HEAD_DIM)
    s = jnp.einsum('bthd,bmhd->bhtm', q, mk, preferred_element_type=jnp.float32) / math.sqrt(MEM_HEAD_DIM)
    p = jax.nn.softmax(s, axis=-1)
    o = jnp.einsum('bhtm,bmhd->bthd', p, mv)
    return mm3(o.reshape(b, t, MEM_HEADS * MEM_HEAD_DIM), w_mo)


def hier_moe(x, w_group, b_group, w_router, b_router, w_e_gate, w_e_up, w_e_down):
    b, t, d = x.shape
    xf = x.reshape(b * t, d)
    pg = jax.nn.softmax(xf @ w_group + b_group, axis=-1)
    g_prob, g_idx = lax.top_k(pg, 1)
    le = (xf @ w_router + b_router).reshape(-1, N_GROUPS, EXPERTS_PER_GROUP)
    le = jnp.take_along_axis(le, g_idx[:, :, None], axis=1)[:, 0]
    pe = jax.nn.softmax(le, axis=-1)
    e_prob, e_idx = lax.top_k(pe, TOPK_IN_GROUP)
    wts = g_prob * e_prob / jnp.sum(e_prob, -1, keepdims=True)
    expert_id = g_idx * EXPERTS_PER_GROUP + e_idx
    combine = jnp.sum(jax.nn.one_hot(expert_id, N_EXPERTS, dtype=jnp.float32) * wts[..., None], axis=1)
    hg = jnp.einsum('nd,edf->nef', xf, w_e_gate)
    hu = jnp.einsum('nd,edf->nef', xf, w_e_up)
    hid = jax.nn.silu(hg) * hu * combine[:, :, None]
    return jnp.einsum('nef,efd->nd', hid, w_e_down).reshape(b, t, d)


def _pad_w_in(w):
    sp = (0,) + SPLIT_POINTS + (w.shape[1],)
    seg = [w[:, sp[i]:sp[i + 1]] for i in range(len(SPLIT_SIZES))]
    glu, q, k, v, qi, ki, wi, z, xbc, dt, gates = seg
    pad = jnp.zeros((w.shape[0], 128 - IDX_DIM - IDX_HEADS - SSM_HEADS), w.dtype)
    n_rep = N_HEADS // N_KV_HEADS
    q = q.reshape(-1, N_KV_HEADS, n_rep, HEAD_DIM).swapaxes(1, 2).reshape(q.shape)
    return jnp.concatenate([glu, q, k, v, qi, ki, wi, dt, pad, z, xbc, gates], axis=1)


def _perm_w_attn_out(w):
    n_rep = N_HEADS // N_KV_HEADS
    return w.reshape(N_KV_HEADS, n_rep, HEAD_DIM, -1).swapaxes(0, 1).reshape(w.shape)


def kernel(x_prompt, x_sample, mem_prompt, cache_k, cache_v, cache_ik, cache_mem_k, cache_mem_v, state_conv, state_ssm_conv, state_ssm, page_table, w_in, conv_dw, ln_conv_g, ln_conv_b, w_conv_out, w_attn_out, ssm_conv_w, ssm_conv_b, ssm_dt_bias, ssm_a_log, ssm_d, ssm_norm_g, w_ssm_out, w_mix_out, ln_mix_g, ln_mix_b, w_mq, w_mk, w_mv, w_mo, ln_mem_g, ln_mem_b, w_group, b_group, w_router, b_router, w_e_gate, w_e_up, w_e_down, ln_ffn_g, ln_ffn_b):
    bf16 = jnp.bfloat16
    bp = x_prompt.shape[0]
    bs = x_sample.shape[0]
    past_len = page_table.shape[1] * PAGE_SIZE
    xp, xs = x_prompt, x_sample
    outs_p = [[] for _ in range(8)]
    outs_s = [[] for _ in range(6)]
    for l in range(DEPTH):
        mix_p = (_pad_w_in(w_in[l]).astype(bf16), conv_dw[l], ln_conv_g[l], ln_conv_b[l],
                 w_conv_out[l].astype(bf16), _perm_w_attn_out(w_attn_out[l]).astype(bf16),
                 ssm_conv_w[l], ssm_conv_b[l], ssm_dt_bias[l], ssm_a_log[l], ssm_d[l], ssm_norm_g[l],
                 w_ssm_out[l].astype(bf16), w_mix_out[l].astype(bf16))
        moe_p = (w_group[l], b_group[l], w_router[l], b_router[l], w_e_gate[l], w_e_up[l], w_e_down[l])
        w_mq_l, w_mo_l = w_mq[l].astype(bf16), w_mo[l].astype(bf16)
        mp, kp, vp, kip, cbp, sbp, hp = token_mixer(
            xp, mix_p,
            jnp.zeros((bp, CONV_WIDTH - 1, CONV_DIM), xp.dtype),
            jnp.zeros((bp, SSM_CONV - 1, SSM_XBC), xp.dtype),
            jnp.zeros((bp, SSM_HEADS, SSM_HEAD_DIM, SSM_STATE), jnp.float32),
            None)
        xp = layer_norm(DN_ALPHA * xp + mp, ln_mix_g[l], ln_mix_b[l])
        mkp = (mem_prompt @ w_mk[l]).reshape(bp, N_MEM, MEM_HEADS, MEM_HEAD_DIM)
        mvp = (mem_prompt @ w_mv[l]).reshape(bp, N_MEM, MEM_HEADS, MEM_HEAD_DIM)
        xp = layer_norm(DN_ALPHA * xp + mem_attention(xp, mkp, mvp, w_mq_l, w_mo_l), ln_mem_g[l], ln_mem_b[l])
        xp = layer_norm(DN_ALPHA * xp + hier_moe(xp, *moe_p), ln_ffn_g[l], ln_ffn_b[l])
        pk = cache_k[page_table, l].reshape(bs, past_len, N_KV_HEADS, HEAD_DIM)
        pv = cache_v[page_table, l].reshape(bs, past_len, N_KV_HEADS, HEAD_DIM)
        pik = cache_ik[page_table, l].reshape(bs, past_len, IDX_DIM)
        ms, ks_new, vs_new, kis, cbs, sbs, hs = token_mixer(
            xs, mix_p, state_conv[:, l], state_ssm_conv[:, l], state_ssm[:, l], (pk, pv, pik, past_len))
        xs = layer_norm(DN_ALPHA * xs + ms, ln_mix_g[l], ln_mix_b[l])
        xs = layer_norm(DN_ALPHA * xs + mem_attention(xs, cache_mem_k[:, l], cache_mem_v[:, l], w_mq_l, w_mo_l),
                        ln_mem_g[l], ln_mem_b[l])
        xs = layer_norm(DN_ALPHA * xs + hier_moe(xs, *moe_p), ln_ffn_g[l], ln_ffn_b[l])
        for lst, arr in zip(outs_p, (kp, vp, kip, mkp, mvp, cbp, sbp, hp)):
            lst.append(arr)
        for lst, arr in zip(outs_s, (ks_new, vs_new, kis, cbs, sbs, hs)):
            lst.append(arr)
    p_k, p_v, p_ik, p_mem_k, p_mem_v, p_conv, p_ssm_conv, p_ssm = [jnp.stack(a, axis=1) for a in outs_p]
    s_k, s_v, s_ik, s_conv, s_ssm_conv, s_ssm = [jnp.stack(a, axis=1) for a in outs_s]
    return (xp, xs, p_k, p_v, p_ik, p_mem_k, p_mem_v, p_conv, p_ssm_conv, p_ssm,
            s_k, s_v, s_ik, s_conv, s_ssm_conv, s_ssm)
```

```python
import functools
import math

import jax
import jax.numpy as jnp
from jax import lax
from jax.experimental import pallas as pl
from jax.experimental.pallas import tpu as pltpu

D_MODEL = 1024
DEPTH = 2
PAGE_SIZE = 128
CONV_DIM = 512
CONV_WIDTH = 31
N_HEADS = 8
N_KV_HEADS = 2
HEAD_DIM = 64
IDX_HEADS = 4
IDX_DIM = 64
TOPK_MAX = 256
Q_BLOCK = 128
SSM_HEADS = 16
SSM_HEAD_DIM = 64
SSM_INNER = SSM_HEADS * SSM_HEAD_DIM
SSM_GROUPS = 2
SSM_STATE = 128
SSM_CONV = 4
SSM_CHUNK = 128
SSM_XBC = SSM_INNER + 2 * SSM_GROUPS * SSM_STATE
N_MEM = 256
MEM_HEADS = 4
MEM_HEAD_DIM = 128
N_GROUPS = 4
EXPERTS_PER_GROUP = 4
N_EXPERTS = N_GROUPS * EXPERTS_PER_GROUP
TOPK_IN_GROUP = 2
D_EXPERT = 512
DN_ALPHA = (2 * DEPTH) ** 0.25
LN_EPS = 1e-5

SPLIT_SIZES = (2 * CONV_DIM, N_HEADS * HEAD_DIM, N_KV_HEADS * HEAD_DIM, N_KV_HEADS * HEAD_DIM,
               IDX_HEADS * IDX_DIM, IDX_DIM, IDX_HEADS, SSM_INNER, SSM_XBC, SSM_HEADS, 3 * D_MODEL)
SPLIT_POINTS = tuple(sum(SPLIT_SIZES[:i + 1]) for i in range(len(SPLIT_SIZES) - 1))

VMEM_LIMIT_BYTES = 56 * 1024 * 1024
MM_COL_CHUNK = 512


def _mm_kernel(x_ref, w_ref, o_ref):
    xb = x_ref[...].astype(jnp.bfloat16)
    n = o_ref.shape[1]
    for c0 in range(0, n, MM_COL_CHUNK):
        c1 = min(n, c0 + MM_COL_CHUNK)
        o_ref[:, c0:c1] = jnp.dot(xb, w_ref[:, c0:c1], preferred_element_type=jnp.float32)


def mm(x, w, tm=256):
    m, k = x.shape
    n = w.shape[1]
    tm = min(tm, m)
    assert m % tm == 0 and n % 128 == 0 and k % 128 == 0
    return pl.pallas_call(
        _mm_kernel,
        grid=(m // tm,),
        in_specs=[pl.BlockSpec((tm, k), lambda i: (i, 0)),
                  pl.BlockSpec((k, n), lambda i: (0, 0), pipeline_mode=pl.Buffered(1))],
        out_specs=pl.BlockSpec((tm, n), lambda i: (i, 0)),
        out_shape=jax.ShapeDtypeStruct((m, n), jnp.float32),
        compiler_params=pltpu.CompilerParams(dimension_semantics=("parallel",),
                                             vmem_limit_bytes=VMEM_LIMIT_BYTES),
    )(x, w)


def mm3(x, w, tm=256):
    b, t, d = x.shape
    return mm(x.reshape(b * t, d), w, tm).reshape(b, t, w.shape[1])


INT32_MIN = -2 ** 31
MASK_BIAS = -1e30
DSA_KEY_CHUNK = 512
LANES = 128


def _nt_dot(a, b):
    return lax.dot_general(a, b, (((1,), (1,)), ((), ())), preferred_element_type=jnp.float32)


def _dsa_prompt_kernel(q_ref, qi_ref, sm_ref, k_ref, v_ref, ki_ref, o_ref,
                       key_scr, s_scr, p_scr, qpad_scr, qis_scr, m_scr, l_scr, acc_scr, *, k_sel):
    f32, i32, bf16 = jnp.float32, jnp.int32, jnp.bfloat16
    qb = q_ref.shape[1]
    kc_w = DSA_KEY_CHUNK
    sub = kc_w // LANES
    i = pl.program_id(1)
    n_chunks = (i * qb + qb + kc_w - 1) // kc_w
    lane = lax.broadcasted_iota(i32, (qb, LANES), 1)
    half = [lane < HEAD_DIM, lane >= HEAD_DIM]

    for pair in range(IDX_HEADS // 2):
        src = qi_ref[0, :, pair * LANES:(pair + 1) * LANES]
        hi = src.astype(bf16).astype(f32)
        lo_swapped = pltpu.roll(src - hi, IDX_DIM, axis=1)
        for hh in range(2):
            h = 2 * pair + hh
            qis_scr[h * qb:(h + 1) * qb, 0:LANES] = jnp.where(half[hh], hi, lo_swapped).astype(bf16)
            qis_scr[h * qb:(h + 1) * qb, LANES:2 * LANES] = jnp.where(half[hh], hi, 0.0).astype(bf16)
    w_idx = [sm_ref[0, :, IDX_DIM + h:IDX_DIM + h + 1] for h in range(IDX_HEADS)]
    row_pos = i * qb + lax.broadcasted_iota(i32, (qb, kc_w), 0)
    col_iota = lax.broadcasted_iota(i32, (qb, kc_w), 1)

    def score_body(c, carry):
        off = pl.multiple_of(c * kc_w, kc_w)
        d = _nt_dot(qis_scr[...], ki_ref[0, pl.ds(off, kc_w), :])
        idx = w_idx[0] * jnp.maximum(d[0:qb], 0.0)
        for h in range(1, IDX_HEADS):
            idx = idx + w_idx[h] * jnp.maximum(d[h * qb:(h + 1) * qb], 0.0)
        idx = jnp.where(idx == 0.0, 0.0, idx)
        bits = lax.bitcast_convert_type(idx, i32)
        key = bits ^ ((bits >> 31) & 0x7FFFFFFF)
        key_scr[c] = jnp.where(off + col_iota <= row_pos, key, INT32_MIN)
        return carry

    lax.fori_loop(0, n_chunks, score_body, 0)

    def count(pred):
        def body(c, acc):
            for g in range(sub):
                acc = acc + jnp.where(pred(key_scr[c, :, g * LANES:(g + 1) * LANES]), 1.0, 0.0)
            return acc
        acc = lax.fori_loop(0, n_chunks, body, jnp.zeros((qb, LANES), f32))
        return jnp.sum(acc, axis=1, keepdims=True)

    def search_body(it, lo):
        cand = lo + lax.shift_left(jnp.int32(1), 31 - it)
        cand_b = jnp.broadcast_to(cand, (qb, LANES))
        cnt = count(lambda kk: kk >= cand_b)
        return jnp.where(cnt >= float(k_sel), cand, lo)

    thr = lax.fori_loop(0, 32, search_body, jnp.full((qb, 1), INT32_MIN, i32))
    thr_b = jnp.broadcast_to(thr, (qb, LANES))
    need_b = jnp.broadcast_to(float(k_sel) - count(lambda kk: kk > thr_b), (qb, LANES))

    tri = jnp.where(lax.broadcasted_iota(i32, (LANES, LANES), 0) <= lax.broadcasted_iota(i32, (LANES, LANES), 1),
                    1.0, 0.0).astype(bf16)
    zero_bits = jnp.zeros((qb, LANES), i32)
    mask_bits = lax.bitcast_convert_type(jnp.full((qb, LANES), MASK_BIAS, f32), i32)

    def select_body(c, seen):
        for g in range(sub):
            kk = key_scr[c, :, g * LANES:(g + 1) * LANES]
            eq = kk == thr_b
            rank = seen + jnp.dot(jnp.where(eq, 1.0, 0.0).astype(bf16), tri, preferred_element_type=f32)
            sel = ((kk > thr_b) | (eq & (rank <= need_b))) & (kk > INT32_MIN)
            key_scr[c, :, g * LANES:(g + 1) * LANES] = jnp.where(sel, zero_bits, mask_bits)
            seen = jnp.broadcast_to(rank[:, LANES - 1:LANES], (qb, LANES))
        return seen

    lax.fori_loop(0, n_chunks, select_body, jnp.zeros((qb, LANES), f32))

    n_blk = N_HEADS
    scale = 1.0 / math.sqrt(HEAD_DIM)
    for g in range(N_HEADS // N_KV_HEADS):
        src = q_ref[0, :, g * LANES:(g + 1) * LANES] * scale
        for j in range(N_KV_HEADS):
            r = g * N_KV_HEADS + j
            qpad_scr[r * qb:(r + 1) * qb, :] = jnp.where(half[j], src, 0.0).astype(bf16)
    m_scr[...] = jnp.full(m_scr.shape, -jnp.inf, f32)
    l_scr[...] = jnp.zeros(l_scr.shape, f32)
    acc_scr[...] = jnp.zeros(acc_scr.shape, f32)

    def attend_body(c, carry):
        off = pl.multiple_of(c * kc_w, kc_w)
        s_scr[...] = _nt_dot(qpad_scr[...], k_ref[0, pl.ds(off, kc_w), :])
        bias = lax.bitcast_convert_type(key_scr[c], f32)
        for r in range(n_blk):
            rows = slice(r * qb, (r + 1) * qb)
            s = s_scr[rows, :] + bias
            m_old = m_scr[rows, :]
            m_new = jnp.maximum(m_old, jnp.broadcast_to(jnp.max(s, axis=1, keepdims=True), (qb, LANES)))
            alpha = jnp.exp(m_old - m_new)
            p = jnp.exp(s - jnp.tile(m_new, (1, sub)))
            l_scr[rows, :] = alpha * l_scr[rows, :] + jnp.broadcast_to(jnp.sum(p, axis=1, keepdims=True), (qb, LANES))
            acc_scr[rows, :] = alpha * acc_scr[rows, :]
            p_scr[rows, :] = p.astype(bf16)
            m_scr[rows, :] = m_new
        acc_scr[...] += jnp.dot(p_scr[...], v_ref[0, pl.ds(off, kc_w), :], preferred_element_type=f32)
        return carry

    lax.fori_loop(0, n_chunks, attend_body, 0)

    for g in range(N_HEADS // N_KV_HEADS):
        outs = []
        for j in range(N_KV_HEADS):
            rows = slice((g * N_KV_HEADS + j) * qb, (g * N_KV_HEADS + j + 1) * qb)
            outs.append(acc_scr[rows, :] / l_scr[rows, :])
        o_ref[0, :, g * LANES:(g + 1) * LANES] = jnp.where(half[0], outs[0], outs[1])


def _split_bf16(x):
    hi = x.astype(jnp.bfloat16)
    return hi, (x - hi.astype(jnp.float32)).astype(jnp.bfloat16)


def dsa_prompt(proj, k, v, ki):
    b, t, _ = proj.shape
    qb = Q_BLOCK
    w_q, w_qi = N_HEADS * HEAD_DIM, IDX_HEADS * IDX_DIM
    assert t % DSA_KEY_CHUNK == 0 and t % qb == 0
    assert IDX_DIM == HEAD_DIM == LANES // 2 and IDX_HEADS % 2 == 0
    k_sel = min(TOPK_MAX, t // 4)
    n_rows = N_HEADS * qb
    ki_hi, ki_lo = _split_bf16(ki)
    ki4 = jnp.concatenate([ki_hi, ki_hi, ki_lo, ki_lo], axis=-1)
    k, v = k.astype(jnp.bfloat16), v.astype(jnp.bfloat16)
    seq_spec = pl.BlockSpec((1, t, LANES), lambda bi, i: (bi, 0, 0))
    return pl.pallas_call(
        functools.partial(_dsa_prompt_kernel, k_sel=k_sel),
        grid=(b, t // qb),
        in_specs=[pl.BlockSpec((1, qb, w_q), lambda bi, i: (bi, i, COL_Q // w_q)),
                  pl.BlockSpec((1, qb, w_qi), lambda bi, i: (bi, i, COL_QI // w_qi)),
                  pl.BlockSpec((1, qb, LANES), lambda bi, i: (bi, i, COL_SMALL // LANES)),
                  seq_spec, seq_spec, pl.BlockSpec((1, t, 2 * LANES), lambda bi, i: (bi, 0, 0))],
        out_specs=pl.BlockSpec((1, qb, N_HEADS * HEAD_DIM), lambda bi, i: (bi, i, 0)),
        out_shape=jax.ShapeDtypeStruct((b, t, N_HEADS * HEAD_DIM), jnp.float32),
        scratch_shapes=[pltpu.VMEM((t // DSA_KEY_CHUNK, qb, DSA_KEY_CHUNK), jnp.int32),
                        pltpu.VMEM((n_rows, DSA_KEY_CHUNK), jnp.float32),
                        pltpu.VMEM((n_rows, DSA_KEY_CHUNK), jnp.bfloat16),
                        pltpu.VMEM((n_rows, LANES), jnp.bfloat16),
                        pltpu.VMEM((IDX_HEADS * qb, 2 * LANES), jnp.bfloat16),
                        pltpu.VMEM((n_rows, LANES), jnp.float32),
                        pltpu.VMEM((n_rows, LANES), jnp.float32),
                        pltpu.VMEM((n_rows, LANES), jnp.float32)],
        compiler_params=pltpu.CompilerParams(dimension_semantics=("parallel", "arbitrary"),
                                             vmem_limit_bytes=VMEM_LIMIT_BYTES),
        name="dsa_prompt",
    )(proj, proj, proj, k, v, ki4)


DEC_ROWS = 8


def _dsa_decode_kernel(pt_ref, q_ref, qi_ref, sm_ref, kn_ref, vn_ref, *rest, n_pages, k_sel):
    f32, i32, bf16 = jnp.float32, jnp.int32, jnp.bfloat16
    k_pages = rest[0:n_pages]
    v_pages = rest[n_pages:2 * n_pages]
    ik_pages = rest[2 * n_pages:3 * n_pages]
    o_ref = rest[3 * n_pages]
    kall, vall, ikall, qis, wpad, qpad, key_scr, p_scr = rest[3 * n_pages + 1:]
    t_new = q_ref.shape[1]
    past = n_pages * PAGE_SIZE
    s_pad = past + PAGE_SIZE
    n_chunks = s_pad // LANES
    rows = DEC_ROWS

    for p in range(n_pages):
        sl = slice(p * PAGE_SIZE, (p + 1) * PAGE_SIZE)
        kall[sl, :] = k_pages[p][...]
        vall[sl, :] = v_pages[p][...]
        ikall[sl, :] = ik_pages[p][...]
    tail = slice(past, s_pad)
    kall[tail, :] = jnp.zeros((PAGE_SIZE, LANES), f32)
    vall[tail, :] = jnp.zeros((PAGE_SIZE, LANES), f32)
    ikall[tail, :] = jnp.zeros((PAGE_SIZE, IDX_DIM), f32)
    kall[past:past + t_new, :] = kn_ref[0]
    vall[past:past + t_new, :] = vn_ref[0]
    ikall[past:past + t_new, :] = sm_ref[0, :, 0:IDX_DIM]

    qis[...] = jnp.zeros(qis.shape, f32)
    wpad[...] = jnp.zeros(wpad.shape, f32)
    for h in range(IDX_HEADS):
        qis[h * rows:h * rows + t_new, :] = qi_ref[0, :, h * IDX_DIM:(h + 1) * IDX_DIM]
    wpad[0:t_new, :] = sm_ref[0]
    q_hi, q_lo = _split_bf16(qis[...])
    k_hi, k_lo = _split_bf16(ikall[...])
    d = _nt_dot(q_hi, k_hi) + _nt_dot(q_lo, k_hi) + _nt_dot(q_hi, k_lo)
    idx = wpad[:, IDX_DIM:IDX_DIM + 1] * jnp.maximum(d[0:rows], 0.0)
    for h in range(1, IDX_HEADS):
        idx = idx + wpad[:, IDX_DIM + h:IDX_DIM + h + 1] * jnp.maximum(d[h * rows:(h + 1) * rows], 0.0)
    idx = jnp.where(idx == 0.0, 0.0, idx)
    bits = lax.bitcast_convert_type(idx, i32)
    key = bits ^ ((bits >> 31) & 0x7FFFFFFF)
    col = lax.broadcasted_iota(i32, (rows, s_pad), 1)
    q_pos = past + lax.broadcasted_iota(i32, (rows, s_pad), 0)
    key_scr[...] = jnp.where((col <= q_pos) & (col < past + t_new), key, INT32_MIN)

    def search_body(it, lo):
        cand = lo + lax.shift_left(jnp.int32(1), 31 - it)
        cnt = jnp.sum(jnp.where(key_scr[...] >= cand, 1.0, 0.0), axis=1, keepdims=True)
        return jnp.where(cnt >= float(k_sel), cand, lo)

    thr = lax.fori_loop(0, 32, search_body, jnp.full((rows, 1), INT32_MIN, i32))
    need = float(k_sel) - jnp.sum(jnp.where(key_scr[...] > thr, 1.0, 0.0), axis=1, keepdims=True)
    tri = jnp.where(lax.broadcasted_iota(i32, (LANES, LANES), 0) <= lax.broadcasted_iota(i32, (LANES, LANES), 1),
                    1.0, 0.0).astype(bf16)
    seen = jnp.zeros((rows, 1), f32)
    for c in range(n_chunks):
        kk = key_scr[:, c * LANES:(c + 1) * LANES]
        eq = kk == thr
        rank = seen + jnp.dot(jnp.where(eq, 1.0, 0.0).astype(bf16), tri, preferred_element_type=f32)
        sel = ((kk > thr) | (eq & (rank <= need))) & (kk > INT32_MIN)
        key_scr[:, c * LANES:(c + 1) * LANES] = lax.bitcast_convert_type(jnp.where(sel, 0.0, MASK_BIAS), i32)
        seen = rank[:, LANES - 1:LANES]
    bias = lax.bitcast_convert_type(key_scr[...], f32)

    lane = lax.broadcasted_iota(i32, (t_new, LANES), 1)
    half = [lane < HEAD_DIM, lane >= HEAD_DIM]
    scale = 1.0 / math.sqrt(HEAD_DIM)
    qpad[...] = jnp.zeros(qpad.shape, f32)
    for g in range(N_HEADS // N_KV_HEADS):
        src = q_ref[0, :, g * LANES:(g + 1) * LANES] * scale
        for j in range(N_KV_HEADS):
            r = g * N_KV_HEADS + j
            qpad[r * rows:r * rows + t_new, :] = jnp.where(half[j], src, 0.0)
    s_all = _nt_dot(qpad[...].astype(bf16), kall[...].astype(bf16))
    for r in range(N_HEADS):
        s = s_all[r * rows:(r + 1) * rows] + bias
        p = jnp.exp(s - jnp.max(s, axis=1, keepdims=True))
        p_scr[r * rows:(r + 1) * rows, :] = p / jnp.sum(p, axis=1, keepdims=True)
    o_all = jnp.dot(p_scr[...].astype(bf16), vall[...].astype(bf16), preferred_element_type=f32)
    for g in range(N_HEADS // N_KV_HEADS):
        r0, r1 = g * N_KV_HEADS * rows, (g * N_KV_HEADS + 1) * rows
        o_ref[0, :, g * LANES:(g + 1) * LANES] = jnp.where(half[0], o_all[r0:r0 + t_new], o_all[r1:r1 + t_new])


def dsa_decode(proj, k_new, v_new, cache_k, cache_v, cache_ik, page_table, layer):
    b, t_new, _ = proj.shape
    n_pages = page_table.shape[1]
    assert t_new <= DEC_ROWS and PAGE_SIZE == LANES
    s_pad = (n_pages + 1) * PAGE_SIZE
    k_sel = min(TOPK_MAX, (n_pages * PAGE_SIZE + t_new) // 4)

    def tok_spec(width, col=0):
        return pl.BlockSpec((1, t_new, width), lambda bi, pt: (bi, 0, col // width))

    def page_spec(width, p):
        return pl.BlockSpec((None, None, PAGE_SIZE, width), lambda bi, pt, p=p: (pt[bi, p], layer, 0, 0))

    grid_spec = pltpu.PrefetchScalarGridSpec(
        num_scalar_prefetch=1,
        grid=(b,),
        in_specs=[tok_spec(N_HEADS * HEAD_DIM, COL_Q), tok_spec(IDX_HEADS * IDX_DIM, COL_QI),
                  tok_spec(LANES, COL_SMALL), tok_spec(LANES), tok_spec(LANES)]
                 + [page_spec(LANES, p) for p in range(n_pages)]
                 + [page_spec(LANES, p) for p in range(n_pages)]
                 + [page_spec(IDX_DIM, p) for p in range(n_pages)],
        out_specs=tok_spec(N_HEADS * HEAD_DIM),
        scratch_shapes=[pltpu.VMEM((s_pad, LANES), jnp.float32),
                        pltpu.VMEM((s_pad, LANES), jnp.float32),
                        pltpu.VMEM((s_pad, IDX_DIM), jnp.float32),
                        pltpu.VMEM((IDX_HEADS * DEC_ROWS, IDX_DIM), jnp.float32),
                        pltpu.VMEM((DEC_ROWS, LANES), jnp.float32),
                        pltpu.VMEM((N_HEADS * DEC_ROWS, LANES), jnp.float32),
                        pltpu.VMEM((DEC_ROWS, s_pad), jnp.int32),
                        pltpu.VMEM((N_HEADS * DEC_ROWS, s_pad), jnp.float32)])
    return pl.pallas_call(
        functools.partial(_dsa_decode_kernel, n_pages=n_pages, k_sel=k_sel),
        grid_spec=grid_spec,
        out_shape=jax.ShapeDtypeStruct((b, t_new, N_HEADS * HEAD_DIM), jnp.float32),
        compiler_params=pltpu.CompilerParams(dimension_semantics=("arbitrary",),
                                             vmem_limit_bytes=VMEM_LIMIT_BYTES),
        name="dsa_decode",
    )(page_table, proj, proj, proj, k_new, v_new,
      *([cache_k] * n_pages), *([cache_v] * n_pages), *([cache_ik] * n_pages))


COL_GATES, COL_XBC, COL_Q, COL_GLU, COL_Z, COL_QI, COL_K, COL_V, COL_SMALL, D_IN_PAD = (
    0, 3072, 4608, 5120, 6144, 7168, 7424, 7552, 7680, 7808)
ROW_BLOCK = 256


def _ln_rows(x, g, b):
    mu = jnp.mean(x, axis=-1, keepdims=True)
    xc = x - mu
    var = jnp.mean(xc * xc, axis=-1, keepdims=True)
    return xc * lax.rsqrt(var + LN_EPS) * g + b


def _bdot(a, w_ref):
    return jnp.dot(a.astype(jnp.bfloat16), w_ref[...], preferred_element_type=jnp.float32)


def _branch_mix_kernel(cv_ref, o_ref, y_ref, z_ref, gates_ref, x_ref, lncg_ref, lncb_ref, ng_ref,
                       wc_ref, wa_ref, ws_ref, wm_ref, lng_ref, lnb_ref, out_ref):
    ca = _ln_rows(cv_ref[...], lncg_ref[...], lncb_ref[...])
    y_a = _bdot(ca * jax.nn.sigmoid(ca), wc_ref)
    y_b = _bdot(o_ref[...], wa_ref)
    z = z_ref[...]
    t = y_ref[...] * (z * jax.nn.sigmoid(z))
    t = t * lax.rsqrt(jnp.mean(t * t, axis=-1, keepdims=True) + LN_EPS) * ng_ref[...]
    y_c = _bdot(t, ws_ref)
    d = D_MODEL
    mix = (jax.nn.sigmoid(gates_ref[:, 0:d]) * y_a + jax.nn.sigmoid(gates_ref[:, d:2 * d]) * y_b
           + jax.nn.sigmoid(gates_ref[:, 2 * d:3 * d]) * y_c)
    out_ref[...] = _ln_rows(DN_ALPHA * x_ref[...] + _bdot(mix, wm_ref), lng_ref[...], lnb_ref[...])


def branch_mix_ln(cv, o, y, proj, x, lncg, lncb, ng, wc, wa, ws, wm, lng, lnb):
    m = x.shape[0]
    tm = min(ROW_BLOCK, m)
    assert m % tm == 0
    d = D_MODEL

    def rows(width, col_block=0):
        return pl.BlockSpec((tm, width), lambda i, cb=col_block: (i, cb))

    def whole(a):
        return pl.BlockSpec(a.shape, lambda i: (0,) * a.ndim, pipeline_mode=pl.Buffered(1))

    vecs = [a.reshape(1, -1) for a in (lncg, lncb, ng)]
    lnv = [a.reshape(1, -1) for a in (lng, lnb)]
    return pl.pallas_call(
        _branch_mix_kernel,
        grid=(m // tm,),
        in_specs=[rows(CONV_DIM), rows(N_HEADS * HEAD_DIM), rows(SSM_INNER), rows(SSM_INNER, COL_Z // SSM_INNER),
                  rows(3 * d, COL_GATES // (3 * d)), rows(d)]
                 + [whole(a) for a in vecs] + [whole(a) for a in (wc, wa, ws, wm)] + [whole(a) for a in lnv],
        out_specs=rows(d),
        out_shape=jax.ShapeDtypeStruct((m, d), jnp.float32),
        compiler_params=pltpu.CompilerParams(dimension_semantics=("parallel",), vmem_limit_bytes=VMEM_LIMIT_BYTES),
        name="branch_mix_ln",
    )(cv, o, y, proj, proj, x, *vecs, wc, wa, ws, wm, *lnv)


def _mem_attn_kernel(x_ref, mk_ref, mv_ref, wq_ref, wo_ref, lng_ref, lnb_ref, out_ref, o_scr, *, seqs, t_seq):
    bf16 = jnp.bfloat16
    x = x_ref[...]
    q = _bdot(x, wq_ref)
    scale = 1.0 / math.sqrt(MEM_HEAD_DIM)
    for s in range(seqs):
        qs = q[s * t_seq:(s + 1) * t_seq].astype(bf16)
        for h in range(MEM_HEADS):
            cols = slice(h * MEM_HEAD_DIM, (h + 1) * MEM_HEAD_DIM)
            sc = _nt_dot(qs[:, cols], mk_ref[s, :, cols].astype(bf16)) * scale
            p = jnp.exp(sc - jnp.max(sc, axis=-1, keepdims=True))
            p = p / jnp.sum(p, axis=-1, keepdims=True)
            o_scr[s * t_seq:(s + 1) * t_seq, cols] = jnp.dot(p.astype(bf16), mv_ref[s, :, cols].astype(bf16),
                                                              preferred_element_type=jnp.float32)
    out_ref[...] = _ln_rows(DN_ALPHA * x + _bdot(o_scr[...], wo_ref), lng_ref[...], lnb_ref[...])


def mem_attn_ln(x, mk, mv, wq, wo, lng, lnb, t_seq, layer=None):
    m, d = x.shape
    hd = MEM_HEADS * MEM_HEAD_DIM
    if t_seq >= ROW_BLOCK:
        seqs, tm = 1, ROW_BLOCK
        assert t_seq % tm == 0
        per_seq = t_seq // tm
        seq_of = lambda i: i // per_seq
    else:
        seqs = max(1, 32 // t_seq)
        tm = seqs * t_seq
        assert m % tm == 0
        seq_of = lambda i: i
    if layer is None:
        mem_spec = pl.BlockSpec((seqs, N_MEM, hd), lambda i: (seq_of(i), 0, 0))
    else:
        mem_spec = pl.BlockSpec((seqs, None, N_MEM, hd), lambda i: (seq_of(i), layer, 0, 0))

    def whole(a):
        return pl.BlockSpec(a.shape, lambda i: (0,) * a.ndim, pipeline_mode=pl.Buffered(1))

    lnv = [a.reshape(1, -1) for a in (lng, lnb)]
    return pl.pallas_call(
        functools.partial(_mem_attn_kernel, seqs=seqs, t_seq=min(t_seq, tm)),
        grid=(m // tm,),
        in_specs=[pl.BlockSpec((tm, d), lambda i: (i, 0)), mem_spec, mem_spec, whole(wq), whole(wo)]
                 + [whole(a) for a in lnv],
        out_specs=pl.BlockSpec((tm, d), lambda i: (i, 0)),
        out_shape=jax.ShapeDtypeStruct((m, d), jnp.float32),
        scratch_shapes=[pltpu.VMEM((tm, hd), jnp.float32)],
        compiler_params=pltpu.CompilerParams(dimension_semantics=("parallel",), vmem_limit_bytes=VMEM_LIMIT_BYTES),
        name="mem_attn_ln",
    )(x, mk, mv, wq, wo, *lnv)


MOE_ROW_BLOCK = 1024


def _moe_kernel(x_ref, wr_hi_ref, wr_lo_ref, br_ref, wg_ref, wu_ref, wd_ref, lng_ref, lnb_ref, out_ref,
                xb_scr, comb_scr, acc_scr):
    f32 = jnp.float32
    e = pl.program_id(1)
    tm = x_ref.shape[0]
    lane = lax.broadcasted_iota(jnp.int32, (tm, LANES), 1).astype(f32)

    @pl.when(e == 0)
    def _route():
        x_hi, x_lo = _split_bf16(x_ref[...])
        xb_scr[...] = x_hi
        lg = (jnp.dot(x_hi, wr_hi_ref[...], preferred_element_type=f32)
              + jnp.dot(x_lo, wr_hi_ref[...], preferred_element_type=f32)
              + jnp.dot(x_hi, wr_lo_ref[...], preferred_element_type=f32) + br_ref[...])
        is_g = lane < N_GROUPS
        mg = jnp.max(jnp.where(is_g, lg, -jnp.inf), axis=1, keepdims=True)
        g_prob = 1.0 / jnp.sum(jnp.where(is_g, jnp.exp(lg - mg), 0.0), axis=1, keepdims=True)
        g_idx = jnp.min(jnp.where(is_g & (lg == mg), lane, float(LANES)), axis=1, keepdims=True)
        lo_e = N_GROUPS + EXPERTS_PER_GROUP * g_idx
        is_e = (lane >= lo_e) & (lane < lo_e + EXPERTS_PER_GROUP)
        me = jnp.max(jnp.where(is_e, lg, -jnp.inf), axis=1, keepdims=True)
        ee = jnp.where(is_e, jnp.exp(lg - me), 0.0)
        pe = jnp.where(is_e, ee / jnp.sum(ee, axis=1, keepdims=True), -1.0)
        p1 = jnp.max(pe, axis=1, keepdims=True)
        first = jnp.min(jnp.where(pe == p1, lane, float(LANES)), axis=1, keepdims=True)
        pe2 = jnp.where(lane == first, -1.0, pe)
        p2 = jnp.max(pe2, axis=1, keepdims=True)
        second = jnp.min(jnp.where(pe2 == p2, lane, float(LANES)), axis=1, keepdims=True)
        norm = g_prob / (p1 + p2)
        comb_scr[...] = jnp.where(lane == first, p1 * norm, jnp.where(lane == second, p2 * norm, 0.0))
        acc_scr[...] = jnp.zeros(acc_scr.shape, f32)

    xb = xb_scr[...]
    hg = jnp.dot(xb, wg_ref[0], preferred_element_type=f32)
    hu = jnp.dot(xb, wu_ref[0], preferred_element_type=f32)
    c = jnp.sum(jnp.where(lane == (e + N_GROUPS).astype(f32), comb_scr[...], 0.0), axis=1, keepdims=True)
    hid = hg * jax.nn.sigmoid(hg) * hu * c
    acc_scr[...] += jnp.dot(hid.astype(jnp.bfloat16), wd_ref[0], preferred_element_type=f32)

    @pl.when(e == pl.num_programs(1) - 1)
    def _finish():
        out_ref[...] = _ln_rows(DN_ALPHA * x_ref[...] + acc_scr[...], lng_ref[...], lnb_ref[...])


def moe_ln(x, w_group, b_group, w_router, b_router, wg, wu, wd, lng, lnb):
    m, d = x.shape
    tm = min(MOE_ROW_BLOCK, m)
    assert m % tm == 0 and N_GROUPS + N_EXPERTS <= LANES
    pad = jnp.zeros((d, LANES - N_GROUPS - N_EXPERTS), jnp.float32)
    wr_hi, wr_lo = _split_bf16(jnp.concatenate([w_group, w_router, pad], axis=1))
    br = jnp.concatenate([b_group, b_router, pad[0]]).reshape(1, LANES)
    lnv = [a.reshape(1, -1) for a in (lng, lnb)]

    def whole(a):
        return pl.BlockSpec(a.shape, lambda i, e: (0,) * a.ndim, pipeline_mode=pl.Buffered(1))

    return pl.pallas_call(
        _moe_kernel,
        grid=(m // tm, N_EXPERTS),
        in_specs=[pl.BlockSpec((tm, d), lambda i, e: (i, 0)), whole(wr_hi), whole(wr_lo), whole(br),
                  pl.BlockSpec((1, d, D_EXPERT), lambda i, e: (e, 0, 0)),
                  pl.BlockSpec((1, d, D_EXPERT), lambda i, e: (e, 0, 0)),
                  pl.BlockSpec((1, D_EXPERT, d), lambda i, e: (e, 0, 0))] + [whole(a) for a in lnv],
        out_specs=pl.BlockSpec((tm, d), lambda i, e: (i, 0)),
        out_shape=jax.ShapeDtypeStruct((m, d), jnp.float32),
        scratch_shapes=[pltpu.VMEM((tm, d), jnp.bfloat16), pltpu.VMEM((tm, LANES), jnp.float32),
                        pltpu.VMEM((tm, d), jnp.float32)],
        compiler_params=pltpu.CompilerParams(dimension_semantics=("parallel", "arbitrary"),
                                             vmem_limit_bytes=VMEM_LIMIT_BYTES),
        name="moe_ln",
    )(x, wr_hi, wr_lo, br, wg, wu, wd, *lnv)


def causal_dwconv(x_pad, w):
    return lax.conv_general_dilated(x_pad, w[:, None, :], window_strides=(1,), padding='VALID',
                                    dimension_numbers=('NWC', 'WIO', 'NWC'),
                                    feature_group_count=x_pad.shape[-1])


def ssd_scan(x, dt, a, bm, cm, h0):
    f32 = jnp.float32
    bsz, l, nh, hp = x.shape
    rep = nh // bm.shape[2]
    chunk = min(SSM_CHUNK, l)
    assert l % chunk == 0
    xdt = x.astype(f32) * dt[..., None]
    la = dt * a
    bh = jnp.repeat(bm.astype(f32), rep, axis=2)
    ch = jnp.repeat(cm.astype(f32), rep, axis=2)
    nc = l // chunk
    ns = bh.shape[-1]
    xdt = xdt.reshape(bsz, nc, chunk, nh, hp)
    la = la.reshape(bsz, nc, chunk, nh)
    bh = bh.reshape(bsz, nc, chunk, nh, ns)
    ch = ch.reshape(bsz, nc, chunk, nh, ns)
    cs = jnp.cumsum(la, axis=2)
    causal = jnp.tril(jnp.ones((chunk, chunk), bool))
    seg = cs[:, :, :, None, :] - cs[:, :, None, :, :]
    decay = jnp.exp(jnp.where(causal[None, None, :, :, None], seg, -jnp.inf))
    att = jnp.einsum('bcqhn,bckhn->bcqkh', ch, bh) * decay
    y_diag = jnp.einsum('bcqkh,bckhp->bcqhp', att, xdt)
    to_end = jnp.exp(cs[:, :, -1:, :] - cs)
    states = jnp.einsum('bckhn,bckh,bckhp->bchpn', bh, to_end, xdt)
    chunk_decay = jnp.exp(cs[:, :, -1, :])

    def step(hc, inp):
        dc, st = inp
        return hc * dc[:, :, None, None] + st, hc

    h_last, h_prev = lax.scan(step, h0.astype(f32), (chunk_decay.swapaxes(0, 1), states.swapaxes(0, 1)))
    h_prev = h_prev.swapaxes(0, 1)
    y_off = jnp.einsum('bcqhn,bchpn,bcqh->bcqhp', ch, h_prev, jnp.exp(cs))
    y = (y_diag + y_off).reshape(bsz, nc * chunk, nh, hp)
    return y, h_last


def token_mixer(x, p, conv_buf, ssm_buf, ssm_h0, past):
    (w_in, conv_dw, ln_conv_g, ln_conv_b, w_conv_out, w_attn_out, ssm_conv_w, ssm_conv_b,
     ssm_dt_bias, ssm_a_log, ssm_d, ssm_norm_g, w_ssm_out, w_mix_out, ln_mix_g, ln_mix_b) = p
    b, t, _ = x.shape
    proj = mm3(x, w_in)
    glu_in = proj[..., COL_GLU:COL_GLU + 2 * CONV_DIM]
    k = proj[..., COL_K:COL_K + LANES]
    v = proj[..., COL_V:COL_V + LANES]
    small = proj[..., COL_SMALL:COL_SMALL + LANES]
    ki = small[..., 0:IDX_DIM]
    dt = small[..., IDX_DIM + IDX_HEADS:IDX_DIM + IDX_HEADS + SSM_HEADS]
    xbc = proj[..., COL_XBC:COL_XBC + SSM_XBC]
    glu = glu_in[..., :CONV_DIM] * jax.nn.sigmoid(glu_in[..., CONV_DIM:])
    glu_pad = jnp.concatenate([conv_buf, glu], axis=1)
    cv = causal_dwconv(glu_pad, conv_dw)
    if past is None:
        o = dsa_prompt(proj, k, v, ki)
    else:
        cache_k, cache_v, cache_ik, page_table, layer = past
        o = dsa_decode(proj, k, v, cache_k, cache_v, cache_ik, page_table, layer)
    xbc_pad = jnp.concatenate([ssm_buf, xbc], axis=1)
    xbc_c = jax.nn.silu(causal_dwconv(xbc_pad, ssm_conv_w) + ssm_conv_b)
    xs, bm, cm = jnp.split(xbc_c, [SSM_INNER, SSM_INNER + SSM_GROUPS * SSM_STATE], axis=-1)
    xs = xs.reshape(b, t, SSM_HEADS, SSM_HEAD_DIM)
    dtf = jax.nn.softplus(dt + ssm_dt_bias)
    a = -jnp.exp(ssm_a_log)
    y, h_last = ssd_scan(xs, dtf, a, bm.reshape(b, t, SSM_GROUPS, SSM_STATE),
                         cm.reshape(b, t, SSM_GROUPS, SSM_STATE), ssm_h0)
    y = (y + ssm_d[:, None] * xs).reshape(b, t, SSM_INNER)
    m = b * t
    x_new = branch_mix_ln(cv.reshape(m, CONV_DIM), o.reshape(m, -1), y.reshape(m, SSM_INNER),
                          proj.reshape(m, D_IN_PAD), x.reshape(m, D_MODEL), ln_conv_g, ln_conv_b, ssm_norm_g,
                          w_conv_out, w_attn_out, w_ssm_out, w_mix_out, ln_mix_g, ln_mix_b).reshape(b, t, D_MODEL)
    return (x_new, k.reshape(b, t, N_KV_HEADS, HEAD_DIM), v.reshape(b, t, N_KV_HEADS, HEAD_DIM), ki,
            glu_pad[:, -(CONV_WIDTH - 1):], xbc_pad[:, -(SSM_CONV - 1):], h_last)


def _pad_w_in(w):
    sp = (0,) + SPLIT_POINTS + (w.shape[1],)
    seg = [w[:, sp[i]:sp[i + 1]] for i in range(len(SPLIT_SIZES))]
    glu, q, k, v, qi, ki, wi, z, xbc, dt, gates = seg
    pad = jnp.zeros((w.shape[0], LANES - IDX_DIM - IDX_HEADS - SSM_HEADS), w.dtype)
    n_rep = N_HEADS // N_KV_HEADS
    q = q.reshape(-1, N_KV_HEADS, n_rep, HEAD_DIM).swapaxes(1, 2).reshape(q.shape)
    out = jnp.concatenate([gates, xbc, q, glu, z, qi, k, v, ki, wi, dt, pad], axis=1)
    assert out.shape[1] == D_IN_PAD
    return out


def _perm_w_attn_out(w):
    n_rep = N_HEADS // N_KV_HEADS
    return w.reshape(N_KV_HEADS, n_rep, HEAD_DIM, -1).swapaxes(0, 1).reshape(w.shape)


def kernel(x_prompt, x_sample, mem_prompt, cache_k, cache_v, cache_ik, cache_mem_k, cache_mem_v, state_conv, state_ssm_conv, state_ssm, page_table, w_in, conv_dw, ln_conv_g, ln_conv_b, w_conv_out, w_attn_out, ssm_conv_w, ssm_conv_b, ssm_dt_bias, ssm_a_log, ssm_d, ssm_norm_g, w_ssm_out, w_mix_out, ln_mix_g, ln_mix_b, w_mq, w_mk, w_mv, w_mo, ln_mem_g, ln_mem_b, w_group, b_group, w_router, b_router, w_e_gate, w_e_up, w_e_down, ln_ffn_g, ln_ffn_b):
    bf16 = jnp.bfloat16
    bp, tp, _ = x_prompt.shape
    bs, ts, _ = x_sample.shape
    cache_k4 = cache_k.reshape(cache_k.shape[:3] + (N_KV_HEADS * HEAD_DIM,))
    cache_v4 = cache_v.reshape(cache_v.shape[:3] + (N_KV_HEADS * HEAD_DIM,))
    cache_mem_k4 = cache_mem_k.reshape(cache_mem_k.shape[:3] + (MEM_HEADS * MEM_HEAD_DIM,))
    cache_mem_v4 = cache_mem_v.reshape(cache_mem_v.shape[:3] + (MEM_HEADS * MEM_HEAD_DIM,))
    xp, xs = x_prompt, x_sample
    outs_p = [[] for _ in range(8)]
    outs_s = [[] for _ in range(6)]
    for l in range(DEPTH):
        mix_p = (_pad_w_in(w_in[l]).astype(bf16), conv_dw[l], ln_conv_g[l], ln_conv_b[l],
                 w_conv_out[l].astype(bf16), _perm_w_attn_out(w_attn_out[l]).astype(bf16),
                 ssm_conv_w[l], ssm_conv_b[l], ssm_dt_bias[l], ssm_a_log[l], ssm_d[l], ssm_norm_g[l],
                 w_ssm_out[l].astype(bf16), w_mix_out[l].astype(bf16), ln_mix_g[l], ln_mix_b[l])
        moe_p = (w_group[l], b_group[l], w_router[l], b_router[l],
                 w_e_gate[l].astype(bf16), w_e_up[l].astype(bf16), w_e_down[l].astype(bf16), ln_ffn_g[l], ln_ffn_b[l])
        mem_p = (w_mq[l].astype(bf16), w_mo[l].astype(bf16), ln_mem_g[l], ln_mem_b[l])
        hd_mem = MEM_HEADS * MEM_HEAD_DIM
        xp, kp, vp, kip, cbp, sbp, hp = token_mixer(
            xp, mix_p,
            jnp.zeros((bp, CONV_WIDTH - 1, CONV_DIM), xp.dtype),
            jnp.zeros((bp, SSM_CONV - 1, SSM_XBC), xp.dtype),
            jnp.zeros((bp, SSM_HEADS, SSM_HEAD_DIM, SSM_STATE), jnp.float32),
            None)
        mem_kv = mm(mem_prompt.reshape(bp * N_MEM, D_MODEL),
                    jnp.concatenate([w_mk[l], w_mv[l]], axis=1).astype(bf16)).reshape(bp, N_MEM, 2 * hd_mem)
        mkp, mvp = mem_kv[..., :hd_mem], mem_kv[..., hd_mem:]
        xp = mem_attn_ln(xp.reshape(bp * tp, D_MODEL), mkp, mvp, *mem_p, t_seq=tp)
        xp = moe_ln(xp, *moe_p).reshape(bp, tp, D_MODEL)
        mkp = mkp.reshape(bp, N_MEM, MEM_HEADS, MEM_HEAD_DIM)
        mvp = mvp.reshape(bp, N_MEM, MEM_HEADS, MEM_HEAD_DIM)
        xs, ks_new, vs_new, kis, cbs, sbs, hs = token_mixer(
            xs, mix_p, state_conv[:, l], state_ssm_conv[:, l], state_ssm[:, l],
            (cache_k4, cache_v4, cache_ik, page_table, l))
        xs = mem_attn_ln(xs.reshape(bs * ts, D_MODEL), cache_mem_k4, cache_mem_v4, *mem_p, t_seq=ts, layer=l)
        xs = moe_ln(xs, *moe_p).reshape(bs, ts, D_MODEL)
        for lst, arr in zip(outs_p, (kp, vp, kip, mkp, mvp, cbp, sbp, hp)):
            lst.append(arr)
        for lst, arr in zip(outs_s, (ks_new, vs_new, kis, cbs, sbs, hs)):
            lst.append(arr)
    p_k, p_v, p_ik, p_mem_k, p_mem_v, p_conv, p_ssm_conv, p_ssm = [jnp.stack(a, axis=1) for a in outs_p]
    s_k, s_v, s_ik, s_conv, s_ssm_conv, s_ssm = [jnp.stack(a, axis=1) for a in outs_s]
    return (xp, xs, p_k, p_v, p_ik, p_mem_k, p_mem_v, p_conv, p_ssm_conv, p_ssm,
            s_k, s_v, s_ik, s_conv, s_ssm_conv, s_ssm)
```

```python
import functools
import math

import jax
import jax.numpy as jnp
from jax import lax
from jax.experimental import pallas as pl
from jax.experimental.pallas import tpu as pltpu

D_MODEL = 1024
DEPTH = 2
PAGE_SIZE = 128
CONV_DIM = 512
CONV_WIDTH = 31
N_HEADS = 8
N_KV_HEADS = 2
HEAD_DIM = 64
IDX_HEADS = 4
IDX_DIM = 64
TOPK_MAX = 256
Q_BLOCK = 128
SSM_HEADS = 16
SSM_HEAD_DIM = 64
SSM_INNER = SSM_HEADS * SSM_HEAD_DIM
SSM_GROUPS = 2
SSM_STATE = 128
SSM_CONV = 4
SSM_CHUNK = 128
SSM_XBC = SSM_INNER + 2 * SSM_GROUPS * SSM_STATE
N_MEM = 256
MEM_HEADS = 4
MEM_HEAD_DIM = 128
N_GROUPS = 4
EXPERTS_PER_GROUP = 4
N_EXPERTS = N_GROUPS * EXPERTS_PER_GROUP
TOPK_IN_GROUP = 2
D_EXPERT = 512
DN_ALPHA = (2 * DEPTH) ** 0.25
LN_EPS = 1e-5

SPLIT_SIZES = (2 * CONV_DIM, N_HEADS * HEAD_DIM, N_KV_HEADS * HEAD_DIM, N_KV_HEADS * HEAD_DIM,
               IDX_HEADS * IDX_DIM, IDX_DIM, IDX_HEADS, SSM_INNER, SSM_XBC, SSM_HEADS, 3 * D_MODEL)
SPLIT_POINTS = tuple(sum(SPLIT_SIZES[:i + 1]) for i in range(len(SPLIT_SIZES) - 1))

VMEM_LIMIT_BYTES = 56 * 1024 * 1024
MM_COL_CHUNK = 512


def _mm_kernel(x_ref, w_ref, o_ref):
    xb = x_ref[...].astype(jnp.bfloat16)
    n = o_ref.shape[1]
    for c0 in range(0, n, MM_COL_CHUNK):
        c1 = min(n, c0 + MM_COL_CHUNK)
        o_ref[:, c0:c1] = jnp.dot(xb, w_ref[:, c0:c1], preferred_element_type=jnp.float32)


def mm(x, w, tm=256):
    m, k = x.shape
    n = w.shape[1]
    tm = min(tm, m)
    assert m % tm == 0 and n % 128 == 0 and k % 128 == 0
    return pl.pallas_call(
        _mm_kernel,
        grid=(m // tm,),
        in_specs=[pl.BlockSpec((tm, k), lambda i: (i, 0)),
                  pl.BlockSpec((k, n), lambda i: (0, 0), pipeline_mode=pl.Buffered(1))],
        out_specs=pl.BlockSpec((tm, n), lambda i: (i, 0)),
        out_shape=jax.ShapeDtypeStruct((m, n), jnp.float32),
        compiler_params=pltpu.CompilerParams(dimension_semantics=("parallel",),
                                             vmem_limit_bytes=VMEM_LIMIT_BYTES),
    )(x, w)


def mm3(x, w, tm=256):
    b, t, d = x.shape
    return mm(x.reshape(b * t, d), w, tm).reshape(b, t, w.shape[1])


INT32_MIN = -2 ** 31
MASK_BIAS = -1e30
DSA_KEY_CHUNK = 512
LANES = 128


def _nt_dot(a, b):
    return lax.dot_general(a, b, (((1,), (1,)), ((), ())), preferred_element_type=jnp.float32)


def _dsa_prompt_kernel(q_ref, qi_ref, sm_ref, k_ref, v_ref, ki_ref, o_ref,
                       key_scr, s_scr, p_scr, qpad_scr, qis_scr, m_scr, l_scr, acc_scr, *, k_sel):
    f32, i32, bf16 = jnp.float32, jnp.int32, jnp.bfloat16
    qb = q_ref.shape[1]
    kc_w = DSA_KEY_CHUNK
    sub = kc_w // LANES
    i = pl.program_id(1)
    n_chunks = (i * qb + qb + kc_w - 1) // kc_w
    lane = lax.broadcasted_iota(i32, (qb, LANES), 1)
    half = [lane < HEAD_DIM, lane >= HEAD_DIM]

    for pair in range(IDX_HEADS // 2):
        src = qi_ref[0, :, pair * LANES:(pair + 1) * LANES]
        hi = src.astype(bf16).astype(f32)
        lo_swapped = pltpu.roll(src - hi, IDX_DIM, axis=1)
        for hh in range(2):
            h = 2 * pair + hh
            qis_scr[h * qb:(h + 1) * qb, 0:LANES] = jnp.where(half[hh], hi, lo_swapped).astype(bf16)
            qis_scr[h * qb:(h + 1) * qb, LANES:2 * LANES] = jnp.where(half[hh], hi, 0.0).astype(bf16)
    w_idx = [sm_ref[0, :, IDX_DIM + h:IDX_DIM + h + 1] for h in range(IDX_HEADS)]
    row_pos = i * qb + lax.broadcasted_iota(i32, (qb, kc_w), 0)
    col_iota = lax.broadcasted_iota(i32, (qb, kc_w), 1)

    def score_body(c, carry):
        off = pl.multiple_of(c * kc_w, kc_w)
        d = _nt_dot(qis_scr[...], ki_ref[0, pl.ds(off, kc_w), :])
        idx = w_idx[0] * jnp.maximum(d[0:qb], 0.0)
        for h in range(1, IDX_HEADS):
            idx = idx + w_idx[h] * jnp.maximum(d[h * qb:(h + 1) * qb], 0.0)
        idx = jnp.where(idx == 0.0, 0.0, idx)
        bits = lax.bitcast_convert_type(idx, i32)
        key = bits ^ ((bits >> 31) & 0x7FFFFFFF)
        key_scr[c] = jnp.where(off + col_iota <= row_pos, key, INT32_MIN)
        return carry

    lax.fori_loop(0, n_chunks, score_body, 0)

    def count(pred):
        def body(c, acc):
            for g in range(sub):
                acc = acc + jnp.where(pred(key_scr[c, :, g * LANES:(g + 1) * LANES]), 1.0, 0.0)
            return acc
        acc = lax.fori_loop(0, n_chunks, body, jnp.zeros((qb, LANES), f32))
        return jnp.sum(acc, axis=1, keepdims=True)

    def search_body(it, lo):
        cand = lo + lax.shift_left(jnp.int32(1), 31 - it)
        cand_b = jnp.broadcast_to(cand, (qb, LANES))
        cnt = count(lambda kk: kk >= cand_b)
        return jnp.where(cnt >= float(k_sel), cand, lo)

    thr = lax.fori_loop(0, 32, search_body, jnp.full((qb, 1), INT32_MIN, i32))
    thr_b = jnp.broadcast_to(thr, (qb, LANES))
    need_b = jnp.broadcast_to(float(k_sel) - count(lambda kk: kk > thr_b), (qb, LANES))

    tri = jnp.where(lax.broadcasted_iota(i32, (LANES, LANES), 0) <= lax.broadcasted_iota(i32, (LANES, LANES), 1),
                    1.0, 0.0).astype(bf16)
    zero_bits = jnp.zeros((qb, LANES), i32)
    mask_bits = lax.bitcast_convert_type(jnp.full((qb, LANES), MASK_BIAS, f32), i32)

    def select_body(c, seen):
        for g in range(sub):
            kk = key_scr[c, :, g * LANES:(g + 1) * LANES]
            eq = kk == thr_b
            rank = seen + jnp.dot(jnp.where(eq, 1.0, 0.0).astype(bf16), tri, preferred_element_type=f32)
            sel = ((kk > thr_b) | (eq & (rank <= need_b))) & (kk > INT32_MIN)
            key_scr[c, :, g * LANES:(g + 1) * LANES] = jnp.where(sel, zero_bits, mask_bits)
            seen = jnp.broadcast_to(rank[:, LANES - 1:LANES], (qb, LANES))
        return seen

    lax.fori_loop(0, n_chunks, select_body, jnp.zeros((qb, LANES), f32))

    n_blk = N_HEADS
    scale = 1.0 / math.sqrt(HEAD_DIM)
    for g in range(N_HEADS // N_KV_HEADS):
        src = q_ref[0, :, g * LANES:(g + 1) * LANES] * scale
        for j in range(N_KV_HEADS):
            r = g * N_KV_HEADS + j
            qpad_scr[r * qb:(r + 1) * qb, :] = jnp.where(half[j], src, 0.0).astype(bf16)
    m_scr[...] = jnp.full(m_scr.shape, -jnp.inf, f32)
    l_scr[...] = jnp.zeros(l_scr.shape, f32)
    acc_scr[...] = jnp.zeros(acc_scr.shape, f32)

    def attend_body(c, carry):
        off = pl.multiple_of(c * kc_w, kc_w)
        s_scr[...] = _nt_dot(qpad_scr[...], k_ref[0, pl.ds(off, kc_w), :])
        bias = lax.bitcast_convert_type(key_scr[c], f32)
        for r in range(n_blk):
            rows = slice(r * qb, (r + 1) * qb)
            s = s_scr[rows, :] + bias
            m_old = m_scr[rows, :]
            m_new = jnp.maximum(m_old, jnp.broadcast_to(jnp.max(s, axis=1, keepdims=True), (qb, LANES)))
            alpha = jnp.exp(m_old - m_new)
            p = jnp.exp(s - jnp.tile(m_new, (1, sub)))
            l_scr[rows, :] = alpha * l_scr[rows, :] + jnp.broadcast_to(jnp.sum(p, axis=1, keepdims=True), (qb, LANES))
            acc_scr[rows, :] = alpha * acc_scr[rows, :]
            p_scr[rows, :] = p.astype(bf16)
            m_scr[rows, :] = m_new
        acc_scr[...] += jnp.dot(p_scr[...], v_ref[0, pl.ds(off, kc_w), :], preferred_element_type=f32)
        return carry

    lax.fori_loop(0, n_chunks, attend_body, 0)

    for g in range(N_HEADS // N_KV_HEADS):
        outs = []
        for j in range(N_KV_HEADS):
            rows = slice((g * N_KV_HEADS + j) * qb, (g * N_KV_HEADS + j + 1) * qb)
            outs.append(acc_scr[rows, :] / l_scr[rows, :])
        o_ref[0, :, g * LANES:(g + 1) * LANES] = jnp.where(half[0], outs[0], outs[1])


def _split_bf16(x):
    hi = x.astype(jnp.bfloat16)
    return hi, (x - hi.astype(jnp.float32)).astype(jnp.bfloat16)


def dsa_prompt(proj, k, v, ki):
    b, t, _ = proj.shape
    qb = Q_BLOCK
    w_q, w_qi = N_HEADS * HEAD_DIM, IDX_HEADS * IDX_DIM
    assert t % DSA_KEY_CHUNK == 0 and t % qb == 0
    assert IDX_DIM == HEAD_DIM == LANES // 2 and IDX_HEADS % 2 == 0
    k_sel = min(TOPK_MAX, t // 4)
    n_rows = N_HEADS * qb
    ki_hi, ki_lo = _split_bf16(ki)
    ki4 = jnp.concatenate([ki_hi, ki_hi, ki_lo, ki_lo], axis=-1)
    k, v = k.astype(jnp.bfloat16), v.astype(jnp.bfloat16)
    seq_spec = pl.BlockSpec((1, t, LANES), lambda bi, i: (bi, 0, 0))
    return pl.pallas_call(
        functools.partial(_dsa_prompt_kernel, k_sel=k_sel),
        grid=(b, t // qb),
        in_specs=[pl.BlockSpec((1, qb, w_q), lambda bi, i: (bi, i, COL_Q // w_q)),
                  pl.BlockSpec((1, qb, w_qi), lambda bi, i: (bi, i, COL_QI // w_qi)),
                  pl.BlockSpec((1, qb, LANES), lambda bi, i: (bi, i, COL_SMALL // LANES)),
                  seq_spec, seq_spec, pl.BlockSpec((1, t, 2 * LANES), lambda bi, i: (bi, 0, 0))],
        out_specs=pl.BlockSpec((1, qb, N_HEADS * HEAD_DIM), lambda bi, i: (bi, i, 0)),
        out_shape=jax.ShapeDtypeStruct((b, t, N_HEADS * HEAD_DIM), jnp.float32),
        scratch_shapes=[pltpu.VMEM((t // DSA_KEY_CHUNK, qb, DSA_KEY_CHUNK), jnp.int32),
                        pltpu.VMEM((n_rows, DSA_KEY_CHUNK), jnp.float32),
                        pltpu.VMEM((n_rows, DSA_KEY_CHUNK), jnp.bfloat16),
                        pltpu.VMEM((n_rows, LANES), jnp.bfloat16),
                        pltpu.VMEM((IDX_HEADS * qb, 2 * LANES), jnp.bfloat16),
                        pltpu.VMEM((n_rows, LANES), jnp.float32),
                        pltpu.VMEM((n_rows, LANES), jnp.float32),
                        pltpu.VMEM((n_rows, LANES), jnp.float32)],
        compiler_params=pltpu.CompilerParams(dimension_semantics=("parallel", "arbitrary"),
                                             vmem_limit_bytes=VMEM_LIMIT_BYTES),
        name="dsa_prompt",
    )(proj, proj, proj, k, v, ki4)


DEC_ROWS = 8


def _dsa_decode_kernel(pt_ref, q_ref, qi_ref, sm_ref, kn_ref, vn_ref, *rest, n_pages, k_sel):
    f32, i32, bf16 = jnp.float32, jnp.int32, jnp.bfloat16
    k_pages = rest[0:n_pages]
    v_pages = rest[n_pages:2 * n_pages]
    ik_pages = rest[2 * n_pages:3 * n_pages]
    o_ref = rest[3 * n_pages]
    kall, vall, ikall, new_scr, qis, wpad, qpad, key_scr, p_scr = rest[3 * n_pages + 1:]
    t_new = q_ref.shape[1]
    past = n_pages * PAGE_SIZE
    s_pad = past + PAGE_SIZE
    n_chunks = s_pad // LANES
    rows = DEC_ROWS

    for p in range(n_pages):
        sl = slice(p * PAGE_SIZE, (p + 1) * PAGE_SIZE)
        for j in range(N_KV_HEADS):
            kall[j * HEAD_DIM:(j + 1) * HEAD_DIM, sl] = k_pages[p][j]
            vall[j * HEAD_DIM:(j + 1) * HEAD_DIM, sl] = v_pages[p][j]
        ikall[:, sl] = ik_pages[p][...]
    tail = slice(past, s_pad)
    for src, dst, n_feat in ((kn_ref[0], kall, LANES), (vn_ref[0], vall, LANES),
                             (sm_ref[0], ikall, IDX_DIM)):
        new_scr[...] = jnp.zeros(new_scr.shape, f32)
        new_scr[0:t_new, :] = src
        dst[:, tail] = new_scr[...].T[0:n_feat, :]

    qis[...] = jnp.zeros(qis.shape, f32)
    wpad[...] = jnp.zeros(wpad.shape, f32)
    for h in range(IDX_HEADS):
        qis[h * rows:h * rows + t_new, :] = qi_ref[0, :, h * IDX_DIM:(h + 1) * IDX_DIM]
    wpad[0:t_new, :] = sm_ref[0]
    q_hi, q_lo = _split_bf16(qis[...])
    k_hi, k_lo = _split_bf16(ikall[...])

    def mm_f32(a, b):
        return jnp.dot(a, b, preferred_element_type=f32)

    d = mm_f32(q_hi, k_hi) + mm_f32(q_lo, k_hi) + mm_f32(q_hi, k_lo)
    idx = wpad[:, IDX_DIM:IDX_DIM + 1] * jnp.maximum(d[0:rows], 0.0)
    for h in range(1, IDX_HEADS):
        idx = idx + wpad[:, IDX_DIM + h:IDX_DIM + h + 1] * jnp.maximum(d[h * rows:(h + 1) * rows], 0.0)
    idx = jnp.where(idx == 0.0, 0.0, idx)
    bits = lax.bitcast_convert_type(idx, i32)
    key = bits ^ ((bits >> 31) & 0x7FFFFFFF)
    col = lax.broadcasted_iota(i32, (rows, s_pad), 1)
    q_pos = past + lax.broadcasted_iota(i32, (rows, s_pad), 0)
    key_scr[...] = jnp.where((col <= q_pos) & (col < past + t_new), key, INT32_MIN)

    def search_body(it, lo):
        cand = lo + lax.shift_left(jnp.int32(1), 31 - it)
        cnt = jnp.sum(jnp.where(key_scr[...] >= cand, 1.0, 0.0), axis=1, keepdims=True)
        return jnp.where(cnt >= float(k_sel), cand, lo)

    thr = lax.fori_loop(0, 32, search_body, jnp.full((rows, 1), INT32_MIN, i32))
    need = float(k_sel) - jnp.sum(jnp.where(key_scr[...] > thr, 1.0, 0.0), axis=1, keepdims=True)
    tri = jnp.where(lax.broadcasted_iota(i32, (LANES, LANES), 0) <= lax.broadcasted_iota(i32, (LANES, LANES), 1),
                    1.0, 0.0).astype(bf16)
    seen = jnp.zeros((rows, 1), f32)
    for c in range(n_chunks):
        kk = key_scr[:, c * LANES:(c + 1) * LANES]
        eq = kk == thr
        rank = seen + jnp.dot(jnp.where(eq, 1.0, 0.0).astype(bf16), tri, preferred_element_type=f32)
        sel = ((kk > thr) | (eq & (rank <= need))) & (kk > INT32_MIN)
        key_scr[:, c * LANES:(c + 1) * LANES] = lax.bitcast_convert_type(jnp.where(sel, 0.0, MASK_BIAS), i32)
        seen = rank[:, LANES - 1:LANES]
    bias = lax.bitcast_convert_type(key_scr[...], f32)

    lane = lax.broadcasted_iota(i32, (t_new, LANES), 1)
    half = [lane < HEAD_DIM, lane >= HEAD_DIM]
    scale = 1.0 / math.sqrt(HEAD_DIM)
    qpad[...] = jnp.zeros(qpad.shape, f32)
    for g in range(N_HEADS // N_KV_HEADS):
        src = q_ref[0, :, g * LANES:(g + 1) * LANES] * scale
        for j in range(N_KV_HEADS):
            r = g * N_KV_HEADS + j
            qpad[r * rows:r * rows + t_new, :] = jnp.where(half[j], src, 0.0)
    s_all = mm_f32(qpad[...].astype(bf16), kall[...].astype(bf16))
    for r in range(N_HEADS):
        s = s_all[r * rows:(r + 1) * rows] + bias
        p = jnp.exp(s - jnp.max(s, axis=1, keepdims=True))
        p_scr[r * rows:(r + 1) * rows, :] = p / jnp.sum(p, axis=1, keepdims=True)
    o_all = _nt_dot(p_scr[...].astype(bf16), vall[...].astype(bf16))
    for g in range(N_HEADS // N_KV_HEADS):
        r0, r1 = g * N_KV_HEADS * rows, (g * N_KV_HEADS + 1) * rows
        o_ref[0, :, g * LANES:(g + 1) * LANES] = jnp.where(half[0], o_all[r0:r0 + t_new], o_all[r1:r1 + t_new])


def dsa_decode(proj, k_new, v_new, cache_k, cache_v, cache_ik, page_table, layer):
    b, t_new, _ = proj.shape
    n_pages = page_table.shape[1]
    assert t_new <= DEC_ROWS and PAGE_SIZE == LANES
    s_pad = (n_pages + 1) * PAGE_SIZE
    k_sel = min(TOPK_MAX, (n_pages * PAGE_SIZE + t_new) // 4)
    cache_k = jnp.transpose(cache_k, (0, 1, 3, 4, 2))
    cache_v = jnp.transpose(cache_v, (0, 1, 3, 4, 2))
    cache_ik = jnp.transpose(cache_ik, (0, 1, 3, 2))

    def tok_spec(width, col=0):
        return pl.BlockSpec((1, t_new, width), lambda bi, pt: (bi, 0, col // width))

    def page_spec(p):
        return pl.BlockSpec((None, None, IDX_DIM, PAGE_SIZE), lambda bi, pt, p=p: (pt[bi, p], layer, 0, 0))

    def kv_page_spec(p):
        return pl.BlockSpec((None, None, N_KV_HEADS, HEAD_DIM, PAGE_SIZE),
                            lambda bi, pt, p=p: (pt[bi, p], layer, 0, 0, 0))

    grid_spec = pltpu.PrefetchScalarGridSpec(
        num_scalar_prefetch=1,
        grid=(b,),
        in_specs=[tok_spec(N_HEADS * HEAD_DIM, COL_Q), tok_spec(IDX_HEADS * IDX_DIM, COL_QI),
                  tok_spec(LANES, COL_SMALL), tok_spec(LANES), tok_spec(LANES)]
                 + [kv_page_spec(p) for p in range(n_pages)]
                 + [kv_page_spec(p) for p in range(n_pages)]
                 + [page_spec(p) for p in range(n_pages)],
        out_specs=tok_spec(N_HEADS * HEAD_DIM),
        scratch_shapes=[pltpu.VMEM((LANES, s_pad), jnp.float32),
                        pltpu.VMEM((LANES, s_pad), jnp.float32),
                        pltpu.VMEM((IDX_DIM, s_pad), jnp.float32),
                        pltpu.VMEM((LANES, LANES), jnp.float32),
                        pltpu.VMEM((IDX_HEADS * DEC_ROWS, IDX_DIM), jnp.float32),
                        pltpu.VMEM((DEC_ROWS, LANES), jnp.float32),
                        pltpu.VMEM((N_HEADS * DEC_ROWS, LANES), jnp.float32),
                        pltpu.VMEM((DEC_ROWS, s_pad), jnp.int32),
                        pltpu.VMEM((N_HEADS * DEC_ROWS, s_pad), jnp.float32)])
    return pl.pallas_call(
        functools.partial(_dsa_decode_kernel, n_pages=n_pages, k_sel=k_sel),
        grid_spec=grid_spec,
        out_shape=jax.ShapeDtypeStruct((b, t_new, N_HEADS * HEAD_DIM), jnp.float32),
        compiler_params=pltpu.CompilerParams(dimension_semantics=("arbitrary",),
                                             vmem_limit_bytes=VMEM_LIMIT_BYTES),
        name="dsa_decode",
    )(page_table, proj, proj, proj, k_new, v_new,
      *([cache_k] * n_pages), *([cache_v] * n_pages), *([cache_ik] * n_pages))


COL_GATES, COL_XBC, COL_Q, COL_GLU, COL_Z, COL_QI, COL_K, COL_V, COL_SMALL, COL_DT, D_IN_PAD = (
    0, 3072, 4608, 5120, 6144, 7168, 7424, 7552, 7680, 7808, 7936)
ROW_BLOCK = 256


def _ln_rows(x, g, b):
    mu = jnp.mean(x, axis=-1, keepdims=True)
    xc = x - mu
    var = jnp.mean(xc * xc, axis=-1, keepdims=True)
    return xc * lax.rsqrt(var + LN_EPS) * g + b


def _bdot(a, w_ref):
    return jnp.dot(a.astype(jnp.bfloat16), w_ref[...], preferred_element_type=jnp.float32)


def _branch_mix_kernel(cv_ref, o_ref, y_ref, z_ref, gates_ref, x_ref, lncg_ref, lncb_ref, ng_ref,
                       wc_ref, wa_ref, ws_ref, wm_ref, lng_ref, lnb_ref, out_ref):
    ca = _ln_rows(cv_ref[...], lncg_ref[...], lncb_ref[...])
    y_a = _bdot(ca * jax.nn.sigmoid(ca), wc_ref)
    y_b = _bdot(o_ref[...], wa_ref)
    z = z_ref[...]
    t = y_ref[...] * (z * jax.nn.sigmoid(z))
    t = t * lax.rsqrt(jnp.mean(t * t, axis=-1, keepdims=True) + LN_EPS) * ng_ref[...]
    y_c = _bdot(t, ws_ref)
    d = D_MODEL
    mix = (jax.nn.sigmoid(gates_ref[:, 0:d]) * y_a + jax.nn.sigmoid(gates_ref[:, d:2 * d]) * y_b
           + jax.nn.sigmoid(gates_ref[:, 2 * d:3 * d]) * y_c)
    out_ref[...] = _ln_rows(DN_ALPHA * x_ref[...] + _bdot(mix, wm_ref), lng_ref[...], lnb_ref[...])


def branch_mix_ln(cv, o, y, proj, x, lncg, lncb, ng, wc, wa, ws, wm, lng, lnb):
    m = x.shape[0]
    tm = min(ROW_BLOCK, m)
    assert m % tm == 0
    d = D_MODEL

    def rows(width, col_block=0):
        return pl.BlockSpec((tm, width), lambda i, cb=col_block: (i, cb))

    def whole(a):
        return pl.BlockSpec(a.shape, lambda i: (0,) * a.ndim, pipeline_mode=pl.Buffered(1))

    vecs = [a.reshape(1, -1) for a in (lncg, lncb, ng)]
    lnv = [a.reshape(1, -1) for a in (lng, lnb)]
    return pl.pallas_call(
        _branch_mix_kernel,
        grid=(m // tm,),
        in_specs=[rows(CONV_DIM), rows(N_HEADS * HEAD_DIM), rows(SSM_INNER), rows(SSM_INNER, COL_Z // SSM_INNER),
                  rows(3 * d, COL_GATES // (3 * d)), rows(d)]
                 + [whole(a) for a in vecs] + [whole(a) for a in (wc, wa, ws, wm)] + [whole(a) for a in lnv],
        out_specs=rows(d),
        out_shape=jax.ShapeDtypeStruct((m, d), jnp.float32),
        compiler_params=pltpu.CompilerParams(dimension_semantics=("parallel",), vmem_limit_bytes=VMEM_LIMIT_BYTES),
        name="branch_mix_ln",
    )(cv, o, y, proj, proj, x, *vecs, wc, wa, ws, wm, *lnv)


def _mem_attn_kernel(x_ref, mk_ref, mv_ref, wq_ref, wo_ref, lng_ref, lnb_ref, out_ref, o_scr, *, seqs, t_seq):
    bf16 = jnp.bfloat16
    x = x_ref[...]
    q = _bdot(x, wq_ref)
    scale = 1.0 / math.sqrt(MEM_HEAD_DIM)
    for s in range(seqs):
        qs = q[s * t_seq:(s + 1) * t_seq].astype(bf16)
        for h in range(MEM_HEADS):
            cols = slice(h * MEM_HEAD_DIM, (h + 1) * MEM_HEAD_DIM)
            sc = _nt_dot(qs[:, cols], mk_ref[s, :, h, :].astype(bf16)) * scale
            p = jnp.exp(sc - jnp.max(sc, axis=-1, keepdims=True))
            p = p / jnp.sum(p, axis=-1, keepdims=True)
            o_scr[s * t_seq:(s + 1) * t_seq, cols] = jnp.dot(p.astype(bf16), mv_ref[s, :, h, :].astype(bf16),
                                                              preferred_element_type=jnp.float32)
    out_ref[...] = _ln_rows(DN_ALPHA * x + _bdot(o_scr[...], wo_ref), lng_ref[...], lnb_ref[...])


def mem_attn_ln(x, mk, mv, wq, wo, lng, lnb, t_seq, layer=None):
    m, d = x.shape
    hd = MEM_HEADS * MEM_HEAD_DIM
    if t_seq >= ROW_BLOCK:
        seqs, tm = 1, ROW_BLOCK
        assert t_seq % tm == 0
        per_seq = t_seq // tm
        seq_of = lambda i: i // per_seq
    else:
        seqs = max(1, 32 // t_seq)
        tm = seqs * t_seq
        assert m % tm == 0
        seq_of = lambda i: i
    if layer is None:
        mem_spec = pl.BlockSpec((seqs, N_MEM, MEM_HEADS, MEM_HEAD_DIM), lambda i: (seq_of(i), 0, 0, 0))
    else:
        mem_spec = pl.BlockSpec((seqs, None, N_MEM, MEM_HEADS, MEM_HEAD_DIM), lambda i: (seq_of(i), layer, 0, 0, 0))

    def whole(a):
        return pl.BlockSpec(a.shape, lambda i: (0,) * a.ndim, pipeline_mode=pl.Buffered(1))

    lnv = [a.reshape(1, -1) for a in (lng, lnb)]
    return pl.pallas_call(
        functools.partial(_mem_attn_kernel, seqs=seqs, t_seq=min(t_seq, tm)),
        grid=(m // tm,),
        in_specs=[pl.BlockSpec((tm, d), lambda i: (i, 0)), mem_spec, mem_spec, whole(wq), whole(wo)]
                 + [whole(a) for a in lnv],
        out_specs=pl.BlockSpec((tm, d), lambda i: (i, 0)),
        out_shape=jax.ShapeDtypeStruct((m, d), jnp.float32),
        scratch_shapes=[pltpu.VMEM((tm, hd), jnp.float32)],
        compiler_params=pltpu.CompilerParams(dimension_semantics=("parallel",), vmem_limit_bytes=VMEM_LIMIT_BYTES),
        name="mem_attn_ln",
    )(x, mk, mv, wq, wo, *lnv)


MOE_ROW_BLOCK = 1024


def _moe_kernel(x_ref, wr_hi_ref, wr_lo_ref, br_ref, wg_ref, wu_ref, wd_ref, lng_ref, lnb_ref, out_ref,
                xb_scr, comb_scr, acc_scr):
    f32 = jnp.float32
    e = pl.program_id(1)
    tm = x_ref.shape[0]
    lane = lax.broadcasted_iota(jnp.int32, (tm, LANES), 1).astype(f32)

    @pl.when(e == 0)
    def _route():
        x_hi, x_lo = _split_bf16(x_ref[...])
        xb_scr[...] = x_hi
        lg = (jnp.dot(x_hi, wr_hi_ref[...], preferred_element_type=f32)
              + jnp.dot(x_lo, wr_hi_ref[...], preferred_element_type=f32)
              + jnp.dot(x_hi, wr_lo_ref[...], preferred_element_type=f32) + br_ref[...])
        is_g = lane < N_GROUPS
        mg = jnp.max(jnp.where(is_g, lg, -jnp.inf), axis=1, keepdims=True)
        g_prob = 1.0 / jnp.sum(jnp.where(is_g, jnp.exp(lg - mg), 0.0), axis=1, keepdims=True)
        g_idx = jnp.min(jnp.where(is_g & (lg == mg), lane, float(LANES)), axis=1, keepdims=True)
        lo_e = N_GROUPS + EXPERTS_PER_GROUP * g_idx
        is_e = (lane >= lo_e) & (lane < lo_e + EXPERTS_PER_GROUP)
        me = jnp.max(jnp.where(is_e, lg, -jnp.inf), axis=1, keepdims=True)
        ee = jnp.where(is_e, jnp.exp(lg - me), 0.0)
        pe = jnp.where(is_e, ee / jnp.sum(ee, axis=1, keepdims=True), -1.0)
        p1 = jnp.max(pe, axis=1, keepdims=True)
        first = jnp.min(jnp.where(pe == p1, lane, float(LANES)), axis=1, keepdims=True)
        pe2 = jnp.where(lane == first, -1.0, pe)
        p2 = jnp.max(pe2, axis=1, keepdims=True)
        second = jnp.min(jnp.where(pe2 == p2, lane, float(LANES)), axis=1, keepdims=True)
        norm = g_prob / (p1 + p2)
        comb_scr[...] = jnp.where(lane == first, p1 * norm, jnp.where(lane == second, p2 * norm, 0.0))
        acc_scr[...] = jnp.zeros(acc_scr.shape, f32)

    xb = xb_scr[...]
    hg = jnp.dot(xb, wg_ref[0], preferred_element_type=f32)
    hu = jnp.dot(xb, wu_ref[0], preferred_element_type=f32)
    c = jnp.sum(jnp.where(lane == (e + N_GROUPS).astype(f32), comb_scr[...], 0.0), axis=1, keepdims=True)
    hid = hg * jax.nn.sigmoid(hg) * hu * c
    acc_scr[...] += jnp.dot(hid.astype(jnp.bfloat16), wd_ref[0], preferred_element_type=f32)

    @pl.when(e == pl.num_programs(1) - 1)
    def _finish():
        out_ref[...] = _ln_rows(DN_ALPHA * x_ref[...] + acc_scr[...], lng_ref[...], lnb_ref[...])


def moe_ln(x, w_group, b_group, w_router, b_router, wg, wu, wd, lng, lnb):
    m, d = x.shape
    tm = min(MOE_ROW_BLOCK, m)
    assert m % tm == 0 and N_GROUPS + N_EXPERTS <= LANES
    pad = jnp.zeros((d, LANES - N_GROUPS - N_EXPERTS), jnp.float32)
    wr_hi, wr_lo = _split_bf16(jnp.concatenate([w_group, w_router, pad], axis=1))
    br = jnp.concatenate([b_group, b_router, pad[0]]).reshape(1, LANES)
    lnv = [a.reshape(1, -1) for a in (lng, lnb)]

    def whole(a):
        return pl.BlockSpec(a.shape, lambda i, e: (0,) * a.ndim, pipeline_mode=pl.Buffered(1))

    return pl.pallas_call(
        _moe_kernel,
        grid=(m // tm, N_EXPERTS),
        in_specs=[pl.BlockSpec((tm, d), lambda i, e: (i, 0)), whole(wr_hi), whole(wr_lo), whole(br),
                  pl.BlockSpec((1, d, D_EXPERT), lambda i, e: (e, 0, 0)),
                  pl.BlockSpec((1, d, D_EXPERT), lambda i, e: (e, 0, 0)),
                  pl.BlockSpec((1, D_EXPERT, d), lambda i, e: (e, 0, 0))] + [whole(a) for a in lnv],
        out_specs=pl.BlockSpec((tm, d), lambda i, e: (i, 0)),
        out_shape=jax.ShapeDtypeStruct((m, d), jnp.float32),
        scratch_shapes=[pltpu.VMEM((tm, d), jnp.bfloat16), pltpu.VMEM((tm, LANES), jnp.float32),
                        pltpu.VMEM((tm, d), jnp.float32)],
        compiler_params=pltpu.CompilerParams(dimension_semantics=("parallel", "arbitrary"),
                                             vmem_limit_bytes=VMEM_LIMIT_BYTES),
        name="moe_ln",
    )(x, wr_hi, wr_lo, br, wg, wu, wd, *lnv)


SSD_HIST_ROWS = 8


def _ssd_kernel(xbc_ref, dt_ref, hist_ref, cw_ref, cb_ref, dtb_ref, alog_ref, dexp_ref, h0_ref,
                y_ref, hT_ref, xp_scr, ht_scr, xe_scr):
    f32, i32, bf16 = jnp.float32, jnp.int32, jnp.bfloat16
    c = pl.program_id(1)
    L = xbc_ref.shape[1]
    n_hist = SSM_CONV - 1
    base = SSD_HIST_ROWS - n_hist
    gw = SSM_STATE
    heads_per_group = SSM_HEADS // SSM_GROUPS
    pairs_per_group = heads_per_group // 2

    @pl.when(c == 0)
    def _init():
        xp_scr[base:SSD_HIST_ROWS, :] = hist_ref[0]
        ht_scr[...] = h0_ref[0]

    xp_scr[SSD_HIST_ROWS:SSD_HIST_ROWS + L, :] = xbc_ref[0]
    acc = cw_ref[0:1, :] * xp_scr[base:base + L, :]
    for j in range(1, SSM_CONV):
        acc = acc + cw_ref[j:j + 1, :] * xp_scr[base + j:base + j + L, :]
    acc = acc + cb_ref[...]
    xbc = acc * jax.nn.sigmoid(acc)
    xp_scr[base:SSD_HIST_ROWS, :] = xp_scr[base + L:SSD_HIST_ROWS + L, :]
    xs = xbc[:, 0:SSM_INNER]
    bm = xbc[:, SSM_INNER:SSM_INNER + SSM_GROUPS * gw]
    cm = xbc[:, SSM_INNER + SSM_GROUPS * gw:SSM_INNER + 2 * SSM_GROUPS * gw]

    x_dt = dt_ref[0] + dtb_ref[...]
    dtf = jnp.maximum(x_dt, 0.0) + jnp.log1p(jnp.exp(-jnp.abs(x_dt)))
    la = dtf * (-jnp.exp(alog_ref[...]))
    row = lax.broadcasted_iota(i32, (L, L), 0)
    colk = lax.broadcasted_iota(i32, (L, L), 1)
    causal = colk <= row
    tril = jnp.where(causal, 1.0, 0.0).astype(bf16)
    la_hi = la.astype(bf16)
    r1 = la - la_hi.astype(f32)
    la_mid = r1.astype(bf16)
    la_lo = (r1 - la_mid.astype(f32)).astype(bf16)
    cs = (jnp.dot(tril, la_hi, preferred_element_type=f32) + jnp.dot(tril, la_mid, preferred_element_type=f32)
          + jnp.dot(tril, la_lo, preferred_element_type=f32))
    cs_t = cs.T
    ecs = jnp.exp(cs)
    to_end = jnp.exp(cs[L - 1:L, :] - cs)

    lane = lax.broadcasted_iota(i32, (L, LANES), 1)
    half0 = lane < SSM_HEAD_DIM

    def pair_cols(a, h0):
        return jnp.where(half0, a[:, h0:h0 + 1], a[:, h0 + 1:h0 + 2])

    for g in range(SSM_GROUPS):
        b_g = bm[:, g * gw:(g + 1) * gw]
        c_g = cm[:, g * gw:(g + 1) * gw].astype(bf16)
        cb = _nt_dot(c_g, b_g.astype(bf16))
        cols = slice(g * heads_per_group * SSM_HEAD_DIM, (g + 1) * heads_per_group * SSM_HEAD_DIM)
        y_off = jnp.dot(c_g, ht_scr[:, cols].astype(bf16), preferred_element_type=f32)
        for i in range(pairs_per_group):
            h0 = g * heads_per_group + 2 * i
            pc = slice((h0 // 2) * LANES, (h0 // 2 + 1) * LANES)
            xs_p = xs[:, pc]
            xdt_p = xs_p * pair_cols(dtf, h0)
            att = []
            for hh in range(2):
                seg = cs[:, h0 + hh:h0 + hh + 1] - cs_t[h0 + hh:h0 + hh + 1, :]
                att.append((cb * jnp.exp(jnp.where(causal, seg, -jnp.inf))).astype(bf16))
            rhs = jnp.concatenate([jnp.where(half0, xdt_p, 0.0), jnp.where(half0, 0.0, xdt_p)], axis=0).astype(bf16)
            y_diag = jnp.dot(jnp.concatenate(att, axis=1), rhs, preferred_element_type=f32)
            xe_scr[:, pc] = xdt_p * pair_cols(to_end, h0)
            y_ref[0, :, pc] = (y_diag + y_off[:, i * LANES:(i + 1) * LANES] * pair_cols(ecs, h0)
                               + dexp_ref[:, pc] * xs_p)
        st = jnp.dot(b_g.T.astype(bf16), xe_scr[:, cols].astype(bf16), preferred_element_type=f32)
        for i in range(pairs_per_group):
            h0 = g * heads_per_group + 2 * i
            pc = slice((h0 // 2) * LANES, (h0 // 2 + 1) * LANES)
            decay = jnp.where(half0[0:1, :], ecs[L - 1:L, h0:h0 + 1], ecs[L - 1:L, h0 + 1:h0 + 2])
            ht_scr[:, pc] = ht_scr[:, pc] * decay + st[:, i * LANES:(i + 1) * LANES]

    @pl.when(c == pl.num_programs(1) - 1)
    def _done():
        hT_ref[0] = ht_scr[...]


def ssd_prompt(proj, hist, conv_w, conv_b, dt_bias, a_log, d_skip, h0_t):
    b, t, _ = proj.shape
    L = SSM_CHUNK
    assert t % L == 0 and SSM_HEAD_DIM * 2 == LANES and SSM_STATE == LANES and SSM_HEADS <= LANES

    def lane_pad(a):
        return jnp.concatenate([a, jnp.zeros((LANES - a.shape[0],), a.dtype)]).reshape(1, LANES)

    def whole(a):
        return pl.BlockSpec(a.shape, lambda bi, c: (0,) * a.ndim)

    params = [conv_w, conv_b.reshape(1, -1), lane_pad(dt_bias), lane_pad(a_log),
              jnp.repeat(d_skip, SSM_HEAD_DIM).reshape(1, SSM_INNER)]
    return pl.pallas_call(
        _ssd_kernel,
        grid=(b, t // L),
        in_specs=[pl.BlockSpec((1, L, SSM_XBC), lambda bi, c: (bi, c, COL_XBC // SSM_XBC)),
                  pl.BlockSpec((1, L, LANES), lambda bi, c: (bi, c, COL_DT // LANES)),
                  pl.BlockSpec((1, SSM_CONV - 1, SSM_XBC), lambda bi, c: (bi, 0, 0))]
                 + [whole(a) for a in params]
                 + [pl.BlockSpec((1, SSM_STATE, SSM_INNER), lambda bi, c: (bi, 0, 0))],
        out_specs=[pl.BlockSpec((1, L, SSM_INNER), lambda bi, c: (bi, c, 0)),
                   pl.BlockSpec((1, SSM_STATE, SSM_INNER), lambda bi, c: (bi, 0, 0))],
        out_shape=[jax.ShapeDtypeStruct((b, t, SSM_INNER), jnp.float32),
                   jax.ShapeDtypeStruct((b, SSM_STATE, SSM_INNER), jnp.float32)],
        scratch_shapes=[pltpu.VMEM((SSD_HIST_ROWS + L, SSM_XBC), jnp.float32),
                        pltpu.VMEM((SSM_STATE, SSM_INNER), jnp.float32),
                        pltpu.VMEM((L, SSM_INNER), jnp.float32)],
        compiler_params=pltpu.CompilerParams(dimension_semantics=("parallel", "arbitrary"),
                                             vmem_limit_bytes=VMEM_LIMIT_BYTES),
        name="ssd_prompt",
    )(proj, proj, hist, *params, h0_t)


def causal_dwconv(x_pad, w):
    return lax.conv_general_dilated(x_pad, w[:, None, :], window_strides=(1,), padding='VALID',
                                    dimension_numbers=('NWC', 'WIO', 'NWC'),
                                    feature_group_count=x_pad.shape[-1])


def ssd_scan(x, dt, a, bm, cm, h0):
    f32 = jnp.float32
    bsz, l, nh, hp = x.shape
    rep = nh // bm.shape[2]
    chunk = min(SSM_CHUNK, l)
    assert l % chunk == 0
    xdt = x.astype(f32) * dt[..., None]
    la = dt * a
    bh = jnp.repeat(bm.astype(f32), rep, axis=2)
    ch = jnp.repeat(cm.astype(f32), rep, axis=2)
    nc = l // chunk
    ns = bh.shape[-1]
    xdt = xdt.reshape(bsz, nc, chunk, nh, hp)
    la = la.reshape(bsz, nc, chunk, nh)
    bh = bh.reshape(bsz, nc, chunk, nh, ns)
    ch = ch.reshape(bsz, nc, chunk, nh, ns)
    cs = jnp.cumsum(la, axis=2)
    causal = jnp.tril(jnp.ones((chunk, chunk), bool))
    seg = cs[:, :, :, None, :] - cs[:, :, None, :, :]
    decay = jnp.exp(jnp.where(causal[None, None, :, :, None], seg, -jnp.inf))
    att = jnp.einsum('bcqhn,bckhn->bcqkh', ch, bh) * decay
    y_diag = jnp.einsum('bcqkh,bckhp->bcqhp', att, xdt)
    to_end = jnp.exp(cs[:, :, -1:, :] - cs)
    states = jnp.einsum('bckhn,bckh,bckhp->bchpn', bh, to_end, xdt)
    chunk_decay = jnp.exp(cs[:, :, -1, :])

    def step(hc, inp):
        dc, st = inp
        return hc * dc[:, :, None, None] + st, hc

    h_last, h_prev = lax.scan(step, h0.astype(f32), (chunk_decay.swapaxes(0, 1), states.swapaxes(0, 1)))
    h_prev = h_prev.swapaxes(0, 1)
    y_off = jnp.einsum('bcqhn,bchpn,bcqh->bcqhp', ch, h_prev, jnp.exp(cs))
    y = (y_diag + y_off).reshape(bsz, nc * chunk, nh, hp)
    return y, h_last


def token_mixer(x, p, conv_buf, ssm_buf, ssm_h0, past):
    (w_in, conv_dw, ln_conv_g, ln_conv_b, w_conv_out, w_attn_out, ssm_conv_w, ssm_conv_b,
     ssm_dt_bias, ssm_a_log, ssm_d, ssm_norm_g, w_ssm_out, w_mix_out, ln_mix_g, ln_mix_b) = p
    b, t, _ = x.shape
    proj = mm3(x, w_in)
    glu_in = proj[..., COL_GLU:COL_GLU + 2 * CONV_DIM]
    k = proj[..., COL_K:COL_K + LANES]
    v = proj[..., COL_V:COL_V + LANES]
    small = proj[..., COL_SMALL:COL_SMALL + LANES]
    ki = small[..., 0:IDX_DIM]
    dt = small[..., IDX_DIM + IDX_HEADS:IDX_DIM + IDX_HEADS + SSM_HEADS]
    xbc = proj[..., COL_XBC:COL_XBC + SSM_XBC]
    glu = glu_in[..., :CONV_DIM] * jax.nn.sigmoid(glu_in[..., CONV_DIM:])
    glu_pad = jnp.concatenate([conv_buf, glu], axis=1)
    cv = causal_dwconv(glu_pad, conv_dw)
    if past is None:
        o = dsa_prompt(proj, k, v, ki)
    else:
        cache_k, cache_v, cache_ik, page_table, layer = past
        o = dsa_decode(proj, k, v, cache_k, cache_v, cache_ik, page_table, layer)
    if past is None:
        h0_t = ssm_h0.reshape(b, SSM_INNER, SSM_STATE).swapaxes(1, 2)
        y, h_t = ssd_prompt(proj, ssm_buf, ssm_conv_w, ssm_conv_b, ssm_dt_bias, ssm_a_log, ssm_d, h0_t)
        h_last = h_t.swapaxes(1, 2).reshape(b, SSM_HEADS, SSM_HEAD_DIM, SSM_STATE)
        xbc_pad = xbc[:, -(SSM_CONV - 1):]
    else:
        xbc_pad = jnp.concatenate([ssm_buf, xbc], axis=1)
        xbc_c = jax.nn.silu(causal_dwconv(xbc_pad, ssm_conv_w) + ssm_conv_b)
        xs, bm, cm = jnp.split(xbc_c, [SSM_INNER, SSM_INNER + SSM_GROUPS * SSM_STATE], axis=-1)
        xs = xs.reshape(b, t, SSM_HEADS, SSM_HEAD_DIM)
        dtf = jax.nn.softplus(dt + ssm_dt_bias)
        a = -jnp.exp(ssm_a_log)
        y, h_last = ssd_scan(xs, dtf, a, bm.reshape(b, t, SSM_GROUPS, SSM_STATE),
                             cm.reshape(b, t, SSM_GROUPS, SSM_STATE), ssm_h0)
        y = (y + ssm_d[:, None] * xs).reshape(b, t, SSM_INNER)
    m = b * t
    x_new = branch_mix_ln(cv.reshape(m, CONV_DIM), o.reshape(m, -1), y.reshape(m, SSM_INNER),
                          proj.reshape(m, D_IN_PAD), x.reshape(m, D_MODEL), ln_conv_g, ln_conv_b, ssm_norm_g,
                          w_conv_out, w_attn_out, w_ssm_out, w_mix_out, ln_mix_g, ln_mix_b).reshape(b, t, D_MODEL)
    return (x_new, k.reshape(b, t, N_KV_HEADS, HEAD_DIM), v.reshape(b, t, N_KV_HEADS, HEAD_DIM), ki,
            glu_pad[:, -(CONV_WIDTH - 1):], xbc_pad[:, -(SSM_CONV - 1):], h_last)


def _pad_w_in(w):
    sp = (0,) + SPLIT_POINTS + (w.shape[1],)
    seg = [w[:, sp[i]:sp[i + 1]] for i in range(len(SPLIT_SIZES))]
    glu, q, k, v, qi, ki, wi, z, xbc, dt, gates = seg
    pad = jnp.zeros((w.shape[0], LANES - IDX_DIM - IDX_HEADS - SSM_HEADS), w.dtype)
    n_rep = N_HEADS // N_KV_HEADS
    q = q.reshape(-1, N_KV_HEADS, n_rep, HEAD_DIM).swapaxes(1, 2).reshape(q.shape)
    dt_pad = jnp.zeros((w.shape[0], LANES - SSM_HEADS), w.dtype)
    out = jnp.concatenate([gates, xbc, q, glu, z, qi, k, v, ki, wi, dt, pad, dt, dt_pad], axis=1)
    assert out.shape[1] == D_IN_PAD
    return out


def _perm_w_attn_out(w):
    n_rep = N_HEADS // N_KV_HEADS
    return w.reshape(N_KV_HEADS, n_rep, HEAD_DIM, -1).swapaxes(0, 1).reshape(w.shape)


def kernel(x_prompt, x_sample, mem_prompt, cache_k, cache_v, cache_ik, cache_mem_k, cache_mem_v, state_conv, state_ssm_conv, state_ssm, page_table, w_in, conv_dw, ln_conv_g, ln_conv_b, w_conv_out, w_attn_out, ssm_conv_w, ssm_conv_b, ssm_dt_bias, ssm_a_log, ssm_d, ssm_norm_g, w_ssm_out, w_mix_out, ln_mix_g, ln_mix_b, w_mq, w_mk, w_mv, w_mo, ln_mem_g, ln_mem_b, w_group, b_group, w_router, b_router, w_e_gate, w_e_up, w_e_down, ln_ffn_g, ln_ffn_b):
    bf16 = jnp.bfloat16
    bp, tp, _ = x_prompt.shape
    bs, ts, _ = x_sample.shape
    xp, xs = x_prompt, x_sample
    outs_p = [[] for _ in range(8)]
    outs_s = [[] for _ in range(6)]
    for l in range(DEPTH):
        mix_p = (_pad_w_in(w_in[l]).astype(bf16), conv_dw[l], ln_conv_g[l], ln_conv_b[l],
                 w_conv_out[l].astype(bf16), _perm_w_attn_out(w_attn_out[l]).astype(bf16),
                 ssm_conv_w[l], ssm_conv_b[l], ssm_dt_bias[l], ssm_a_log[l], ssm_d[l], ssm_norm_g[l],
                 w_ssm_out[l].astype(bf16), w_mix_out[l].astype(bf16), ln_mix_g[l], ln_mix_b[l])
        moe_p = (w_group[l], b_group[l], w_router[l], b_router[l],
                 w_e_gate[l].astype(bf16), w_e_up[l].astype(bf16), w_e_down[l].astype(bf16), ln_ffn_g[l], ln_ffn_b[l])
        mem_p = (w_mq[l].astype(bf16), w_mo[l].astype(bf16), ln_mem_g[l], ln_mem_b[l])
        hd_mem = MEM_HEADS * MEM_HEAD_DIM
        xp, kp, vp, kip, cbp, sbp, hp = token_mixer(
            xp, mix_p,
            jnp.zeros((bp, CONV_WIDTH - 1, CONV_DIM), xp.dtype),
            jnp.zeros((bp, SSM_CONV - 1, SSM_XBC), xp.dtype),
            jnp.zeros((bp, SSM_HEADS, SSM_HEAD_DIM, SSM_STATE), jnp.float32),
            None)
        mem_kv = mm(mem_prompt.reshape(bp * N_MEM, D_MODEL),
                    jnp.concatenate([w_mk[l], w_mv[l]], axis=1).astype(bf16)).reshape(bp, N_MEM, 2 * hd_mem)
        mkp = mem_kv[..., :hd_mem].reshape(bp, N_MEM, MEM_HEADS, MEM_HEAD_DIM)
        mvp = mem_kv[..., hd_mem:].reshape(bp, N_MEM, MEM_HEADS, MEM_HEAD_DIM)
        xp = mem_attn_ln(xp.reshape(bp * tp, D_MODEL), mkp, mvp, *mem_p, t_seq=tp)
        xp = moe_ln(xp, *moe_p).reshape(bp, tp, D_MODEL)
        xs, ks_new, vs_new, kis, cbs, sbs, hs = token_mixer(
            xs, mix_p, state_conv[:, l], state_ssm_conv[:, l], state_ssm[:, l],
            (cache_k, cache_v, cache_ik, page_table, l))
        xs = mem_attn_ln(xs.reshape(bs * ts, D_MODEL), cache_mem_k, cache_mem_v, *mem_p, t_seq=ts, layer=l)
        xs = moe_ln(xs, *moe_p).reshape(bs, ts, D_MODEL)
        for lst, arr in zip(outs_p, (kp, vp, kip, mkp, mvp, cbp, sbp, hp)):
            lst.append(arr)
        for lst, arr in zip(outs_s, (ks_new, vs_new, kis, cbs, sbs, hs)):
            lst.append(arr)
    p_k, p_v, p_ik, p_mem_k, p_mem_v, p_conv, p_ssm_conv, p_ssm = [jnp.stack(a, axis=1) for a in outs_p]
    s_k, s_v, s_ik, s_conv, s_ssm_conv, s_ssm = [jnp.stack(a, axis=1) for a in outs_s]
    return (xp, xs, p_k, p_v, p_ik, p_mem_k, p_mem_v, p_conv, p_ssm_conv, p_ssm,
            s_k, s_v, s_ik, s_conv, s_ssm_conv, s_ssm)
```

```python
import functools
import math

import jax
import jax.numpy as jnp
from jax import lax
from jax.experimental import pallas as pl
from jax.experimental.pallas import tpu as pltpu

D_MODEL = 1024
DEPTH = 2
PAGE_SIZE = 128
CONV_DIM = 512
CONV_WIDTH = 31
N_HEADS = 8
N_KV_HEADS = 2
HEAD_DIM = 64
IDX_HEADS = 4
IDX_DIM = 64
TOPK_MAX = 256
Q_BLOCK = 128
SSM_HEADS = 16
SSM_HEAD_DIM = 64
SSM_INNER = SSM_HEADS * SSM_HEAD_DIM
SSM_GROUPS = 2
SSM_STATE = 128
SSM_CONV = 4
SSM_CHUNK = 128
SSM_XBC = SSM_INNER + 2 * SSM_GROUPS * SSM_STATE
N_MEM = 256
MEM_HEADS = 4
MEM_HEAD_DIM = 128
N_GROUPS = 4
EXPERTS_PER_GROUP = 4
N_EXPERTS = N_GROUPS * EXPERTS_PER_GROUP
TOPK_IN_GROUP = 2
D_EXPERT = 512
DN_ALPHA = (2 * DEPTH) ** 0.25
LN_EPS = 1e-5

SPLIT_SIZES = (2 * CONV_DIM, N_HEADS * HEAD_DIM, N_KV_HEADS * HEAD_DIM, N_KV_HEADS * HEAD_DIM,
               IDX_HEADS * IDX_DIM, IDX_DIM, IDX_HEADS, SSM_INNER, SSM_XBC, SSM_HEADS, 3 * D_MODEL)
SPLIT_POINTS = tuple(sum(SPLIT_SIZES[:i + 1]) for i in range(len(SPLIT_SIZES) - 1))

VMEM_LIMIT_BYTES = 56 * 1024 * 1024
MM_COL_CHUNK = 512


def _mm_kernel(x_ref, w_ref, o_ref):
    xb = x_ref[...].astype(jnp.bfloat16)
    n = o_ref.shape[1]
    for c0 in range(0, n, MM_COL_CHUNK):
        c1 = min(n, c0 + MM_COL_CHUNK)
        o_ref[:, c0:c1] = jnp.dot(xb, w_ref[:, c0:c1], preferred_element_type=jnp.float32)


def mm(x, w, tm=256):
    m, k = x.shape
    n = w.shape[1]
    tm = min(tm, m)
    assert m % tm == 0 and n % 128 == 0 and k % 128 == 0
    return pl.pallas_call(
        _mm_kernel,
        grid=(m // tm,),
        in_specs=[pl.BlockSpec((tm, k), lambda i: (i, 0)),
                  pl.BlockSpec((k, n), lambda i: (0, 0), pipeline_mode=pl.Buffered(1))],
        out_specs=pl.BlockSpec((tm, n), lambda i: (i, 0)),
        out_shape=jax.ShapeDtypeStruct((m, n), jnp.float32),
        compiler_params=pltpu.CompilerParams(dimension_semantics=("parallel",),
                                             vmem_limit_bytes=VMEM_LIMIT_BYTES),
    )(x, w)


def mm3(x, w, tm=256):
    b, t, d = x.shape
    return mm(x.reshape(b * t, d), w, tm).reshape(b, t, w.shape[1])


INT32_MIN = -2 ** 31
MASK_BIAS = -1e30
DSA_KEY_CHUNK = 512
HALF_BIAS = 2 ** 15
LANES = 128


def _nt_dot(a, b):
    return lax.dot_general(a, b, (((1,), (1,)), ((), ())), preferred_element_type=jnp.float32)


def _dsa_prompt_kernel(q_ref, qi_ref, sm_ref, k_ref, v_ref, ki_ref, o_ref,
                       key_scr, khi_scr, klo_scr, rank_scr, s_scr, p_scr, qpad_scr, qis_scr, m_scr, l_scr, acc_scr,
                       *, k_sel):
    f32, i32, i16, bf16 = jnp.float32, jnp.int32, jnp.int16, jnp.bfloat16
    qb = q_ref.shape[1]
    kc_w = DSA_KEY_CHUNK
    sub = kc_w // LANES
    i = pl.program_id(1)
    n_chunks = (i * qb + qb + kc_w - 1) // kc_w
    lane = lax.broadcasted_iota(i32, (qb, LANES), 1)
    half = [lane < HEAD_DIM, lane >= HEAD_DIM]

    for pair in range(IDX_HEADS // 2):
        src = qi_ref[0, :, pair * LANES:(pair + 1) * LANES]
        hi = src.astype(bf16).astype(f32)
        lo_swapped = pltpu.roll(src - hi, IDX_DIM, axis=1)
        for hh in range(2):
            h = 2 * pair + hh
            qis_scr[h * qb:(h + 1) * qb, 0:LANES] = jnp.where(half[hh], hi, lo_swapped).astype(bf16)
            qis_scr[h * qb:(h + 1) * qb, LANES:2 * LANES] = jnp.where(half[hh], hi, 0.0).astype(bf16)
    w_idx = [sm_ref[0, :, IDX_DIM + h:IDX_DIM + h + 1] for h in range(IDX_HEADS)]
    row_pos = i * qb + lax.broadcasted_iota(i32, (qb, kc_w), 0)
    col_iota = lax.broadcasted_iota(i32, (qb, kc_w), 1)

    def score_body(c, carry):
        off = pl.multiple_of(c * kc_w, kc_w)
        d = _nt_dot(qis_scr[...], ki_ref[0, pl.ds(off, kc_w), :])
        idx = w_idx[0] * jnp.maximum(d[0:qb], 0.0)
        for h in range(1, IDX_HEADS):
            idx = idx + w_idx[h] * jnp.maximum(d[h * qb:(h + 1) * qb], 0.0)
        idx = jnp.where(idx == 0.0, 0.0, idx)
        bits = lax.bitcast_convert_type(idx, i32)
        key = bits ^ ((bits >> 31) & 0x7FFFFFFF)
        key = jnp.where(off + col_iota <= row_pos, key, INT32_MIN)
        key_scr[c] = key
        khi_scr[c] = (key >> 16).astype(i16)
        klo_scr[c] = ((key & 0xFFFF) - HALF_BIAS).astype(i16)
        return carry

    lax.fori_loop(0, n_chunks, score_body, 0)

    def count16(ref, pred):
        def body(c, acc):
            for g in range(sub):
                acc = acc + jnp.where(pred(ref[c, :, g * LANES:(g + 1) * LANES]), jnp.int16(1), jnp.int16(0))
            return acc
        acc = lax.fori_loop(0, n_chunks, body, jnp.zeros((qb, LANES), i16))
        return jnp.sum(acc.astype(f32), axis=1, keepdims=True)

    def search16(ref, k_row):
        def body(it, lo):
            cand = lo + lax.shift_left(jnp.int32(1), 15 - it)
            cand_b = jnp.broadcast_to(cand, (qb, LANES)).astype(i16)
            cnt = count16(ref, lambda kk: kk >= cand_b)
            return jnp.where(cnt >= k_row, cand, lo)
        return lax.fori_loop(0, 16, body, jnp.full((qb, 1), -HALF_BIAS, i32))

    t_hi = search16(khi_scr, float(k_sel))
    t_hi_b = jnp.broadcast_to(t_hi, (qb, LANES)).astype(i16)
    k_low = float(k_sel) - count16(khi_scr, lambda kk: kk > t_hi_b)
    t_hi_w = jnp.tile(t_hi_b, (1, sub))

    def low_body(c, carry):
        klo_scr[c] = jnp.where(khi_scr[c] == t_hi_w, klo_scr[c], jnp.int16(-HALF_BIAS))
        return carry

    lax.fori_loop(0, n_chunks, low_body, 0)
    t_lo = search16(klo_scr, k_low)
    t_lo_b = jnp.broadcast_to(t_lo, (qb, LANES)).astype(i16)
    need_b = jnp.broadcast_to(k_low - count16(klo_scr, lambda kk: kk > t_lo_b), (qb, LANES))
    thr = t_hi * (2 * HALF_BIAS) + (t_lo + HALF_BIAS)
    thr_w = jnp.broadcast_to(thr, (qb, kc_w))
    need_w = jnp.tile(need_b, (1, sub))

    r_i = lax.broadcasted_iota(i32, (kc_w, kc_w + LANES), 0)
    c_i = lax.broadcasted_iota(i32, (kc_w, kc_w + LANES), 1)
    rank_scr[...] = jnp.where((r_i <= c_i) | (c_i >= kc_w), 1.0, 0.0).astype(bf16)
    zero_bits = jnp.zeros((qb, kc_w), i32)
    mask_bits = lax.bitcast_convert_type(jnp.full((qb, kc_w), MASK_BIAS, f32), i32)

    def select_body(c, seen):
        kk = key_scr[c]
        eq = kk == thr_w
        pr = jnp.dot(jnp.where(eq, 1.0, 0.0).astype(bf16), rank_scr[...], preferred_element_type=f32)
        rank = jnp.tile(seen, (1, sub)) + pr[:, 0:kc_w]
        sel = ((kk > thr_w) | (eq & (rank <= need_w))) & (kk > INT32_MIN)
        key_scr[c] = jnp.where(sel, zero_bits, mask_bits)
        return seen + pr[:, kc_w:kc_w + LANES]

    lax.fori_loop(0, n_chunks, select_body, jnp.zeros((qb, LANES), f32))

    n_blk = N_HEADS
    scale = 1.0 / math.sqrt(HEAD_DIM)
    for g in range(N_HEADS // N_KV_HEADS):
        src = q_ref[0, :, g * LANES:(g + 1) * LANES] * scale
        for j in range(N_KV_HEADS):
            r = g * N_KV_HEADS + j
            qpad_scr[r * qb:(r + 1) * qb, :] = jnp.where(half[j], src, 0.0).astype(bf16)
    m_scr[...] = jnp.full(m_scr.shape, -jnp.inf, f32)
    l_scr[...] = jnp.zeros(l_scr.shape, f32)
    acc_scr[...] = jnp.zeros(acc_scr.shape, f32)

    def attend_body(c, carry):
        off = pl.multiple_of(c * kc_w, kc_w)
        k_c = k_ref[0, pl.ds(off, kc_w), :]
        v_c = v_ref[0, pl.ds(off, kc_w), :]
        bias = lax.bitcast_convert_type(key_scr[c], f32)
        s_scr[...] = _nt_dot(qpad_scr[...], k_c)
        for r in range(n_blk):
            rows = slice(r * qb, (r + 1) * qb)
            s = s_scr[rows, :] + bias
            m_old = m_scr[rows, :]
            m_new = jnp.maximum(m_old, jnp.broadcast_to(jnp.max(s, axis=1, keepdims=True), (qb, LANES)))
            alpha = jnp.exp(m_old - m_new)
            p = jnp.exp(s - jnp.tile(m_new, (1, sub)))
            l_scr[rows, :] = alpha * l_scr[rows, :] + jnp.broadcast_to(jnp.sum(p, axis=1, keepdims=True), (qb, LANES))
            acc_scr[rows, :] = alpha * acc_scr[rows, :]
            p_scr[rows, :] = p.astype(bf16)
            m_scr[rows, :] = m_new
        acc_scr[...] += jnp.dot(p_scr[...], v_c, preferred_element_type=f32)
        return carry

    lax.fori_loop(0, n_chunks, attend_body, 0)

    for g in range(N_HEADS // N_KV_HEADS):
        outs = []
        for j in range(N_KV_HEADS):
            rows = slice((g * N_KV_HEADS + j) * qb, (g * N_KV_HEADS + j + 1) * qb)
            outs.append(acc_scr[rows, :] / l_scr[rows, :])
        o_ref[0, :, g * LANES:(g + 1) * LANES] = jnp.where(half[0], outs[0], outs[1])


def _split_bf16(x):
    hi = x.astype(jnp.bfloat16)
    return hi, (x - hi.astype(jnp.float32)).astype(jnp.bfloat16)


def dsa_prompt(proj, k, v, ki):
    b, t, _ = proj.shape
    qb = Q_BLOCK
    w_q, w_qi = N_HEADS * HEAD_DIM, IDX_HEADS * IDX_DIM
    assert t % DSA_KEY_CHUNK == 0 and t % qb == 0
    assert IDX_DIM == HEAD_DIM == LANES // 2 and IDX_HEADS % 2 == 0
    k_sel = min(TOPK_MAX, t // 4)
    n_rows = N_HEADS * qb
    ki_hi, ki_lo = _split_bf16(ki)
    ki4 = jnp.concatenate([ki_hi, ki_hi, ki_lo, ki_lo], axis=-1)
    k, v = k.astype(jnp.bfloat16), v.astype(jnp.bfloat16)
    seq_spec = pl.BlockSpec((1, t, LANES), lambda bi, i: (bi, 0, 0))
    return pl.pallas_call(
        functools.partial(_dsa_prompt_kernel, k_sel=k_sel),
        grid=(b, t // qb),
        in_specs=[pl.BlockSpec((1, qb, w_q), lambda bi, i: (bi, i, COL_Q // w_q)),
                  pl.BlockSpec((1, qb, w_qi), lambda bi, i: (bi, i, COL_QI // w_qi)),
                  pl.BlockSpec((1, qb, LANES), lambda bi, i: (bi, i, COL_SMALL // LANES)),
                  seq_spec, seq_spec, pl.BlockSpec((1, t, 2 * LANES), lambda bi, i: (bi, 0, 0))],
        out_specs=pl.BlockSpec((1, qb, N_HEADS * HEAD_DIM), lambda bi, i: (bi, i, 0)),
        out_shape=jax.ShapeDtypeStruct((b, t, N_HEADS * HEAD_DIM), jnp.float32),
        scratch_shapes=[pltpu.VMEM((t // DSA_KEY_CHUNK, qb, DSA_KEY_CHUNK), jnp.int32),
                        pltpu.VMEM((t // DSA_KEY_CHUNK, qb, DSA_KEY_CHUNK), jnp.int16),
                        pltpu.VMEM((t // DSA_KEY_CHUNK, qb, DSA_KEY_CHUNK), jnp.int16),
                        pltpu.VMEM((DSA_KEY_CHUNK, DSA_KEY_CHUNK + LANES), jnp.bfloat16),
                        pltpu.VMEM((n_rows, DSA_KEY_CHUNK), jnp.float32),
                        pltpu.VMEM((n_rows, DSA_KEY_CHUNK), jnp.bfloat16),
                        pltpu.VMEM((n_rows, LANES), jnp.bfloat16),
                        pltpu.VMEM((IDX_HEADS * qb, 2 * LANES), jnp.bfloat16),
                        pltpu.VMEM((n_rows, LANES), jnp.float32),
                        pltpu.VMEM((n_rows, LANES), jnp.float32),
                        pltpu.VMEM((n_rows, LANES), jnp.float32)],
        compiler_params=pltpu.CompilerParams(dimension_semantics=("parallel", "arbitrary"),
                                             vmem_limit_bytes=VMEM_LIMIT_BYTES),
        name="dsa_prompt",
    )(proj, proj, proj, k, v, ki4)


DEC_ROWS = 8


def _dsa_decode_kernel(pt_ref, q_ref, qi_ref, sm_ref, kn_ref, vn_ref, *rest, n_pages, k_sel):
    f32, i32, bf16 = jnp.float32, jnp.int32, jnp.bfloat16
    k_pages = rest[0:n_pages]
    v_pages = rest[n_pages:2 * n_pages]
    ik_pages = rest[2 * n_pages:3 * n_pages]
    o_ref = rest[3 * n_pages]
    kall, vall, ikall, new_scr, qis, wpad, qpad, key_scr, p_scr = rest[3 * n_pages + 1:]
    t_new = q_ref.shape[1]
    past = n_pages * PAGE_SIZE
    s_pad = past + PAGE_SIZE
    n_chunks = s_pad // LANES
    rows = DEC_ROWS

    for p in range(n_pages):
        sl = slice(p * PAGE_SIZE, (p + 1) * PAGE_SIZE)
        for j in range(N_KV_HEADS):
            kall[j * HEAD_DIM:(j + 1) * HEAD_DIM, sl] = k_pages[p][j]
            vall[j * HEAD_DIM:(j + 1) * HEAD_DIM, sl] = v_pages[p][j]
        ikall[:, sl] = ik_pages[p][...]
    tail = slice(past, s_pad)
    for src, dst, n_feat in ((kn_ref[0], kall, LANES), (vn_ref[0], vall, LANES),
                             (sm_ref[0], ikall, IDX_DIM)):
        new_scr[...] = jnp.zeros(new_scr.shape, f32)
        new_scr[0:t_new, :] = src
        dst[:, tail] = new_scr[...].T[0:n_feat, :]

    qis[...] = jnp.zeros(qis.shape, f32)
    wpad[...] = jnp.zeros(wpad.shape, f32)
    for h in range(IDX_HEADS):
        qis[h * rows:h * rows + t_new, :] = qi_ref[0, :, h * IDX_DIM:(h + 1) * IDX_DIM]
    wpad[0:t_new, :] = sm_ref[0]
    q_hi, q_lo = _split_bf16(qis[...])
    k_hi, k_lo = _split_bf16(ikall[...])

    def mm_f32(a, b):
        return jnp.dot(a, b, preferred_element_type=f32)

    d = mm_f32(q_hi, k_hi) + mm_f32(q_lo, k_hi) + mm_f32(q_hi, k_lo)
    idx = wpad[:, IDX_DIM:IDX_DIM + 1] * jnp.maximum(d[0:rows], 0.0)
    for h in range(1, IDX_HEADS):
        idx = idx + wpad[:, IDX_DIM + h:IDX_DIM + h + 1] * jnp.maximum(d[h * rows:(h + 1) * rows], 0.0)
    idx = jnp.where(idx == 0.0, 0.0, idx)
    bits = lax.bitcast_convert_type(idx, i32)
    key = bits ^ ((bits >> 31) & 0x7FFFFFFF)
    col = lax.broadcasted_iota(i32, (rows, s_pad), 1)
    q_pos = past + lax.broadcasted_iota(i32, (rows, s_pad), 0)
    key_scr[...] = jnp.where((col <= q_pos) & (col < past + t_new), key, INT32_MIN)

    def search_body(it, lo):
        cand = lo + lax.shift_left(jnp.int32(1), 31 - it)
        cnt = jnp.sum(jnp.where(key_scr[...] >= cand, 1.0, 0.0), axis=1, keepdims=True)
        return jnp.where(cnt >= float(k_sel), cand, lo)

    thr = lax.fori_loop(0, 32, search_body, jnp.full((rows, 1), INT32_MIN, i32))
    need = float(k_sel) - jnp.sum(jnp.where(key_scr[...] > thr, 1.0, 0.0), axis=1, keepdims=True)
    tri = jnp.where(lax.broadcasted_iota(i32, (LANES, LANES), 0) <= lax.broadcasted_iota(i32, (LANES, LANES), 1),
                    1.0, 0.0).astype(bf16)
    seen = jnp.zeros((rows, 1), f32)
    for c in range(n_chunks):
        kk = key_scr[:, c * LANES:(c + 1) * LANES]
        eq = kk == thr
        rank = seen + jnp.dot(jnp.where(eq, 1.0, 0.0).astype(bf16), tri, preferred_element_type=f32)
        sel = ((kk > thr) | (eq & (rank <= need))) & (kk > INT32_MIN)
        key_scr[:, c * LANES:(c + 1) * LANES] = lax.bitcast_convert_type(jnp.where(sel, 0.0, MASK_BIAS), i32)
        seen = rank[:, LANES - 1:LANES]
    bias = lax.bitcast_convert_type(key_scr[...], f32)

    lane = lax.broadcasted_iota(i32, (t_new, LANES), 1)
    half = [lane < HEAD_DIM, lane >= HEAD_DIM]
    scale = 1.0 / math.sqrt(HEAD_DIM)
    qpad[...] = jnp.zeros(qpad.shape, f32)
    for g in range(N_HEADS // N_KV_HEADS):
        src = q_ref[0, :, g * LANES:(g + 1) * LANES] * scale
        for j in range(N_KV_HEADS):
            r = g * N_KV_HEADS + j
            qpad[r * rows:r * rows + t_new, :] = jnp.where(half[j], src, 0.0)
    s_all = mm_f32(qpad[...].astype(bf16), kall[...].astype(bf16))
    for r in range(N_HEADS):
        s = s_all[r * rows:(r + 1) * rows] + bias
        p = jnp.exp(s - jnp.max(s, axis=1, keepdims=True))
        p_scr[r * rows:(r + 1) * rows, :] = p / jnp.sum(p, axis=1, keepdims=True)
    o_all = _nt_dot(p_scr[...].astype(bf16), vall[...].astype(bf16))
    for g in range(N_HEADS // N_KV_HEADS):
        r0, r1 = g * N_KV_HEADS * rows, (g * N_KV_HEADS + 1) * rows
        o_ref[0, :, g * LANES:(g + 1) * LANES] = jnp.where(half[0], o_all[r0:r0 + t_new], o_all[r1:r1 + t_new])


def dsa_decode(proj, k_new, v_new, cache_k, cache_v, cache_ik, page_table, layer):
    b, t_new, _ = proj.shape
    n_pages = page_table.shape[1]
    assert t_new <= DEC_ROWS and PAGE_SIZE == LANES
    s_pad = (n_pages + 1) * PAGE_SIZE
    k_sel = min(TOPK_MAX, (n_pages * PAGE_SIZE + t_new) // 4)
    cache_k = jnp.transpose(cache_k, (0, 1, 3, 4, 2))
    cache_v = jnp.transpose(cache_v, (0, 1, 3, 4, 2))
    cache_ik = jnp.transpose(cache_ik, (0, 1, 3, 2))

    def tok_spec(width, col=0):
        return pl.BlockSpec((1, t_new, width), lambda bi, pt: (bi, 0, col // width))

    def page_spec(p):
        return pl.BlockSpec((None, None, IDX_DIM, PAGE_SIZE), lambda bi, pt, p=p: (pt[bi, p], layer, 0, 0))

    def kv_page_spec(p):
        return pl.BlockSpec((None, None, N_KV_HEADS, HEAD_DIM, PAGE_SIZE),
                            lambda bi, pt, p=p: (pt[bi, p], layer, 0, 0, 0))

    grid_spec = pltpu.PrefetchScalarGridSpec(
        num_scalar_prefetch=1,
        grid=(b,),
        in_specs=[tok_spec(N_HEADS * HEAD_DIM, COL_Q), tok_spec(IDX_HEADS * IDX_DIM, COL_QI),
                  tok_spec(LANES, COL_SMALL), tok_spec(LANES), tok_spec(LANES)]
                 + [kv_page_spec(p) for p in range(n_pages)]
                 + [kv_page_spec(p) for p in range(n_pages)]
                 + [page_spec(p) for p in range(n_pages)],
        out_specs=tok_spec(N_HEADS * HEAD_DIM),
        scratch_shapes=[pltpu.VMEM((LANES, s_pad), jnp.float32),
                        pltpu.VMEM((LANES, s_pad), jnp.float32),
                        pltpu.VMEM((IDX_DIM, s_pad), jnp.float32),
                        pltpu.VMEM((LANES, LANES), jnp.float32),
                        pltpu.VMEM((IDX_HEADS * DEC_ROWS, IDX_DIM), jnp.float32),
                        pltpu.VMEM((DEC_ROWS, LANES), jnp.float32),
                        pltpu.VMEM((N_HEADS * DEC_ROWS, LANES), jnp.float32),
                        pltpu.VMEM((DEC_ROWS, s_pad), jnp.int32),
                        pltpu.VMEM((N_HEADS * DEC_ROWS, s_pad), jnp.float32)])
    return pl.pallas_call(
        functools.partial(_dsa_decode_kernel, n_pages=n_pages, k_sel=k_sel),
        grid_spec=grid_spec,
        out_shape=jax.ShapeDtypeStruct((b, t_new, N_HEADS * HEAD_DIM), jnp.float32),
        compiler_params=pltpu.CompilerParams(dimension_semantics=("arbitrary",),
                                             vmem_limit_bytes=VMEM_LIMIT_BYTES),
        name="dsa_decode",
    )(page_table, proj, proj, proj, k_new, v_new,
      *([cache_k] * n_pages), *([cache_v] * n_pages), *([cache_ik] * n_pages))


COL_GATES, COL_XBC, COL_Q, COL_GLU, COL_Z, COL_QI, COL_K, COL_V, COL_SMALL, COL_DT, D_IN_PAD = (
    0, 3072, 4608, 5120, 6144, 7168, 7424, 7552, 7680, 7808, 7936)
ROW_BLOCK = 256


def _ln_rows(x, g, b):
    mu = jnp.mean(x, axis=-1, keepdims=True)
    xc = x - mu
    var = jnp.mean(xc * xc, axis=-1, keepdims=True)
    return xc * lax.rsqrt(var + LN_EPS) * g + b


def _bdot(a, w_ref):
    return jnp.dot(a.astype(jnp.bfloat16), w_ref[...], preferred_element_type=jnp.float32)


def _branch_mix_kernel(cv_ref, o_ref, y_ref, z_ref, gates_ref, x_ref, lncg_ref, lncb_ref, ng_ref,
                       wc_ref, wa_ref, ws_ref, wm_ref, lng_ref, lnb_ref, out_ref):
    ca = _ln_rows(cv_ref[...], lncg_ref[...], lncb_ref[...])
    y_a = _bdot(ca * jax.nn.sigmoid(ca), wc_ref)
    y_b = _bdot(o_ref[...], wa_ref)
    z = z_ref[...]
    t = y_ref[...] * (z * jax.nn.sigmoid(z))
    t = t * lax.rsqrt(jnp.mean(t * t, axis=-1, keepdims=True) + LN_EPS) * ng_ref[...]
    y_c = _bdot(t, ws_ref)
    d = D_MODEL
    mix = (jax.nn.sigmoid(gates_ref[:, 0:d]) * y_a + jax.nn.sigmoid(gates_ref[:, d:2 * d]) * y_b
           + jax.nn.sigmoid(gates_ref[:, 2 * d:3 * d]) * y_c)
    out_ref[...] = _ln_rows(DN_ALPHA * x_ref[...] + _bdot(mix, wm_ref), lng_ref[...], lnb_ref[...])


def branch_mix_ln(cv, o, y, proj, x, lncg, lncb, ng, wc, wa, ws, wm, lng, lnb):
    m = x.shape[0]
    tm = min(ROW_BLOCK, m)
    assert m % tm == 0
    d = D_MODEL

    def rows(width, col_block=0):
        return pl.BlockSpec((tm, width), lambda i, cb=col_block: (i, cb))

    def whole(a):
        return pl.BlockSpec(a.shape, lambda i: (0,) * a.ndim, pipeline_mode=pl.Buffered(1))

    vecs = [a.reshape(1, -1) for a in (lncg, lncb, ng)]
    lnv = [a.reshape(1, -1) for a in (lng, lnb)]
    return pl.pallas_call(
        _branch_mix_kernel,
        grid=(m // tm,),
        in_specs=[rows(CONV_DIM), rows(N_HEADS * HEAD_DIM), rows(SSM_INNER), rows(SSM_INNER, COL_Z // SSM_INNER),
                  rows(3 * d, COL_GATES // (3 * d)), rows(d)]
                 + [whole(a) for a in vecs] + [whole(a) for a in (wc, wa, ws, wm)] + [whole(a) for a in lnv],
        out_specs=rows(d),
        out_shape=jax.ShapeDtypeStruct((m, d), jnp.float32),
        compiler_params=pltpu.CompilerParams(dimension_semantics=("parallel",), vmem_limit_bytes=VMEM_LIMIT_BYTES),
        name="branch_mix_ln",
    )(cv, o, y, proj, proj, x, *vecs, wc, wa, ws, wm, *lnv)


def _mem_attn_kernel(x_ref, mk_ref, mv_ref, wq_ref, wo_ref, lng_ref, lnb_ref, out_ref, o_scr, *, seqs, t_seq):
    bf16 = jnp.bfloat16
    x = x_ref[...]
    q = _bdot(x, wq_ref)
    scale = 1.0 / math.sqrt(MEM_HEAD_DIM)
    for s in range(seqs):
        qs = q[s * t_seq:(s + 1) * t_seq].astype(bf16)
        for h in range(MEM_HEADS):
            cols = slice(h * MEM_HEAD_DIM, (h + 1) * MEM_HEAD_DIM)
            sc = _nt_dot(qs[:, cols], mk_ref[s, :, h, :].astype(bf16)) * scale
            p = jnp.exp(sc - jnp.max(sc, axis=-1, keepdims=True))
            p = p / jnp.sum(p, axis=-1, keepdims=True)
            o_scr[s * t_seq:(s + 1) * t_seq, cols] = jnp.dot(p.astype(bf16), mv_ref[s, :, h, :].astype(bf16),
                                                              preferred_element_type=jnp.float32)
    out_ref[...] = _ln_rows(DN_ALPHA * x + _bdot(o_scr[...], wo_ref), lng_ref[...], lnb_ref[...])


def mem_attn_ln(x, mk, mv, wq, wo, lng, lnb, t_seq, layer=None):
    m, d = x.shape
    hd = MEM_HEADS * MEM_HEAD_DIM
    if t_seq >= ROW_BLOCK:
        seqs, tm = 1, ROW_BLOCK
        assert t_seq % tm == 0
        per_seq = t_seq // tm
        seq_of = lambda i: i // per_seq
    else:
        seqs = max(1, 32 // t_seq)
        tm = seqs * t_seq
        assert m % tm == 0
        seq_of = lambda i: i
    if layer is None:
        mem_spec = pl.BlockSpec((seqs, N_MEM, MEM_HEADS, MEM_HEAD_DIM), lambda i: (seq_of(i), 0, 0, 0))
    else:
        mem_spec = pl.BlockSpec((seqs, None, N_MEM, MEM_HEADS, MEM_HEAD_DIM), lambda i: (seq_of(i), layer, 0, 0, 0))

    def whole(a):
        return pl.BlockSpec(a.shape, lambda i: (0,) * a.ndim, pipeline_mode=pl.Buffered(1))

    lnv = [a.reshape(1, -1) for a in (lng, lnb)]
    return pl.pallas_call(
        functools.partial(_mem_attn_kernel, seqs=seqs, t_seq=min(t_seq, tm)),
        grid=(m // tm,),
        in_specs=[pl.BlockSpec((tm, d), lambda i: (i, 0)), mem_spec, mem_spec, whole(wq), whole(wo)]
                 + [whole(a) for a in lnv],
        out_specs=pl.BlockSpec((tm, d), lambda i: (i, 0)),
        out_shape=jax.ShapeDtypeStruct((m, d), jnp.float32),
        scratch_shapes=[pltpu.VMEM((tm, hd), jnp.float32)],
        compiler_params=pltpu.CompilerParams(dimension_semantics=("parallel",), vmem_limit_bytes=VMEM_LIMIT_BYTES),
        name="mem_attn_ln",
    )(x, mk, mv, wq, wo, *lnv)


MOE_ROW_BLOCK = 1024


def _moe_kernel(x_ref, wr_hi_ref, wr_lo_ref, br_ref, wg_ref, wu_ref, wd_ref, lng_ref, lnb_ref, out_ref,
                xb_scr, comb_scr, acc_scr):
    f32 = jnp.float32
    e = pl.program_id(1)
    tm = x_ref.shape[0]
    lane = lax.broadcasted_iota(jnp.int32, (tm, LANES), 1).astype(f32)

    @pl.when(e == 0)
    def _route():
        x_hi, x_lo = _split_bf16(x_ref[...])
        xb_scr[...] = x_hi
        lg = (jnp.dot(x_hi, wr_hi_ref[...], preferred_element_type=f32)
              + jnp.dot(x_lo, wr_hi_ref[...], preferred_element_type=f32)
              + jnp.dot(x_hi, wr_lo_ref[...], preferred_element_type=f32) + br_ref[...])
        is_g = lane < N_GROUPS
        mg = jnp.max(jnp.where(is_g, lg, -jnp.inf), axis=1, keepdims=True)
        g_prob = 1.0 / jnp.sum(jnp.where(is_g, jnp.exp(lg - mg), 0.0), axis=1, keepdims=True)
        g_idx = jnp.min(jnp.where(is_g & (lg == mg), lane, float(LANES)), axis=1, keepdims=True)
        lo_e = N_GROUPS + EXPERTS_PER_GROUP * g_idx
        is_e = (lane >= lo_e) & (lane < lo_e + EXPERTS_PER_GROUP)
        me = jnp.max(jnp.where(is_e, lg, -jnp.inf), axis=1, keepdims=True)
        ee = jnp.where(is_e, jnp.exp(lg - me), 0.0)
        pe = jnp.where(is_e, ee / jnp.sum(ee, axis=1, keepdims=True), -1.0)
        p1 = jnp.max(pe, axis=1, keepdims=True)
        first = jnp.min(jnp.where(pe == p1, lane, float(LANES)), axis=1, keepdims=True)
        pe2 = jnp.where(lane == first, -1.0, pe)
        p2 = jnp.max(pe2, axis=1, keepdims=True)
        second = jnp.min(jnp.where(pe2 == p2, lane, float(LANES)), axis=1, keepdims=True)
        norm = g_prob / (p1 + p2)
        comb_scr[...] = jnp.where(lane == first, p1 * norm, jnp.where(lane == second, p2 * norm, 0.0))
        acc_scr[...] = jnp.zeros(acc_scr.shape, f32)

    xb = xb_scr[...]
    hg = jnp.dot(xb, wg_ref[0], preferred_element_type=f32)
    hu = jnp.dot(xb, wu_ref[0], preferred_element_type=f32)
    c = jnp.sum(jnp.where(lane == (e + N_GROUPS).astype(f32), comb_scr[...], 0.0), axis=1, keepdims=True)
    hid = hg * jax.nn.sigmoid(hg) * hu * c
    acc_scr[...] += jnp.dot(hid.astype(jnp.bfloat16), wd_ref[0], preferred_element_type=f32)

    @pl.when(e == pl.num_programs(1) - 1)
    def _finish():
        out_ref[...] = _ln_rows(DN_ALPHA * x_ref[...] + acc_scr[...], lng_ref[...], lnb_ref[...])


def moe_ln(x, w_group, b_group, w_router, b_router, wg, wu, wd, lng, lnb):
    m, d = x.shape
    tm = min(MOE_ROW_BLOCK, m)
    assert m % tm == 0 and N_GROUPS + N_EXPERTS <= LANES
    pad = jnp.zeros((d, LANES - N_GROUPS - N_EXPERTS), jnp.float32)
    wr_hi, wr_lo = _split_bf16(jnp.concatenate([w_group, w_router, pad], axis=1))
    br = jnp.concatenate([b_group, b_router, pad[0]]).reshape(1, LANES)
    lnv = [a.reshape(1, -1) for a in (lng, lnb)]

    def whole(a):
        return pl.BlockSpec(a.shape, lambda i, e: (0,) * a.ndim, pipeline_mode=pl.Buffered(1))

    return pl.pallas_call(
        _moe_kernel,
        grid=(m // tm, N_EXPERTS),
        in_specs=[pl.BlockSpec((tm, d), lambda i, e: (i, 0)), whole(wr_hi), whole(wr_lo), whole(br),
                  pl.BlockSpec((1, d, D_EXPERT), lambda i, e: (e, 0, 0)),
                  pl.BlockSpec((1, d, D_EXPERT), lambda i, e: (e, 0, 0)),
                  pl.BlockSpec((1, D_EXPERT, d), lambda i, e: (e, 0, 0))] + [whole(a) for a in lnv],
        out_specs=pl.BlockSpec((tm, d), lambda i, e: (i, 0)),
        out_shape=jax.ShapeDtypeStruct((m, d), jnp.float32),
        scratch_shapes=[pltpu.VMEM((tm, d), jnp.bfloat16), pltpu.VMEM((tm, LANES), jnp.float32),
                        pltpu.VMEM((tm, d), jnp.float32)],
        compiler_params=pltpu.CompilerParams(dimension_semantics=("parallel", "arbitrary"),
                                             vmem_limit_bytes=VMEM_LIMIT_BYTES),
        name="moe_ln",
    )(x, wr_hi, wr_lo, br, wg, wu, wd, *lnv)


SSD_HIST_ROWS = 8


def _ssd_kernel(xbc_ref, dt_ref, hist_ref, cw_ref, cb_ref, dtb_ref, alog_ref, dexp_ref, h0_ref,
                y_ref, hT_ref, xp_scr, ht_scr, xe_scr):
    f32, i32, bf16 = jnp.float32, jnp.int32, jnp.bfloat16
    c = pl.program_id(1)
    L = xbc_ref.shape[1]
    n_hist = SSM_CONV - 1
    base = SSD_HIST_ROWS - n_hist
    gw = SSM_STATE
    heads_per_group = SSM_HEADS // SSM_GROUPS
    pairs_per_group = heads_per_group // 2

    @pl.when(c == 0)
    def _init():
        xp_scr[base:SSD_HIST_ROWS, :] = hist_ref[0]
        ht_scr[...] = h0_ref[0]

    xp_scr[SSD_HIST_ROWS:SSD_HIST_ROWS + L, :] = xbc_ref[0]
    acc = cw_ref[0:1, :] * xp_scr[base:base + L, :]
    for j in range(1, SSM_CONV):
        acc = acc + cw_ref[j:j + 1, :] * xp_scr[base + j:base + j + L, :]
    acc = acc + cb_ref[...]
    xbc = acc * jax.nn.sigmoid(acc)
    xp_scr[base:SSD_HIST_ROWS, :] = xp_scr[base + L:SSD_HIST_ROWS + L, :]
    xs = xbc[:, 0:SSM_INNER]
    bm = xbc[:, SSM_INNER:SSM_INNER + SSM_GROUPS * gw]
    cm = xbc[:, SSM_INNER + SSM_GROUPS * gw:SSM_INNER + 2 * SSM_GROUPS * gw]

    x_dt = dt_ref[0] + dtb_ref[...]
    dtf = jnp.maximum(x_dt, 0.0) + jnp.log1p(jnp.exp(-jnp.abs(x_dt)))
    la = dtf * (-jnp.exp(alog_ref[...]))
    row = lax.broadcasted_iota(i32, (L, L), 0)
    colk = lax.broadcasted_iota(i32, (L, L), 1)
    causal = colk <= row
    tril = jnp.where(causal, 1.0, 0.0).astype(bf16)
    la_hi = la.astype(bf16)
    r1 = la - la_hi.astype(f32)
    la_mid = r1.astype(bf16)
    la_lo = (r1 - la_mid.astype(f32)).astype(bf16)
    cs = (jnp.dot(tril, la_hi, preferred_element_type=f32) + jnp.dot(tril, la_mid, preferred_element_type=f32)
          + jnp.dot(tril, la_lo, preferred_element_type=f32))
    cs_t = cs.T
    ecs = jnp.exp(cs)
    to_end = jnp.exp(cs[L - 1:L, :] - cs)

    lane = lax.broadcasted_iota(i32, (L, LANES), 1)
    half0 = lane < SSM_HEAD_DIM

    def pair_cols(a, h0):
        return jnp.where(half0, a[:, h0:h0 + 1], a[:, h0 + 1:h0 + 2])

    for g in range(SSM_GROUPS):
        b_g = bm[:, g * gw:(g + 1) * gw]
        c_g = cm[:, g * gw:(g + 1) * gw].astype(bf16)
        cb = _nt_dot(c_g, b_g.astype(bf16))
        cols = slice(g * heads_per_group * SSM_HEAD_DIM, (g + 1) * heads_per_group * SSM_HEAD_DIM)
        y_off = jnp.dot(c_g, ht_scr[:, cols].astype(bf16), preferred_element_type=f32)
        for i in range(pairs_per_group):
            h0 = g * heads_per_group + 2 * i
            pc = slice((h0 // 2) * LANES, (h0 // 2 + 1) * LANES)
            xs_p = xs[:, pc]
            xdt_p = xs_p * pair_cols(dtf, h0)
            att = []
            for hh in range(2):
                seg = cs[:, h0 + hh:h0 + hh + 1] - cs_t[h0 + hh:h0 + hh + 1, :]
                att.append((cb * jnp.exp(jnp.where(causal, seg, -jnp.inf))).astype(bf16))
            rhs = jnp.concatenate([jnp.where(half0, xdt_p, 0.0), jnp.where(half0, 0.0, xdt_p)], axis=0).astype(bf16)
            y_diag = jnp.dot(jnp.concatenate(att, axis=1), rhs, preferred_element_type=f32)
            xe_scr[:, pc] = xdt_p * pair_cols(to_end, h0)
            y_ref[0, :, pc] = (y_diag + y_off[:, i * LANES:(i + 1) * LANES] * pair_cols(ecs, h0)
                               + dexp_ref[:, pc] * xs_p)
        bt_hi, bt_lo = _split_bf16(b_g.T)
        xe_hi, xe_lo = _split_bf16(xe_scr[:, cols])
        st = (jnp.dot(bt_hi, xe_hi, preferred_element_type=f32) + jnp.dot(bt_lo, xe_hi, preferred_element_type=f32)
              + jnp.dot(bt_hi, xe_lo, preferred_element_type=f32))
        for i in range(pairs_per_group):
            h0 = g * heads_per_group + 2 * i
            pc = slice((h0 // 2) * LANES, (h0 // 2 + 1) * LANES)
            decay = jnp.where(half0[0:1, :], ecs[L - 1:L, h0:h0 + 1], ecs[L - 1:L, h0 + 1:h0 + 2])
            ht_scr[:, pc] = ht_scr[:, pc] * decay + st[:, i * LANES:(i + 1) * LANES]

    @pl.when(c == pl.num_programs(1) - 1)
    def _done():
        hT_ref[0] = ht_scr[...]


def ssd_prompt(proj, hist, conv_w, conv_b, dt_bias, a_log, d_skip, h0_t):
    b, t, _ = proj.shape
    L = SSM_CHUNK
    assert t % L == 0 and SSM_HEAD_DIM * 2 == LANES and SSM_STATE == LANES and SSM_HEADS <= LANES

    def lane_pad(a):
        return jnp.concatenate([a, jnp.zeros((LANES - a.shape[0],), a.dtype)]).reshape(1, LANES)

    def whole(a):
        return pl.BlockSpec(a.shape, lambda bi, c: (0,) * a.ndim)

    params = [conv_w, conv_b.reshape(1, -1), lane_pad(dt_bias), lane_pad(a_log),
              jnp.repeat(d_skip, SSM_HEAD_DIM).reshape(1, SSM_INNER)]
    return pl.pallas_call(
        _ssd_kernel,
        grid=(b, t // L),
        in_specs=[pl.BlockSpec((1, L, SSM_XBC), lambda bi, c: (bi, c, COL_XBC // SSM_XBC)),
                  pl.BlockSpec((1, L, LANES), lambda bi, c: (bi, c, COL_DT // LANES)),
                  pl.BlockSpec((1, SSM_CONV - 1, SSM_XBC), lambda bi, c: (bi, 0, 0))]
                 + [whole(a) for a in params]
                 + [pl.BlockSpec((1, SSM_STATE, SSM_INNER), lambda bi, c: (bi, 0, 0))],
        out_specs=[pl.BlockSpec((1, L, SSM_INNER), lambda bi, c: (bi, c, 0)),
                   pl.BlockSpec((1, SSM_STATE, SSM_INNER), lambda bi, c: (bi, 0, 0))],
        out_shape=[jax.ShapeDtypeStruct((b, t, SSM_INNER), jnp.float32),
                   jax.ShapeDtypeStruct((b, SSM_STATE, SSM_INNER), jnp.float32)],
        scratch_shapes=[pltpu.VMEM((SSD_HIST_ROWS + L, SSM_XBC), jnp.float32),
                        pltpu.VMEM((SSM_STATE, SSM_INNER), jnp.float32),
                        pltpu.VMEM((L, SSM_INNER), jnp.float32)],
        compiler_params=pltpu.CompilerParams(dimension_semantics=("parallel", "arbitrary"),
                                             vmem_limit_bytes=VMEM_LIMIT_BYTES),
        name="ssd_prompt",
    )(proj, proj, hist, *params, h0_t)


def causal_dwconv(x_pad, w):
    return lax.conv_general_dilated(x_pad, w[:, None, :], window_strides=(1,), padding='VALID',
                                    dimension_numbers=('NWC', 'WIO', 'NWC'),
                                    feature_group_count=x_pad.shape[-1])


def ssd_scan(x, dt, a, bm, cm, h0):
    f32 = jnp.float32
    bsz, l, nh, hp = x.shape
    rep = nh // bm.shape[2]
    chunk = min(SSM_CHUNK, l)
    assert l % chunk == 0
    xdt = x.astype(f32) * dt[..., None]
    la = dt * a
    bh = jnp.repeat(bm.astype(f32), rep, axis=2)
    ch = jnp.repeat(cm.astype(f32), rep, axis=2)
    nc = l // chunk
    ns = bh.shape[-1]
    xdt = xdt.reshape(bsz, nc, chunk, nh, hp)
    la = la.reshape(bsz, nc, chunk, nh)
    bh = bh.reshape(bsz, nc, chunk, nh, ns)
    ch = ch.reshape(bsz, nc, chunk, nh, ns)
    cs = jnp.cumsum(la, axis=2)
    causal = jnp.tril(jnp.ones((chunk, chunk), bool))
    seg = cs[:, :, :, None, :] - cs[:, :, None, :, :]
    decay = jnp.exp(jnp.where(causal[None, None, :, :, None], seg, -jnp.inf))
    att = jnp.einsum('bcqhn,bckhn->bcqkh', ch, bh) * decay
    y_diag = jnp.einsum('bcqkh,bckhp->bcqhp', att, xdt)
    to_end = jnp.exp(cs[:, :, -1:, :] - cs)
    states = jnp.einsum('bckhn,bckh,bckhp->bchpn', bh, to_end, xdt)
    chunk_decay = jnp.exp(cs[:, :, -1, :])

    def step(hc, inp):
        dc, st = inp
        return hc * dc[:, :, None, None] + st, hc

    h_last, h_prev = lax.scan(step, h0.astype(f32), (chunk_decay.swapaxes(0, 1), states.swapaxes(0, 1)))
    h_prev = h_prev.swapaxes(0, 1)
    y_off = jnp.einsum('bcqhn,bchpn,bcqh->bcqhp', ch, h_prev, jnp.exp(cs))
    y = (y_diag + y_off).reshape(bsz, nc * chunk, nh, hp)
    return y, h_last


def token_mixer(x, p, conv_buf, ssm_buf, ssm_h0, past):
    (w_in, conv_dw, ln_conv_g, ln_conv_b, w_conv_out, w_attn_out, ssm_conv_w, ssm_conv_b,
     ssm_dt_bias, ssm_a_log, ssm_d, ssm_norm_g, w_ssm_out, w_mix_out, ln_mix_g, ln_mix_b) = p
    b, t, _ = x.shape
    proj = mm3(x, w_in)
    glu_in = proj[..., COL_GLU:COL_GLU + 2 * CONV_DIM]
    k = proj[..., COL_K:COL_K + LANES]
    v = proj[..., COL_V:COL_V + LANES]
    small = proj[..., COL_SMALL:COL_SMALL + LANES]
    ki = small[..., 0:IDX_DIM]
    dt = small[..., IDX_DIM + IDX_HEADS:IDX_DIM + IDX_HEADS + SSM_HEADS]
    xbc = proj[..., COL_XBC:COL_XBC + SSM_XBC]
    glu = glu_in[..., :CONV_DIM] * jax.nn.sigmoid(glu_in[..., CONV_DIM:])
    glu_pad = jnp.concatenate([conv_buf, glu], axis=1)
    cv = causal_dwconv(glu_pad, conv_dw)
    if past is None:
        o = dsa_prompt(proj, k, v, ki)
    else:
        cache_k, cache_v, cache_ik, page_table, layer = past
        o = dsa_decode(proj, k, v, cache_k, cache_v, cache_ik, page_table, layer)
    if past is None:
        h0_t = ssm_h0.reshape(b, SSM_INNER, SSM_STATE).swapaxes(1, 2)
        y, h_t = ssd_prompt(proj, ssm_buf, ssm_conv_w, ssm_conv_b, ssm_dt_bias, ssm_a_log, ssm_d, h0_t)
        h_last = h_t.swapaxes(1, 2).reshape(b, SSM_HEADS, SSM_HEAD_DIM, SSM_STATE)
        xbc_pad = xbc[:, -(SSM_CONV - 1):]
    else:
        xbc_pad = jnp.concatenate([ssm_buf, xbc], axis=1)
        xbc_c = jax.nn.silu(causal_dwconv(xbc_pad, ssm_conv_w) + ssm_conv_b)
        xs, bm, cm = jnp.split(xbc_c, [SSM_INNER, SSM_INNER + SSM_GROUPS * SSM_STATE], axis=-1)
        xs = xs.reshape(b, t, SSM_HEADS, SSM_HEAD_DIM)
        dtf = jax.nn.softplus(dt + ssm_dt_bias)
        a = -jnp.exp(ssm_a_log)
        y, h_last = ssd_scan(xs, dtf, a, bm.reshape(b, t, SSM_GROUPS, SSM_STATE),
                             cm.reshape(b, t, SSM_GROUPS, SSM_STATE), ssm_h0)
        y = (y + ssm_d[:, None] * xs).reshape(b, t, SSM_INNER)
    m = b * t
    x_new = branch_mix_ln(cv.reshape(m, CONV_DIM), o.reshape(m, -1), y.reshape(m, SSM_INNER),
                          proj.reshape(m, D_IN_PAD), x.reshape(m, D_MODEL), ln_conv_g, ln_conv_b, ssm_norm_g,
                          w_conv_out, w_attn_out, w_ssm_out, w_mix_out, ln_mix_g, ln_mix_b).reshape(b, t, D_MODEL)
    return (x_new, k.reshape(b, t, N_KV_HEADS, HEAD_DIM), v.reshape(b, t, N_KV_HEADS, HEAD_DIM), ki,
            glu_pad[:, -(CONV_WIDTH - 1):], xbc_pad[:, -(SSM_CONV - 1):], h_last)


def _pad_w_in(w):
    sp = (0,) + SPLIT_POINTS + (w.shape[1],)
    seg = [w[:, sp[i]:sp[i + 1]] for i in range(len(SPLIT_SIZES))]
    glu, q, k, v, qi, ki, wi, z, xbc, dt, gates = seg
    pad = jnp.zeros((w.shape[0], LANES - IDX_DIM - IDX_HEADS - SSM_HEADS), w.dtype)
    n_rep = N_HEADS // N_KV_HEADS
    q = q.reshape(-1, N_KV_HEADS, n_rep, HEAD_DIM).swapaxes(1, 2).reshape(q.shape)
    dt_pad = jnp.zeros((w.shape[0], LANES - SSM_HEADS), w.dtype)
    out = jnp.concatenate([gates, xbc, q, glu, z, qi, k, v, ki, wi, dt, pad, dt, dt_pad], axis=1)
    assert out.shape[1] == D_IN_PAD
    return out


def _perm_w_attn_out(w):
    n_rep = N_HEADS // N_KV_HEADS
    return w.reshape(N_KV_HEADS, n_rep, HEAD_DIM, -1).swapaxes(0, 1).reshape(w.shape)


def kernel(x_prompt, x_sample, mem_prompt, cache_k, cache_v, cache_ik, cache_mem_k, cache_mem_v, state_conv, state_ssm_conv, state_ssm, page_table, w_in, conv_dw, ln_conv_g, ln_conv_b, w_conv_out, w_attn_out, ssm_conv_w, ssm_conv_b, ssm_dt_bias, ssm_a_log, ssm_d, ssm_norm_g, w_ssm_out, w_mix_out, ln_mix_g, ln_mix_b, w_mq, w_mk, w_mv, w_mo, ln_mem_g, ln_mem_b, w_group, b_group, w_router, b_router, w_e_gate, w_e_up, w_e_down, ln_ffn_g, ln_ffn_b):
    bf16 = jnp.bfloat16
    bp, tp, _ = x_prompt.shape
    bs, ts, _ = x_sample.shape
    xp, xs = x_prompt, x_sample
    outs_p = [[] for _ in range(8)]
    outs_s = [[] for _ in range(6)]
    for l in range(DEPTH):
        mix_p = (_pad_w_in(w_in[l]).astype(bf16), conv_dw[l], ln_conv_g[l], ln_conv_b[l],
                 w_conv_out[l].astype(bf16), _perm_w_attn_out(w_attn_out[l]).astype(bf16),
                 ssm_conv_w[l], ssm_conv_b[l], ssm_dt_bias[l], ssm_a_log[l], ssm_d[l], ssm_norm_g[l],
                 w_ssm_out[l].astype(bf16), w_mix_out[l].astype(bf16), ln_mix_g[l], ln_mix_b[l])
        moe_p = (w_group[l], b_group[l], w_router[l], b_router[l],
                 w_e_gate[l].astype(bf16), w_e_up[l].astype(bf16), w_e_down[l].astype(bf16), ln_ffn_g[l], ln_ffn_b[l])
        mem_p = (w_mq[l].astype(bf16), w_mo[l].astype(bf16), ln_mem_g[l], ln_mem_b[l])
        hd_mem = MEM_HEADS * MEM_HEAD_DIM
        xp, kp, vp, kip, cbp, sbp, hp = token_mixer(
            xp, mix_p,
            jnp.zeros((bp, CONV_WIDTH - 1, CONV_DIM), xp.dtype),
            jnp.zeros((bp, SSM_CONV - 1, SSM_XBC), xp.dtype),
            jnp.zeros((bp, SSM_HEADS, SSM_HEAD_DIM, SSM_STATE), jnp.float32),
            None)
        mem_kv = mm(mem_prompt.reshape(bp * N_MEM, D_MODEL),
                    jnp.concatenate([w_mk[l], w_mv[l]], axis=1).astype(bf16)).reshape(bp, N_MEM, 2 * hd_mem)
        mkp = mem_kv[..., :hd_mem].reshape(bp, N_MEM, MEM_HEADS, MEM_HEAD_DIM)
        mvp = mem_kv[..., hd_mem:].reshape(bp, N_MEM, MEM_HEADS, MEM_HEAD_DIM)
        xp = mem_attn_ln(xp.reshape(bp * tp, D_MODEL), mkp, mvp, *mem_p, t_seq=tp)
        xp = moe_ln(xp, *moe_p).reshape(bp, tp, D_MODEL)
        xs, ks_new, vs_new, kis, cbs, sbs, hs = token_mixer(
            xs, mix_p, state_conv[:, l], state_ssm_conv[:, l], state_ssm[:, l],
            (cache_k, cache_v, cache_ik, page_table, l))
        xs = mem_attn_ln(xs.reshape(bs * ts, D_MODEL), cache_mem_k, cache_mem_v, *mem_p, t_seq=ts, layer=l)
        xs = moe_ln(xs, *moe_p).reshape(bs, ts, D_MODEL)
        for lst, arr in zip(outs_p, (kp, vp, kip, mkp, mvp, cbp, sbp, hp)):
            lst.append(arr)
        for lst, arr in zip(outs_s, (ks_new, vs_new, kis, cbs, sbs, hs)):
            lst.append(arr)
    p_k, p_v, p_ik, p_mem_k, p_mem_v, p_conv, p_ssm_conv, p_ssm = [jnp.stack(a, axis=1) for a in outs_p]
    s_k, s_v, s_ik, s_conv, s_ssm_conv, s_ssm = [jnp.stack(a, axis=1) for a in outs_s]
    return (xp, xs, p_k, p_v, p_ik, p_mem_k, p_mem_v, p_conv, p_ssm_conv, p_ssm,
            s_k, s_v, s_ik, s_conv, s_ssm_conv, s_ssm)
```

```python
import functools
import math

import jax
import jax.numpy as jnp
from jax import lax
from jax.experimental import pallas as pl
from jax.experimental.pallas import tpu as pltpu

D_MODEL = 1024
DEPTH = 2
PAGE_SIZE = 128
CONV_DIM = 512
CONV_WIDTH = 31
N_HEADS = 8
N_KV_HEADS = 2
HEAD_DIM = 64
IDX_HEADS = 4
IDX_DIM = 64
TOPK_MAX = 256
Q_BLOCK = 128
SSM_HEADS = 16
SSM_HEAD_DIM = 64
SSM_INNER = SSM_HEADS * SSM_HEAD_DIM
SSM_GROUPS = 2
SSM_STATE = 128
SSM_CONV = 4
SSM_CHUNK = 128
SSM_XBC = SSM_INNER + 2 * SSM_GROUPS * SSM_STATE
N_MEM = 256
MEM_HEADS = 4
MEM_HEAD_DIM = 128
N_GROUPS = 4
EXPERTS_PER_GROUP = 4
N_EXPERTS = N_GROUPS * EXPERTS_PER_GROUP
TOPK_IN_GROUP = 2
D_EXPERT = 512
DN_ALPHA = (2 * DEPTH) ** 0.25
LN_EPS = 1e-5

SPLIT_SIZES = (2 * CONV_DIM, N_HEADS * HEAD_DIM, N_KV_HEADS * HEAD_DIM, N_KV_HEADS * HEAD_DIM,
               IDX_HEADS * IDX_DIM, IDX_DIM, IDX_HEADS, SSM_INNER, SSM_XBC, SSM_HEADS, 3 * D_MODEL)
SPLIT_POINTS = tuple(sum(SPLIT_SIZES[:i + 1]) for i in range(len(SPLIT_SIZES) - 1))

VMEM_LIMIT_BYTES = 56 * 1024 * 1024
MM_COL_CHUNK = 512


def _mm_kernel(x_ref, w_ref, o_ref):
    xb = x_ref[...].astype(jnp.bfloat16)
    n = o_ref.shape[1]
    for c0 in range(0, n, MM_COL_CHUNK):
        c1 = min(n, c0 + MM_COL_CHUNK)
        o_ref[:, c0:c1] = jnp.dot(xb, w_ref[:, c0:c1], preferred_element_type=jnp.float32)


def mm(x, w, tm=256):
    m, k = x.shape
    n = w.shape[1]
    tm = min(tm, m)
    assert m % tm == 0 and n % 128 == 0 and k % 128 == 0
    return pl.pallas_call(
        _mm_kernel,
        grid=(m // tm,),
        in_specs=[pl.BlockSpec((tm, k), lambda i: (i, 0)),
                  pl.BlockSpec((k, n), lambda i: (0, 0), pipeline_mode=pl.Buffered(1))],
        out_specs=pl.BlockSpec((tm, n), lambda i: (i, 0)),
        out_shape=jax.ShapeDtypeStruct((m, n), jnp.float32),
        compiler_params=pltpu.CompilerParams(dimension_semantics=("parallel",),
                                             vmem_limit_bytes=VMEM_LIMIT_BYTES),
    )(x, w)


def _mm_nt_kernel(x_ref, wt_ref, o_ref):
    xb = x_ref[...].astype(jnp.bfloat16)
    n = o_ref.shape[1]
    for c0 in range(0, n, MM_COL_CHUNK):
        c1 = min(n, c0 + MM_COL_CHUNK)
        o_ref[:, c0:c1] = lax.dot_general(xb, wt_ref[c0:c1, :], (((1,), (1,)), ((), ())),
                                          preferred_element_type=jnp.float32)


def mm_nt(x, wt, tm=256):
    m, k = x.shape
    n = wt.shape[0]
    tm = min(tm, m)
    assert m % tm == 0 and n % 128 == 0 and k % 128 == 0
    return pl.pallas_call(
        _mm_nt_kernel,
        grid=(m // tm,),
        in_specs=[pl.BlockSpec((tm, k), lambda i: (i, 0)),
                  pl.BlockSpec((n, k), lambda i: (0, 0), pipeline_mode=pl.Buffered(1))],
        out_specs=pl.BlockSpec((tm, n), lambda i: (i, 0)),
        out_shape=jax.ShapeDtypeStruct((m, n), jnp.float32),
        compiler_params=pltpu.CompilerParams(dimension_semantics=("parallel",),
                                             vmem_limit_bytes=VMEM_LIMIT_BYTES),
        name="in_proj",
    )(x, wt)


INT32_MIN = -2 ** 31
MASK_BIAS = -1e30
DSA_KEY_CHUNK = 512
LANES = 128


def _nt_dot(a, b):
    return lax.dot_general(a, b, (((1,), (1,)), ((), ())), preferred_element_type=jnp.float32)


def _dsa_prompt_kernel(q_ref, qi_ref, sm_ref, k_ref, v_ref, ki_ref, o_ref,
                       key_scr, rank_scr, s_scr, p_scr, qpad_scr, qis_scr, m_scr, l_scr, acc_scr, *, k_sel):
    f32, i32, bf16 = jnp.float32, jnp.int32, jnp.bfloat16
    qb = q_ref.shape[1]
    kc_w = DSA_KEY_CHUNK
    sub = kc_w // LANES
    i = pl.program_id(1)
    n_chunks = (i * qb + qb + kc_w - 1) // kc_w
    lane = lax.broadcasted_iota(i32, (qb, LANES), 1)
    half = [lane < HEAD_DIM, lane >= HEAD_DIM]

    for pair in range(IDX_HEADS // 2):
        src = qi_ref[0, :, pair * LANES:(pair + 1) * LANES]
        hi = src.astype(bf16).astype(f32)
        lo_swapped = pltpu.roll(src - hi, IDX_DIM, axis=1)
        for hh in range(2):
            h = 2 * pair + hh
            qis_scr[h * qb:(h + 1) * qb, 0:LANES] = jnp.where(half[hh], hi, lo_swapped).astype(bf16)
            qis_scr[h * qb:(h + 1) * qb, LANES:2 * LANES] = jnp.where(half[hh], hi, 0.0).astype(bf16)
    w_idx = [sm_ref[0, :, IDX_DIM + h:IDX_DIM + h + 1] for h in range(IDX_HEADS)]
    row_pos = i * qb + lax.broadcasted_iota(i32, (qb, kc_w), 0)
    col_iota = lax.broadcasted_iota(i32, (qb, kc_w), 1)

    def score_body(c, carry):
        off = pl.multiple_of(c * kc_w, kc_w)
        d = _nt_dot(qis_scr[...], ki_ref[0, pl.ds(off, kc_w), :])
        idx = w_idx[0] * jnp.maximum(d[0:qb], 0.0)
        for h in range(1, IDX_HEADS):
            idx = idx + w_idx[h] * jnp.maximum(d[h * qb:(h + 1) * qb], 0.0)
        idx = jnp.where(idx == 0.0, 0.0, idx)
        bits = lax.bitcast_convert_type(idx, i32)
        key = bits ^ ((bits >> 31) & 0x7FFFFFFF)
        key_scr[c] = jnp.where(off + col_iota <= row_pos, key, INT32_MIN)
        return carry

    lax.fori_loop(0, n_chunks, score_body, 0)

    def count(pred):
        def body(c, acc):
            for g in range(sub):
                acc = acc + jnp.where(pred(key_scr[c, :, g * LANES:(g + 1) * LANES]), 1.0, 0.0)
            return acc
        acc = lax.fori_loop(0, n_chunks, body, jnp.zeros((qb, LANES), f32))
        return jnp.sum(acc, axis=1, keepdims=True)

    def search_body(it, lo):
        cand = lo + lax.shift_left(jnp.int32(1), 31 - it)
        cand_b = jnp.broadcast_to(cand, (qb, LANES))
        cnt = count(lambda kk: kk >= cand_b)
        return jnp.where(cnt >= float(k_sel), cand, lo)

    thr = lax.fori_loop(0, 32, search_body, jnp.full((qb, 1), INT32_MIN, i32))
    thr_b = jnp.broadcast_to(thr, (qb, LANES))
    need_b = jnp.broadcast_to(float(k_sel) - count(lambda kk: kk > thr_b), (qb, LANES))
    thr_w = jnp.broadcast_to(thr, (qb, kc_w))
    need_w = jnp.tile(need_b, (1, sub))

    r_i = lax.broadcasted_iota(i32, (kc_w, kc_w + LANES), 0)
    c_i = lax.broadcasted_iota(i32, (kc_w, kc_w + LANES), 1)
    rank_scr[...] = jnp.where((r_i <= c_i) | (c_i >= kc_w), 1.0, 0.0).astype(bf16)
    zero_bits = jnp.zeros((qb, kc_w), i32)
    mask_bits = lax.bitcast_convert_type(jnp.full((qb, kc_w), MASK_BIAS, f32), i32)

    def select_body(c, seen):
        kk = key_scr[c]
        eq = kk == thr_w
        pr = jnp.dot(jnp.where(eq, 1.0, 0.0).astype(bf16), rank_scr[...], preferred_element_type=f32)
        rank = jnp.tile(seen, (1, sub)) + pr[:, 0:kc_w]
        sel = ((kk > thr_w) | (eq & (rank <= need_w))) & (kk > INT32_MIN)
        key_scr[c] = jnp.where(sel, zero_bits, mask_bits)
        return seen + pr[:, kc_w:kc_w + LANES]

    lax.fori_loop(0, n_chunks, select_body, jnp.zeros((qb, LANES), f32))

    n_blk = N_HEADS
    scale = 1.0 / math.sqrt(HEAD_DIM)
    for g in range(N_HEADS // N_KV_HEADS):
        src = q_ref[0, :, g * LANES:(g + 1) * LANES] * scale
        for j in range(N_KV_HEADS):
            r = g * N_KV_HEADS + j
            qpad_scr[r * qb:(r + 1) * qb, :] = jnp.where(half[j], src, 0.0).astype(bf16)
    m_scr[...] = jnp.full(m_scr.shape, -jnp.inf, f32)
    l_scr[...] = jnp.zeros(l_scr.shape, f32)
    acc_scr[...] = jnp.zeros(acc_scr.shape, f32)

    def attend_body(c, carry):
        off = pl.multiple_of(c * kc_w, kc_w)
        k_c = k_ref[0, pl.ds(off, kc_w), :]
        v_c = v_ref[0, pl.ds(off, kc_w), :]
        bias = lax.bitcast_convert_type(key_scr[c], f32)
        s_scr[...] = _nt_dot(qpad_scr[...], k_c)
        for r in range(n_blk):
            rows = slice(r * qb, (r + 1) * qb)
            s = s_scr[rows, :] + bias
            m_old = m_scr[rows, :]
            m_new = jnp.maximum(m_old, jnp.broadcast_to(jnp.max(s, axis=1, keepdims=True), (qb, LANES)))
            alpha = jnp.exp(m_old - m_new)
            p = jnp.exp(s - jnp.tile(m_new, (1, sub)))
            l_scr[rows, :] = alpha * l_scr[rows, :] + jnp.broadcast_to(jnp.sum(p, axis=1, keepdims=True), (qb, LANES))
            acc_scr[rows, :] = alpha * acc_scr[rows, :]
            p_scr[rows, :] = p.astype(bf16)
            m_scr[rows, :] = m_new
        acc_scr[...] += jnp.dot(p_scr[...], v_c, preferred_element_type=f32)
        return carry

    lax.fori_loop(0, n_chunks, attend_body, 0)

    for g in range(N_HEADS // N_KV_HEADS):
        outs = []
        for j in range(N_KV_HEADS):
            rows = slice((g * N_KV_HEADS + j) * qb, (g * N_KV_HEADS + j + 1) * qb)
            outs.append(acc_scr[rows, :] / l_scr[rows, :])
        o_ref[0, :, g * LANES:(g + 1) * LANES] = jnp.where(half[0], outs[0], outs[1])


def _split_bf16(x):
    hi = x.astype(jnp.bfloat16)
    return hi, (x - hi.astype(jnp.float32)).astype(jnp.bfloat16)


def dsa_prompt(proj, k, v, ki):
    b, t, _ = proj.shape
    qb = Q_BLOCK
    w_q, w_qi = N_HEADS * HEAD_DIM, IDX_HEADS * IDX_DIM
    assert t % DSA_KEY_CHUNK == 0 and t % qb == 0
    assert IDX_DIM == HEAD_DIM == LANES // 2 and IDX_HEADS % 2 == 0
    k_sel = min(TOPK_MAX, t // 4)
    n_rows = N_HEADS * qb
    ki_hi, ki_lo = _split_bf16(ki)
    ki4 = jnp.concatenate([ki_hi, ki_hi, ki_lo, ki_lo], axis=-1)
    k, v = k.astype(jnp.bfloat16), v.astype(jnp.bfloat16)
    seq_spec = pl.BlockSpec((1, t, LANES), lambda bi, i: (bi, 0, 0))
    return pl.pallas_call(
        functools.partial(_dsa_prompt_kernel, k_sel=k_sel),
        grid=(b, t // qb),
        in_specs=[pl.BlockSpec((1, qb, w_q), lambda bi, i: (bi, i, COL_Q // w_q)),
                  pl.BlockSpec((1, qb, w_qi), lambda bi, i: (bi, i, COL_QI // w_qi)),
                  pl.BlockSpec((1, qb, LANES), lambda bi, i: (bi, i, COL_SMALL // LANES)),
                  seq_spec, seq_spec, pl.BlockSpec((1, t, 2 * LANES), lambda bi, i: (bi, 0, 0))],
        out_specs=pl.BlockSpec((1, qb, N_HEADS * HEAD_DIM), lambda bi, i: (bi, i, 0)),
        out_shape=jax.ShapeDtypeStruct((b, t, N_HEADS * HEAD_DIM), jnp.float32),
        scratch_shapes=[pltpu.VMEM((t // DSA_KEY_CHUNK, qb, DSA_KEY_CHUNK), jnp.int32),
                        pltpu.VMEM((DSA_KEY_CHUNK, DSA_KEY_CHUNK + LANES), jnp.bfloat16),
                        pltpu.VMEM((n_rows, DSA_KEY_CHUNK), jnp.float32),
                        pltpu.VMEM((n_rows, DSA_KEY_CHUNK), jnp.bfloat16),
                        pltpu.VMEM((n_rows, LANES), jnp.bfloat16),
                        pltpu.VMEM((IDX_HEADS * qb, 2 * LANES), jnp.bfloat16),
                        pltpu.VMEM((n_rows, LANES), jnp.float32),
                        pltpu.VMEM((n_rows, LANES), jnp.float32),
                        pltpu.VMEM((n_rows, LANES), jnp.float32)],
        compiler_params=pltpu.CompilerParams(dimension_semantics=("parallel", "arbitrary"),
                                             vmem_limit_bytes=VMEM_LIMIT_BYTES),
        name="dsa_prompt",
    )(proj, proj, proj, k, v, ki4)


DEC_ROWS = 8


def _dsa_decode_kernel(pt_ref, q_ref, qi_ref, sm_ref, kn_ref, vn_ref, *rest, n_pages, k_sel):
    f32, i32, bf16 = jnp.float32, jnp.int32, jnp.bfloat16
    k_pages = rest[0:n_pages]
    v_pages = rest[n_pages:2 * n_pages]
    ik_pages = rest[2 * n_pages:3 * n_pages]
    o_ref = rest[3 * n_pages]
    kall, vall, ikall, new_scr, qis, wpad, qpad, key_scr, p_scr = rest[3 * n_pages + 1:]
    t_new = q_ref.shape[1]
    past = n_pages * PAGE_SIZE
    s_pad = past + PAGE_SIZE
    n_chunks = s_pad // LANES
    rows = DEC_ROWS

    for p in range(n_pages):
        sl = slice(p * PAGE_SIZE, (p + 1) * PAGE_SIZE)
        for j in range(N_KV_HEADS):
            kall[j * HEAD_DIM:(j + 1) * HEAD_DIM, sl] = k_pages[p][j]
            vall[j * HEAD_DIM:(j + 1) * HEAD_DIM, sl] = v_pages[p][j]
        ikall[:, sl] = ik_pages[p][...]
    tail = slice(past, s_pad)
    for src, dst, n_feat in ((kn_ref[0], kall, LANES), (vn_ref[0], vall, LANES),
                             (sm_ref[0], ikall, IDX_DIM)):
        new_scr[...] = jnp.zeros(new_scr.shape, f32)
        new_scr[0:t_new, :] = src
        dst[:, tail] = new_scr[...].T[0:n_feat, :]

    qis[...] = jnp.zeros(qis.shape, f32)
    wpad[...] = jnp.zeros(wpad.shape, f32)
    for h in range(IDX_HEADS):
        qis[h * rows:h * rows + t_new, :] = qi_ref[0, :, h * IDX_DIM:(h + 1) * IDX_DIM]
    wpad[0:t_new, :] = sm_ref[0]
    q_hi, q_lo = _split_bf16(qis[...])
    k_hi, k_lo = _split_bf16(ikall[...])

    def mm_f32(a, b):
        return jnp.dot(a, b, preferred_element_type=f32)

    d = mm_f32(q_hi, k_hi) + mm_f32(q_lo, k_hi) + mm_f32(q_hi, k_lo)
    idx = wpad[:, IDX_DIM:IDX_DIM + 1] * jnp.maximum(d[0:rows], 0.0)
    for h in range(1, IDX_HEADS):
        idx = idx + wpad[:, IDX_DIM + h:IDX_DIM + h + 1] * jnp.maximum(d[h * rows:(h + 1) * rows], 0.0)
    idx = jnp.where(idx == 0.0, 0.0, idx)
    bits = lax.bitcast_convert_type(idx, i32)
    key = bits ^ ((bits >> 31) & 0x7FFFFFFF)
    col = lax.broadcasted_iota(i32, (rows, s_pad), 1)
    q_pos = past + lax.broadcasted_iota(i32, (rows, s_pad), 0)
    key_scr[...] = jnp.where((col <= q_pos) & (col < past + t_new), key, INT32_MIN)

    def search_body(it, lo):
        cand = lo + lax.shift_left(jnp.int32(1), 31 - it)
        cnt = jnp.sum(jnp.where(key_scr[...] >= cand, 1.0, 0.0), axis=1, keepdims=True)
        return jnp.where(cnt >= float(k_sel), cand, lo)

    thr = lax.fori_loop(0, 32, search_body, jnp.full((rows, 1), INT32_MIN, i32))
    need = float(k_sel) - jnp.sum(jnp.where(key_scr[...] > thr, 1.0, 0.0), axis=1, keepdims=True)
    tri = jnp.where(lax.broadcasted_iota(i32, (LANES, LANES), 0) <= lax.broadcasted_iota(i32, (LANES, LANES), 1),
                    1.0, 0.0).astype(bf16)
    seen = jnp.zeros((rows, 1), f32)
    for c in range(n_chunks):
        kk = key_scr[:, c * LANES:(c + 1) * LANES]
        eq = kk == thr
        rank = seen + jnp.dot(jnp.where(eq, 1.0, 0.0).astype(bf16), tri, preferred_element_type=f32)
        sel = ((kk > thr) | (eq & (rank <= need))) & (kk > INT32_MIN)
        key_scr[:, c * LANES:(c + 1) * LANES] = lax.bitcast_convert_type(jnp.where(sel, 0.0, MASK_BIAS), i32)
        seen = rank[:, LANES - 1:LANES]
    bias = lax.bitcast_convert_type(key_scr[...], f32)

    lane = lax.broadcasted_iota(i32, (t_new, LANES), 1)
    half = [lane < HEAD_DIM, lane >= HEAD_DIM]
    scale = 1.0 / math.sqrt(HEAD_DIM)
    qpad[...] = jnp.zeros(qpad.shape, f32)
    for g in range(N_HEADS // N_KV_HEADS):
        src = q_ref[0, :, g * LANES:(g + 1) * LANES] * scale
        for j in range(N_KV_HEADS):
            r = g * N_KV_HEADS + j
            qpad[r * rows:r * rows + t_new, :] = jnp.where(half[j], src, 0.0)
    s_all = mm_f32(qpad[...].astype(bf16), kall[...].astype(bf16))
    for r in range(N_HEADS):
        s = s_all[r * rows:(r + 1) * rows] + bias
        p = jnp.exp(s - jnp.max(s, axis=1, keepdims=True))
        p_scr[r * rows:(r + 1) * rows, :] = p / jnp.sum(p, axis=1, keepdims=True)
    o_all = _nt_dot(p_scr[...].astype(bf16), vall[...].astype(bf16))
    for g in range(N_HEADS // N_KV_HEADS):
        r0, r1 = g * N_KV_HEADS * rows, (g * N_KV_HEADS + 1) * rows
        o_ref[0, :, g * LANES:(g + 1) * LANES] = jnp.where(half[0], o_all[r0:r0 + t_new], o_all[r1:r1 + t_new])


def dsa_decode(proj, k_new, v_new, cache_k, cache_v, cache_ik, page_table, layer):
    b, t_new, _ = proj.shape
    n_pages = page_table.shape[1]
    assert t_new <= DEC_ROWS and PAGE_SIZE == LANES
    s_pad = (n_pages + 1) * PAGE_SIZE
    k_sel = min(TOPK_MAX, (n_pages * PAGE_SIZE + t_new) // 4)
    cache_k = jnp.transpose(cache_k, (0, 1, 3, 4, 2))
    cache_v = jnp.transpose(cache_v, (0, 1, 3, 4, 2))
    cache_ik = jnp.transpose(cache_ik, (0, 1, 3, 2))

    def tok_spec(width, col=0):
        return pl.BlockSpec((1, t_new, width), lambda bi, pt: (bi, 0, col // width))

    def page_spec(p):
        return pl.BlockSpec((None, None, IDX_DIM, PAGE_SIZE), lambda bi, pt, p=p: (pt[bi, p], layer, 0, 0))

    def kv_page_spec(p):
        return pl.BlockSpec((None, None, N_KV_HEADS, HEAD_DIM, PAGE_SIZE),
                            lambda bi, pt, p=p: (pt[bi, p], layer, 0, 0, 0))

    grid_spec = pltpu.PrefetchScalarGridSpec(
        num_scalar_prefetch=1,
        grid=(b,),
        in_specs=[tok_spec(N_HEADS * HEAD_DIM, COL_Q), tok_spec(IDX_HEADS * IDX_DIM, COL_QI),
                  tok_spec(LANES, COL_SMALL), tok_spec(LANES), tok_spec(LANES)]
                 + [kv_page_spec(p) for p in range(n_pages)]
                 + [kv_page_spec(p) for p in range(n_pages)]
                 + [page_spec(p) for p in range(n_pages)],
        out_specs=tok_spec(N_HEADS * HEAD_DIM),
        scratch_shapes=[pltpu.VMEM((LANES, s_pad), jnp.float32),
                        pltpu.VMEM((LANES, s_pad), jnp.float32),
                        pltpu.VMEM((IDX_DIM, s_pad), jnp.float32),
                        pltpu.VMEM((LANES, LANES), jnp.float32),
                        pltpu.VMEM((IDX_HEADS * DEC_ROWS, IDX_DIM), jnp.float32),
                        pltpu.VMEM((DEC_ROWS, LANES), jnp.float32),
                        pltpu.VMEM((N_HEADS * DEC_ROWS, LANES), jnp.float32),
                        pltpu.VMEM((DEC_ROWS, s_pad), jnp.int32),
                        pltpu.VMEM((N_HEADS * DEC_ROWS, s_pad), jnp.float32)])
    return pl.pallas_call(
        functools.partial(_dsa_decode_kernel, n_pages=n_pages, k_sel=k_sel),
        grid_spec=grid_spec,
        out_shape=jax.ShapeDtypeStruct((b, t_new, N_HEADS * HEAD_DIM), jnp.float32),
        compiler_params=pltpu.CompilerParams(dimension_semantics=("arbitrary",),
                                             vmem_limit_bytes=VMEM_LIMIT_BYTES),
        name="dsa_decode",
    )(page_table, proj, proj, proj, k_new, v_new,
      *([cache_k] * n_pages), *([cache_v] * n_pages), *([cache_ik] * n_pages))


COL_GATES, COL_XBC, COL_Q, COL_GLU, COL_Z, COL_QI, COL_K, COL_V, COL_SMALL, COL_DT, D_IN_PAD = (
    0, 3072, 4608, 5120, 6144, 7168, 7424, 7552, 7680, 7808, 7936)
ROW_BLOCK = 256


def _ln_rows(x, g, b):
    mu = jnp.mean(x, axis=-1, keepdims=True)
    xc = x - mu
    var = jnp.mean(xc * xc, axis=-1, keepdims=True)
    return xc * lax.rsqrt(var + LN_EPS) * g + b


def _bdot(a, w_ref):
    return jnp.dot(a.astype(jnp.bfloat16), w_ref[...], preferred_element_type=jnp.float32)


def _branch_mix_kernel(cv_ref, o_ref, y_ref, z_ref, gates_ref, x_ref, lncg_ref, lncb_ref, ng_ref,
                       wc_ref, wa_ref, ws_ref, wm_ref, lng_ref, lnb_ref, out_ref):
    ca = _ln_rows(cv_ref[...], lncg_ref[...], lncb_ref[...])
    y_a = _bdot(ca * jax.nn.sigmoid(ca), wc_ref)
    y_b = _bdot(o_ref[...], wa_ref)
    z = z_ref[...]
    t = y_ref[...] * (z * jax.nn.sigmoid(z))
    t = t * lax.rsqrt(jnp.mean(t * t, axis=-1, keepdims=True) + LN_EPS) * ng_ref[...]
    y_c = _bdot(t, ws_ref)
    d = D_MODEL
    mix = (jax.nn.sigmoid(gates_ref[:, 0:d]) * y_a + jax.nn.sigmoid(gates_ref[:, d:2 * d]) * y_b
           + jax.nn.sigmoid(gates_ref[:, 2 * d:3 * d]) * y_c)
    out_ref[...] = _ln_rows(DN_ALPHA * x_ref[...] + _bdot(mix, wm_ref), lng_ref[...], lnb_ref[...])


def branch_mix_ln(cv, o, y, proj, x, lncg, lncb, ng, wc, wa, ws, wm, lng, lnb):
    m = x.shape[0]
    tm = min(ROW_BLOCK, m)
    assert m % tm == 0
    d = D_MODEL

    def rows(width, col_block=0):
        return pl.BlockSpec((tm, width), lambda i, cb=col_block: (i, cb))

    def whole(a):
        return pl.BlockSpec(a.shape, lambda i: (0,) * a.ndim, pipeline_mode=pl.Buffered(1))

    vecs = [a.reshape(1, -1) for a in (lncg, lncb, ng)]
    lnv = [a.reshape(1, -1) for a in (lng, lnb)]
    return pl.pallas_call(
        _branch_mix_kernel,
        grid=(m // tm,),
        in_specs=[rows(CONV_DIM), rows(N_HEADS * HEAD_DIM), rows(SSM_INNER), rows(SSM_INNER, COL_Z // SSM_INNER),
                  rows(3 * d, COL_GATES // (3 * d)), rows(d)]
                 + [whole(a) for a in vecs] + [whole(a) for a in (wc, wa, ws, wm)] + [whole(a) for a in lnv],
        out_specs=rows(d),
        out_shape=jax.ShapeDtypeStruct((m, d), jnp.float32),
        compiler_params=pltpu.CompilerParams(dimension_semantics=("parallel",), vmem_limit_bytes=VMEM_LIMIT_BYTES),
        name="branch_mix_ln",
    )(cv, o, y, proj, proj, x, *vecs, wc, wa, ws, wm, *lnv)


def _mem_attn_kernel(x_ref, mk_ref, mv_ref, wq_ref, wo_ref, lng_ref, lnb_ref, out_ref, o_scr, *, seqs, t_seq):
    bf16 = jnp.bfloat16
    x = x_ref[...]
    q = _bdot(x, wq_ref)
    scale = 1.0 / math.sqrt(MEM_HEAD_DIM)
    for s in range(seqs):
        qs = q[s * t_seq:(s + 1) * t_seq].astype(bf16)
        for h in range(MEM_HEADS):
            cols = slice(h * MEM_HEAD_DIM, (h + 1) * MEM_HEAD_DIM)
            sc = _nt_dot(qs[:, cols], mk_ref[s, :, h, :].astype(bf16)) * scale
            p = jnp.exp(sc - jnp.max(sc, axis=-1, keepdims=True))
            p = p / jnp.sum(p, axis=-1, keepdims=True)
            o_scr[s * t_seq:(s + 1) * t_seq, cols] = jnp.dot(p.astype(bf16), mv_ref[s, :, h, :].astype(bf16),
                                                              preferred_element_type=jnp.float32)
    out_ref[...] = _ln_rows(DN_ALPHA * x + _bdot(o_scr[...], wo_ref), lng_ref[...], lnb_ref[...])


def mem_attn_ln(x, mk, mv, wq, wo, lng, lnb, t_seq, layer=None):
    m, d = x.shape
    hd = MEM_HEADS * MEM_HEAD_DIM
    if t_seq >= ROW_BLOCK:
        seqs, tm = 1, ROW_BLOCK
        assert t_seq % tm == 0
        per_seq = t_seq // tm
        seq_of = lambda i: i // per_seq
    else:
        seqs = max(1, 32 // t_seq)
        tm = seqs * t_seq
        assert m % tm == 0
        seq_of = lambda i: i
    if layer is None:
        mem_spec = pl.BlockSpec((seqs, N_MEM, MEM_HEADS, MEM_HEAD_DIM), lambda i: (seq_of(i), 0, 0, 0))
    else:
        mem_spec = pl.BlockSpec((seqs, None, N_MEM, MEM_HEADS, MEM_HEAD_DIM), lambda i: (seq_of(i), layer, 0, 0, 0))

    def whole(a):
        return pl.BlockSpec(a.shape, lambda i: (0,) * a.ndim, pipeline_mode=pl.Buffered(1))

    lnv = [a.reshape(1, -1) for a in (lng, lnb)]
    return pl.pallas_call(
        functools.partial(_mem_attn_kernel, seqs=seqs, t_seq=min(t_seq, tm)),
        grid=(m // tm,),
        in_specs=[pl.BlockSpec((tm, d), lambda i: (i, 0)), mem_spec, mem_spec, whole(wq), whole(wo)]
                 + [whole(a) for a in lnv],
        out_specs=pl.BlockSpec((tm, d), lambda i: (i, 0)),
        out_shape=jax.ShapeDtypeStruct((m, d), jnp.float32),
        scratch_shapes=[pltpu.VMEM((tm, hd), jnp.float32)],
        compiler_params=pltpu.CompilerParams(dimension_semantics=("parallel",), vmem_limit_bytes=VMEM_LIMIT_BYTES),
        name="mem_attn_ln",
    )(x, mk, mv, wq, wo, *lnv)


MOE_ROW_BLOCK = 1024


def _moe_kernel(x_ref, wr_hi_ref, wr_lo_ref, br_ref, wg_ref, wu_ref, wd_ref, lng_ref, lnb_ref, out_ref,
                xb_scr, comb_scr, acc_scr):
    f32 = jnp.float32
    e = pl.program_id(1)
    tm = x_ref.shape[0]
    lane = lax.broadcasted_iota(jnp.int32, (tm, LANES), 1).astype(f32)

    @pl.when(e == 0)
    def _route():
        x_hi, x_lo = _split_bf16(x_ref[...])
        xb_scr[...] = x_hi
        lg = (jnp.dot(x_hi, wr_hi_ref[...], preferred_element_type=f32)
              + jnp.dot(x_lo, wr_hi_ref[...], preferred_element_type=f32)
              + jnp.dot(x_hi, wr_lo_ref[...], preferred_element_type=f32) + br_ref[...])
        is_g = lane < N_GROUPS
        mg = jnp.max(jnp.where(is_g, lg, -jnp.inf), axis=1, keepdims=True)
        g_prob = 1.0 / jnp.sum(jnp.where(is_g, jnp.exp(lg - mg), 0.0), axis=1, keepdims=True)
        g_idx = jnp.min(jnp.where(is_g & (lg == mg), lane, float(LANES)), axis=1, keepdims=True)
        lo_e = N_GROUPS + EXPERTS_PER_GROUP * g_idx
        is_e = (lane >= lo_e) & (lane < lo_e + EXPERTS_PER_GROUP)
        me = jnp.max(jnp.where(is_e, lg, -jnp.inf), axis=1, keepdims=True)
        ee = jnp.where(is_e, jnp.exp(lg - me), 0.0)
        pe = jnp.where(is_e, ee / jnp.sum(ee, axis=1, keepdims=True), -1.0)
        p1 = jnp.max(pe, axis=1, keepdims=True)
        first = jnp.min(jnp.where(pe == p1, lane, float(LANES)), axis=1, keepdims=True)
        pe2 = jnp.where(lane == first, -1.0, pe)
        p2 = jnp.max(pe2, axis=1, keepdims=True)
        second = jnp.min(jnp.where(pe2 == p2, lane, float(LANES)), axis=1, keepdims=True)
        norm = g_prob / (p1 + p2)
        comb_scr[...] = jnp.where(lane == first, p1 * norm, jnp.where(lane == second, p2 * norm, 0.0))
        acc_scr[...] = jnp.zeros(acc_scr.shape, f32)

    xb = xb_scr[...]
    hg = jnp.dot(xb, wg_ref[0], preferred_element_type=f32)
    hu = jnp.dot(xb, wu_ref[0], preferred_element_type=f32)
    c = jnp.sum(jnp.where(lane == (e + N_GROUPS).astype(f32), comb_scr[...], 0.0), axis=1, keepdims=True)
    hid = hg * jax.nn.sigmoid(hg) * hu * c
    acc_scr[...] += jnp.dot(hid.astype(jnp.bfloat16), wd_ref[0], preferred_element_type=f32)

    @pl.when(e == pl.num_programs(1) - 1)
    def _finish():
        out_ref[...] = _ln_rows(DN_ALPHA * x_ref[...] + acc_scr[...], lng_ref[...], lnb_ref[...])


def moe_ln(x, w_group, b_group, w_router, b_router, wg, wu, wd, lng, lnb):
    m, d = x.shape
    tm = min(MOE_ROW_BLOCK, m)
    assert m % tm == 0 and N_GROUPS + N_EXPERTS <= LANES
    pad = jnp.zeros((d, LANES - N_GROUPS - N_EXPERTS), jnp.float32)
    wr_hi, wr_lo = _split_bf16(jnp.concatenate([w_group, w_router, pad], axis=1))
    br = jnp.concatenate([b_group, b_router, pad[0]]).reshape(1, LANES)
    lnv = [a.reshape(1, -1) for a in (lng, lnb)]

    def whole(a):
        return pl.BlockSpec(a.shape, lambda i, e: (0,) * a.ndim, pipeline_mode=pl.Buffered(1))

    return pl.pallas_call(
        _moe_kernel,
        grid=(m // tm, N_EXPERTS),
        in_specs=[pl.BlockSpec((tm, d), lambda i, e: (i, 0)), whole(wr_hi), whole(wr_lo), whole(br),
                  pl.BlockSpec((1, d, D_EXPERT), lambda i, e: (e, 0, 0)),
                  pl.BlockSpec((1, d, D_EXPERT), lambda i, e: (e, 0, 0)),
                  pl.BlockSpec((1, D_EXPERT, d), lambda i, e: (e, 0, 0))] + [whole(a) for a in lnv],
        out_specs=pl.BlockSpec((tm, d), lambda i, e: (i, 0)),
        out_shape=jax.ShapeDtypeStruct((m, d), jnp.float32),
        scratch_shapes=[pltpu.VMEM((tm, d), jnp.bfloat16), pltpu.VMEM((tm, LANES), jnp.float32),
                        pltpu.VMEM((tm, d), jnp.float32)],
        compiler_params=pltpu.CompilerParams(dimension_semantics=("parallel", "arbitrary"),
                                             vmem_limit_bytes=VMEM_LIMIT_BYTES),
        name="moe_ln",
    )(x, wr_hi, wr_lo, br, wg, wu, wd, *lnv)


CONV_HIST_ROWS = 32
CONV_ROW_BLOCK = 128
CONV_ROW_TILE = 32
SUBLANES = 8


def _conv_glu_kernel(x_ref, hist_ref, w_ref, cv_ref, st_ref, xp_scr, z_scr):
    f32 = jnp.float32
    ti = pl.program_id(1)
    tb = x_ref.shape[1]
    n_hist = CONV_WIDTH - 1
    base = CONV_HIST_ROWS - n_hist

    @pl.when(ti == 0)
    def _init():
        xp_scr[0:CONV_HIST_ROWS, :] = jnp.zeros((CONV_HIST_ROWS, CONV_DIM), f32)
        xp_scr[base:CONV_HIST_ROWS, :] = hist_ref[0]

    xin = x_ref[0]
    xp_scr[CONV_HIST_ROWS:CONV_HIST_ROWS + tb, :] = xin[:, :CONV_DIM] * jax.nn.sigmoid(xin[:, CONV_DIM:])
    if tb % SUBLANES == 0:
        rows_z = tb + SUBLANES
        xp_scr[CONV_HIST_ROWS + tb:CONV_HIST_ROWS + tb + 2 * SUBLANES, :] = jnp.zeros((2 * SUBLANES, CONV_DIM), f32)
        for b in range(SUBLANES):
            taps = [(a, SUBLANES * a + b - base) for a in range(CONV_HIST_ROWS // SUBLANES + 1)]
            taps = [(a, j) for a, j in taps if 0 <= j < CONV_WIDTH]
            for r0 in range(0, rows_z, CONV_ROW_TILE):
                nr = min(CONV_ROW_TILE, rows_z - r0)
                acc = None
                for a, j in taps:
                    term = w_ref[j:j + 1, :] * xp_scr[SUBLANES * a + r0:SUBLANES * a + r0 + nr, :]
                    acc = term if acc is None else acc + term
                z_scr[b, r0:r0 + nr, :] = acc
        out = z_scr[0, 0:tb, :]
        for b in range(1, SUBLANES):
            out = out + z_scr[b, b:b + tb, :]
    else:
        out = w_ref[0:1, :] * xp_scr[base:base + tb, :]
        for j in range(1, CONV_WIDTH):
            out = out + w_ref[j:j + 1, :] * xp_scr[base + j:base + j + tb, :]
    cv_ref[0] = out
    st_ref[0] = xp_scr[base + tb:base + tb + n_hist, :]
    xp_scr[base:CONV_HIST_ROWS, :] = xp_scr[base + tb:CONV_HIST_ROWS + tb, :]


def conv_glu(proj, hist, w):
    b, t, _ = proj.shape
    tb = CONV_ROW_BLOCK if t % CONV_ROW_BLOCK == 0 else t
    n_hist = CONV_WIDTH - 1
    assert tb >= n_hist or t == tb
    return pl.pallas_call(
        _conv_glu_kernel,
        grid=(b, t // tb),
        in_specs=[pl.BlockSpec((1, tb, 2 * CONV_DIM), lambda bi, ti: (bi, ti, COL_GLU // (2 * CONV_DIM))),
                  pl.BlockSpec((1, n_hist, CONV_DIM), lambda bi, ti: (bi, 0, 0)),
                  pl.BlockSpec(w.shape, lambda bi, ti: (0, 0))],
        out_specs=[pl.BlockSpec((1, tb, CONV_DIM), lambda bi, ti: (bi, ti, 0)),
                   pl.BlockSpec((1, n_hist, CONV_DIM), lambda bi, ti: (bi, 0, 0))],
        out_shape=[jax.ShapeDtypeStruct((b, t, CONV_DIM), jnp.float32),
                   jax.ShapeDtypeStruct((b, n_hist, CONV_DIM), jnp.float32)],
        scratch_shapes=[pltpu.VMEM((CONV_HIST_ROWS + tb + 2 * SUBLANES, CONV_DIM), jnp.float32),
                        pltpu.VMEM((SUBLANES, tb + SUBLANES, CONV_DIM), jnp.float32)],
        compiler_params=pltpu.CompilerParams(dimension_semantics=("parallel", "arbitrary"),
                                             vmem_limit_bytes=VMEM_LIMIT_BYTES),
        name="conv_glu",
    )(proj, hist, w)


SSD_HIST_ROWS = 8


def _ssd_kernel(xbc_ref, dt_ref, hist_ref, cw_ref, cb_ref, dtb_ref, alog_ref, dexp_ref, h0_ref,
                y_ref, hT_ref, xp_scr, ht_scr, xe_scr):
    f32, i32, bf16 = jnp.float32, jnp.int32, jnp.bfloat16
    c = pl.program_id(1)
    L = xbc_ref.shape[1]
    n_hist = SSM_CONV - 1
    base = SSD_HIST_ROWS - n_hist
    gw = SSM_STATE
    heads_per_group = SSM_HEADS // SSM_GROUPS
    pairs_per_group = heads_per_group // 2

    @pl.when(c == 0)
    def _init():
        xp_scr[base:SSD_HIST_ROWS, :] = hist_ref[0]
        ht_scr[...] = h0_ref[0]

    xp_scr[SSD_HIST_ROWS:SSD_HIST_ROWS + L, :] = xbc_ref[0]
    acc = cw_ref[0:1, :] * xp_scr[base:base + L, :]
    for j in range(1, SSM_CONV):
        acc = acc + cw_ref[j:j + 1, :] * xp_scr[base + j:base + j + L, :]
    acc = acc + cb_ref[...]
    xbc = acc * jax.nn.sigmoid(acc)
    xp_scr[base:SSD_HIST_ROWS, :] = xp_scr[base + L:SSD_HIST_ROWS + L, :]
    xs = xbc[:, 0:SSM_INNER]
    bm = xbc[:, SSM_INNER:SSM_INNER + SSM_GROUPS * gw]
    cm = xbc[:, SSM_INNER + SSM_GROUPS * gw:SSM_INNER + 2 * SSM_GROUPS * gw]

    x_dt = dt_ref[0] + dtb_ref[...]
    dtf = jnp.maximum(x_dt, 0.0) + jnp.log1p(jnp.exp(-jnp.abs(x_dt)))
    la = dtf * (-jnp.exp(alog_ref[...]))
    row = lax.broadcasted_iota(i32, (L, L), 0)
    colk = lax.broadcasted_iota(i32, (L, L), 1)
    causal = colk <= row
    tril = jnp.where(causal, 1.0, 0.0).astype(bf16)
    la_hi = la.astype(bf16)
    r1 = la - la_hi.astype(f32)
    la_mid = r1.astype(bf16)
    la_lo = (r1 - la_mid.astype(f32)).astype(bf16)
    cs = (jnp.dot(tril, la_hi, preferred_element_type=f32) + jnp.dot(tril, la_mid, preferred_element_type=f32)
          + jnp.dot(tril, la_lo, preferred_element_type=f32))
    cs_t = cs.T
    ecs = jnp.exp(cs)
    to_end = jnp.exp(cs[L - 1:L, :] - cs)

    lane = lax.broadcasted_iota(i32, (L, LANES), 1)
    half0 = lane < SSM_HEAD_DIM

    def pair_cols(a, h0):
        return jnp.where(half0, a[:, h0:h0 + 1], a[:, h0 + 1:h0 + 2])

    for g in range(SSM_GROUPS):
        b_g = bm[:, g * gw:(g + 1) * gw]
        c_g = cm[:, g * gw:(g + 1) * gw].astype(bf16)
        cb = _nt_dot(c_g, b_g.astype(bf16))
        cols = slice(g * heads_per_group * SSM_HEAD_DIM, (g + 1) * heads_per_group * SSM_HEAD_DIM)
        y_off = jnp.dot(c_g, ht_scr[:, cols].astype(bf16), preferred_element_type=f32)
        for i in range(pairs_per_group):
            h0 = g * heads_per_group + 2 * i
            pc = slice((h0 // 2) * LANES, (h0 // 2 + 1) * LANES)
            xs_p = xs[:, pc]
            xdt_p = xs_p * pair_cols(dtf, h0)
            att = []
            for hh in range(2):
                seg = cs[:, h0 + hh:h0 + hh + 1] - cs_t[h0 + hh:h0 + hh + 1, :]
                att.append((cb * jnp.exp(jnp.where(causal, seg, -jnp.inf))).astype(bf16))
            rhs = jnp.concatenate([jnp.where(half0, xdt_p, 0.0), jnp.where(half0, 0.0, xdt_p)], axis=0).astype(bf16)
            y_diag = jnp.dot(jnp.concatenate(att, axis=1), rhs, preferred_element_type=f32)
            xe_scr[:, pc] = xdt_p * pair_cols(to_end, h0)
            y_ref[0, :, pc] = (y_diag + y_off[:, i * LANES:(i + 1) * LANES] * pair_cols(ecs, h0)
                               + dexp_ref[:, pc] * xs_p)
        bt_hi, bt_lo = _split_bf16(b_g.T)
        xe_hi, xe_lo = _split_bf16(xe_scr[:, cols])
        st = (jnp.dot(bt_hi, xe_hi, preferred_element_type=f32) + jnp.dot(bt_lo, xe_hi, preferred_element_type=f32)
              + jnp.dot(bt_hi, xe_lo, preferred_element_type=f32))
        for i in range(pairs_per_group):
            h0 = g * heads_per_group + 2 * i
            pc = slice((h0 // 2) * LANES, (h0 // 2 + 1) * LANES)
            decay = jnp.where(half0[0:1, :], ecs[L - 1:L, h0:h0 + 1], ecs[L - 1:L, h0 + 1:h0 + 2])
            ht_scr[:, pc] = ht_scr[:, pc] * decay + st[:, i * LANES:(i + 1) * LANES]

    @pl.when(c == pl.num_programs(1) - 1)
    def _done():
        hT_ref[0] = ht_scr[...]


def ssd_prompt(proj, hist, conv_w, conv_b, dt_bias, a_log, d_skip, h0_t):
    b, t, _ = proj.shape
    L = SSM_CHUNK
    assert t % L == 0 and SSM_HEAD_DIM * 2 == LANES and SSM_STATE == LANES and SSM_HEADS <= LANES

    def lane_pad(a):
        return jnp.concatenate([a, jnp.zeros((LANES - a.shape[0],), a.dtype)]).reshape(1, LANES)

    def whole(a):
        return pl.BlockSpec(a.shape, lambda bi, c: (0,) * a.ndim)

    params = [conv_w, conv_b.reshape(1, -1), lane_pad(dt_bias), lane_pad(a_log),
              jnp.repeat(d_skip, SSM_HEAD_DIM).reshape(1, SSM_INNER)]
    return pl.pallas_call(
        _ssd_kernel,
        grid=(b, t // L),
        in_specs=[pl.BlockSpec((1, L, SSM_XBC), lambda bi, c: (bi, c, COL_XBC // SSM_XBC)),
                  pl.BlockSpec((1, L, LANES), lambda bi, c: (bi, c, COL_DT // LANES)),
                  pl.BlockSpec((1, SSM_CONV - 1, SSM_XBC), lambda bi, c: (bi, 0, 0))]
                 + [whole(a) for a in params]
                 + [pl.BlockSpec((1, SSM_STATE, SSM_INNER), lambda bi, c: (bi, 0, 0))],
        out_specs=[pl.BlockSpec((1, L, SSM_INNER), lambda bi, c: (bi, c, 0)),
                   pl.BlockSpec((1, SSM_STATE, SSM_INNER), lambda bi, c: (bi, 0, 0))],
        out_shape=[jax.ShapeDtypeStruct((b, t, SSM_INNER), jnp.float32),
                   jax.ShapeDtypeStruct((b, SSM_STATE, SSM_INNER), jnp.float32)],
        scratch_shapes=[pltpu.VMEM((SSD_HIST_ROWS + L, SSM_XBC), jnp.float32),
                        pltpu.VMEM((SSM_STATE, SSM_INNER), jnp.float32),
                        pltpu.VMEM((L, SSM_INNER), jnp.float32)],
        compiler_params=pltpu.CompilerParams(dimension_semantics=("parallel", "arbitrary"),
                                             vmem_limit_bytes=VMEM_LIMIT_BYTES),
        name="ssd_prompt",
    )(proj, proj, hist, *params, h0_t)


def causal_dwconv(x_pad, w):
    return lax.conv_general_dilated(x_pad, w[:, None, :], window_strides=(1,), padding='VALID',
                                    dimension_numbers=('NWC', 'WIO', 'NWC'),
                                    feature_group_count=x_pad.shape[-1])


def ssd_scan(x, dt, a, bm, cm, h0):
    f32 = jnp.float32
    bsz, l, nh, hp = x.shape
    rep = nh // bm.shape[2]
    chunk = min(SSM_CHUNK, l)
    assert l % chunk == 0
    xdt = x.astype(f32) * dt[..., None]
    la = dt * a
    bh = jnp.repeat(bm.astype(f32), rep, axis=2)
    ch = jnp.repeat(cm.astype(f32), rep, axis=2)
    nc = l // chunk
    ns = bh.shape[-1]
    xdt = xdt.reshape(bsz, nc, chunk, nh, hp)
    la = la.reshape(bsz, nc, chunk, nh)
    bh = bh.reshape(bsz, nc, chunk, nh, ns)
    ch = ch.reshape(bsz, nc, chunk, nh, ns)
    cs = jnp.cumsum(la, axis=2)
    causal = jnp.tril(jnp.ones((chunk, chunk), bool))
    seg = cs[:, :, :, None, :] - cs[:, :, None, :, :]
    decay = jnp.exp(jnp.where(causal[None, None, :, :, None], seg, -jnp.inf))
    att = jnp.einsum('bcqhn,bckhn->bcqkh', ch, bh) * decay
    y_diag = jnp.einsum('bcqkh,bckhp->bcqhp', att, xdt)
    to_end = jnp.exp(cs[:, :, -1:, :] - cs)
    states = jnp.einsum('bckhn,bckh,bckhp->bchpn', bh, to_end, xdt)
    chunk_decay = jnp.exp(cs[:, :, -1, :])

    def step(hc, inp):
        dc, st = inp
        return hc * dc[:, :, None, None] + st, hc

    h_last, h_prev = lax.scan(step, h0.astype(f32), (chunk_decay.swapaxes(0, 1), states.swapaxes(0, 1)))
    h_prev = h_prev.swapaxes(0, 1)
    y_off = jnp.einsum('bcqhn,bchpn,bcqh->bcqhp', ch, h_prev, jnp.exp(cs))
    y = (y_diag + y_off).reshape(bsz, nc * chunk, nh, hp)
    return y, h_last


def token_mixer(x, p, conv_buf, ssm_buf, ssm_h0, past):
    (w_in, conv_dw, ln_conv_g, ln_conv_b, w_conv_out, w_attn_out, ssm_conv_w, ssm_conv_b,
     ssm_dt_bias, ssm_a_log, ssm_d, ssm_norm_g, w_ssm_out, w_mix_out, ln_mix_g, ln_mix_b) = p
    b, t, _ = x.shape
    proj = mm_nt(x.reshape(b * t, D_MODEL), w_in).reshape(b, t, D_IN_PAD)
    k = proj[..., COL_K:COL_K + LANES]
    v = proj[..., COL_V:COL_V + LANES]
    small = proj[..., COL_SMALL:COL_SMALL + LANES]
    ki = small[..., 0:IDX_DIM]
    dt = small[..., IDX_DIM + IDX_HEADS:IDX_DIM + IDX_HEADS + SSM_HEADS]
    xbc = proj[..., COL_XBC:COL_XBC + SSM_XBC]
    cv, conv_state = conv_glu(proj, conv_buf, conv_dw)
    if past is None:
        o = dsa_prompt(proj, k, v, ki)
    else:
        cache_k, cache_v, cache_ik, page_table, layer = past
        o = dsa_decode(proj, k, v, cache_k, cache_v, cache_ik, page_table, layer)
    if past is None:
        h0_t = ssm_h0.reshape(b, SSM_INNER, SSM_STATE).swapaxes(1, 2)
        y, h_t = ssd_prompt(proj, ssm_buf, ssm_conv_w, ssm_conv_b, ssm_dt_bias, ssm_a_log, ssm_d, h0_t)
        h_last = h_t.swapaxes(1, 2).reshape(b, SSM_HEADS, SSM_HEAD_DIM, SSM_STATE)
        xbc_pad = xbc[:, -(SSM_CONV - 1):]
    else:
        xbc_pad = jnp.concatenate([ssm_buf, xbc], axis=1)
        xbc_c = jax.nn.silu(causal_dwconv(xbc_pad, ssm_conv_w) + ssm_conv_b)
        xs, bm, cm = jnp.split(xbc_c, [SSM_INNER, SSM_INNER + SSM_GROUPS * SSM_STATE], axis=-1)
        xs = xs.reshape(b, t, SSM_HEADS, SSM_HEAD_DIM)
        dtf = jax.nn.softplus(dt + ssm_dt_bias)
        a = -jnp.exp(ssm_a_log)
        y, h_last = ssd_scan(xs, dtf, a, bm.reshape(b, t, SSM_GROUPS, SSM_STATE),
                             cm.reshape(b, t, SSM_GROUPS, SSM_STATE), ssm_h0)
        y = (y + ssm_d[:, None] * xs).reshape(b, t, SSM_INNER)
    m = b * t
    x_new = branch_mix_ln(cv.reshape(m, CONV_DIM), o.reshape(m, -1), y.reshape(m, SSM_INNER),
                          proj.reshape(m, D_IN_PAD), x.reshape(m, D_MODEL), ln_conv_g, ln_conv_b, ssm_norm_g,
                          w_conv_out, w_attn_out, w_ssm_out, w_mix_out, ln_mix_g, ln_mix_b).reshape(b, t, D_MODEL)
    return (x_new, k.reshape(b, t, N_KV_HEADS, HEAD_DIM), v.reshape(b, t, N_KV_HEADS, HEAD_DIM), ki,
            conv_state, xbc_pad[:, -(SSM_CONV - 1):], h_last)


def _pad_w_in_t(wt):
    sp = (0,) + SPLIT_POINTS + (wt.shape[0],)
    seg = [wt[sp[i]:sp[i + 1]] for i in range(len(SPLIT_SIZES))]
    glu, q, k, v, qi, ki, wi, z, xbc, dt, gates = seg
    pad = jnp.zeros((LANES - IDX_DIM - IDX_HEADS - SSM_HEADS, wt.shape[1]), wt.dtype)
    n_rep = N_HEADS // N_KV_HEADS
    q = q.reshape(N_KV_HEADS, n_rep, HEAD_DIM, -1).swapaxes(0, 1).reshape(q.shape)
    dt_pad = jnp.zeros((LANES - SSM_HEADS, wt.shape[1]), wt.dtype)
    out = jnp.concatenate([gates, xbc, q, glu, z, qi, k, v, ki, wi, dt, pad, dt, dt_pad], axis=0)
    assert out.shape[0] == D_IN_PAD
    return out


def _perm_w_attn_out(w):
    n_rep = N_HEADS // N_KV_HEADS
    return w.reshape(N_KV_HEADS, n_rep, HEAD_DIM, -1).swapaxes(0, 1).reshape(w.shape)


def kernel(x_prompt, x_sample, mem_prompt, cache_k, cache_v, cache_ik, cache_mem_k, cache_mem_v, state_conv, state_ssm_conv, state_ssm, page_table, w_in, conv_dw, ln_conv_g, ln_conv_b, w_conv_out, w_attn_out, ssm_conv_w, ssm_conv_b, ssm_dt_bias, ssm_a_log, ssm_d, ssm_norm_g, w_ssm_out, w_mix_out, ln_mix_g, ln_mix_b, w_mq, w_mk, w_mv, w_mo, ln_mem_g, ln_mem_b, w_group, b_group, w_router, b_router, w_e_gate, w_e_up, w_e_down, ln_ffn_g, ln_ffn_b):
    bf16 = jnp.bfloat16
    bp, tp, _ = x_prompt.shape
    bs, ts, _ = x_sample.shape
    xp, xs = x_prompt, x_sample
    outs_p = [[] for _ in range(8)]
    outs_s = [[] for _ in range(6)]
    for l in range(DEPTH):
        mix_p = (_pad_w_in_t(jnp.transpose(w_in[l])).astype(bf16), conv_dw[l], ln_conv_g[l], ln_conv_b[l],
                 w_conv_out[l].astype(bf16), _perm_w_attn_out(w_attn_out[l]).astype(bf16),
                 ssm_conv_w[l], ssm_conv_b[l], ssm_dt_bias[l], ssm_a_log[l], ssm_d[l], ssm_norm_g[l],
                 w_ssm_out[l].astype(bf16), w_mix_out[l].astype(bf16), ln_mix_g[l], ln_mix_b[l])
        moe_p = (w_group[l], b_group[l], w_router[l], b_router[l],
                 w_e_gate[l].astype(bf16), w_e_up[l].astype(bf16), w_e_down[l].astype(bf16), ln_ffn_g[l], ln_ffn_b[l])
        mem_p = (w_mq[l].astype(bf16), w_mo[l].astype(bf16), ln_mem_g[l], ln_mem_b[l])
        hd_mem = MEM_HEADS * MEM_HEAD_DIM
        xp, kp, vp, kip, cbp, sbp, hp = token_mixer(
            xp, mix_p,
            jnp.zeros((bp, CONV_WIDTH - 1, CONV_DIM), xp.dtype),
            jnp.zeros((bp, SSM_CONV - 1, SSM_XBC), xp.dtype),
            jnp.zeros((bp, SSM_HEADS, SSM_HEAD_DIM, SSM_STATE), jnp.float32),
            None)
        mem_kv = mm(mem_prompt.reshape(bp * N_MEM, D_MODEL),
                    jnp.concatenate([w_mk[l], w_mv[l]], axis=1).astype(bf16)).reshape(bp, N_MEM, 2 * hd_mem)
        mkp = mem_kv[..., :hd_mem].reshape(bp, N_MEM, MEM_HEADS, MEM_HEAD_DIM)
        mvp = mem_kv[..., hd_mem:].reshape(bp, N_MEM, MEM_HEADS, MEM_HEAD_DIM)
        xp = mem_attn_ln(xp.reshape(bp * tp, D_MODEL), mkp, mvp, *mem_p, t_seq=tp)
        xp = moe_ln(xp, *moe_p).reshape(bp, tp, D_MODEL)
        xs, ks_new, vs_new, kis, cbs, sbs, hs = token_mixer(
            xs, mix_p, state_conv[:, l], state_ssm_conv[:, l], state_ssm[:, l],
            (cache_k, cache_v, cache_ik, page_table, l))
        xs = mem_attn_ln(xs.reshape(bs * ts, D_MODEL), cache_mem_k, cache_mem_v, *mem_p, t_seq=ts, layer=l)
        xs = moe_ln(xs, *moe_p).reshape(bs, ts, D_MODEL)
        for lst, arr in zip(outs_p, (kp, vp, kip, mkp, mvp, cbp, sbp, hp)):
            lst.append(arr)
        for lst, arr in zip(outs_s, (ks_new, vs_new, kis, cbs, sbs, hs)):
            lst.append(arr)
    p_k, p_v, p_ik, p_mem_k, p_mem_v, p_conv, p_ssm_conv, p_ssm = [jnp.stack(a, axis=1) for a in outs_p]
    s_k, s_v, s_ik, s_conv, s_ssm_conv, s_ssm = [jnp.stack(a, axis=1) for a in outs_s]
    return (xp, xs, p_k, p_v, p_ik, p_mem_k, p_mem_v, p_conv, p_ssm_conv, p_ssm,
            s_k, s_v, s_ik, s_conv, s_ssm_conv, s_ssm)
```

```python
import functools
import math

import jax
import jax.numpy as jnp
from jax import lax
from jax.experimental import pallas as pl
from jax.experimental.pallas import tpu as pltpu

D_MODEL = 1024
DEPTH = 2
PAGE_SIZE = 128
CONV_DIM = 512
CONV_WIDTH = 31
N_HEADS = 8
N_KV_HEADS = 2
HEAD_DIM = 64
IDX_HEADS = 4
IDX_DIM = 64
TOPK_MAX = 256
Q_BLOCK = 128
SSM_HEADS = 16
SSM_HEAD_DIM = 64
SSM_INNER = SSM_HEADS * SSM_HEAD_DIM
SSM_GROUPS = 2
SSM_STATE = 128
SSM_CONV = 4
SSM_CHUNK = 128
SSM_XBC = SSM_INNER + 2 * SSM_GROUPS * SSM_STATE
N_MEM = 256
MEM_HEADS = 4
MEM_HEAD_DIM = 128
N_GROUPS = 4
EXPERTS_PER_GROUP = 4
N_EXPERTS = N_GROUPS * EXPERTS_PER_GROUP
TOPK_IN_GROUP = 2
D_EXPERT = 512
DN_ALPHA = (2 * DEPTH) ** 0.25
LN_EPS = 1e-5

SPLIT_SIZES = (2 * CONV_DIM, N_HEADS * HEAD_DIM, N_KV_HEADS * HEAD_DIM, N_KV_HEADS * HEAD_DIM,
               IDX_HEADS * IDX_DIM, IDX_DIM, IDX_HEADS, SSM_INNER, SSM_XBC, SSM_HEADS, 3 * D_MODEL)
SPLIT_POINTS = tuple(sum(SPLIT_SIZES[:i + 1]) for i in range(len(SPLIT_SIZES) - 1))

VMEM_LIMIT_BYTES = 56 * 1024 * 1024
MM_COL_CHUNK = 512


def _mm_kernel(x_ref, w_ref, o_ref):
    xb = x_ref[...].astype(jnp.bfloat16)
    n = o_ref.shape[1]
    for c0 in range(0, n, MM_COL_CHUNK):
        c1 = min(n, c0 + MM_COL_CHUNK)
        o_ref[:, c0:c1] = jnp.dot(xb, w_ref[:, c0:c1], preferred_element_type=jnp.float32)


def mm(x, w, tm=256):
    m, k = x.shape
    n = w.shape[1]
    tm = min(tm, m)
    assert m % tm == 0 and n % 128 == 0 and k % 128 == 0
    return pl.pallas_call(
        _mm_kernel,
        grid=(m // tm,),
        in_specs=[pl.BlockSpec((tm, k), lambda i: (i, 0)),
                  pl.BlockSpec((k, n), lambda i: (0, 0), pipeline_mode=pl.Buffered(1))],
        out_specs=pl.BlockSpec((tm, n), lambda i: (i, 0)),
        out_shape=jax.ShapeDtypeStruct((m, n), jnp.float32),
        compiler_params=pltpu.CompilerParams(dimension_semantics=("parallel",),
                                             vmem_limit_bytes=VMEM_LIMIT_BYTES),
    )(x, w)


def _mm_nt_kernel(x_ref, wt_ref, o_ref):
    xb = x_ref[...].astype(jnp.bfloat16)
    n = o_ref.shape[1]
    for c0 in range(0, n, MM_COL_CHUNK):
        c1 = min(n, c0 + MM_COL_CHUNK)
        o_ref[:, c0:c1] = lax.dot_general(xb, wt_ref[c0:c1, :], (((1,), (1,)), ((), ())),
                                          preferred_element_type=jnp.float32)


def mm_nt(x, wt, tm=256):
    m, k = x.shape
    n = wt.shape[0]
    tm = min(tm, m)
    assert m % tm == 0 and n % 128 == 0 and k % 128 == 0
    return pl.pallas_call(
        _mm_nt_kernel,
        grid=(m // tm,),
        in_specs=[pl.BlockSpec((tm, k), lambda i: (i, 0)),
                  pl.BlockSpec((n, k), lambda i: (0, 0), pipeline_mode=pl.Buffered(1))],
        out_specs=pl.BlockSpec((tm, n), lambda i: (i, 0)),
        out_shape=jax.ShapeDtypeStruct((m, n), jnp.float32),
        compiler_params=pltpu.CompilerParams(dimension_semantics=("parallel",),
                                             vmem_limit_bytes=VMEM_LIMIT_BYTES),
        name="in_proj",
    )(x, wt)


INT32_MIN = -2 ** 31
MASK_BIAS = -1e30
DSA_KEY_CHUNK = 512
LANES = 128


def _nt_dot(a, b):
    return lax.dot_general(a, b, (((1,), (1,)), ((), ())), preferred_element_type=jnp.float32)


def _dsa_prompt_kernel(q_ref, qi_ref, sm_ref, k_ref, v_ref, ki_ref, o_ref,
                       key_scr, rank_scr, s_scr, p_scr, qpad_scr, qis_scr, m_scr, l_scr, acc_scr, *, k_sel):
    f32, i32, bf16 = jnp.float32, jnp.int32, jnp.bfloat16
    qb = q_ref.shape[1]
    kc_w = DSA_KEY_CHUNK
    sub = kc_w // LANES
    i = pl.program_id(1)
    n_chunks = (i * qb + qb + kc_w - 1) // kc_w
    lane = lax.broadcasted_iota(i32, (qb, LANES), 1)
    half = [lane < HEAD_DIM, lane >= HEAD_DIM]

    for pair in range(IDX_HEADS // 2):
        src = qi_ref[0, :, pair * LANES:(pair + 1) * LANES]
        hi = src.astype(bf16).astype(f32)
        lo_swapped = pltpu.roll(src - hi, IDX_DIM, axis=1)
        for hh in range(2):
            h = 2 * pair + hh
            qis_scr[h * qb:(h + 1) * qb, 0:LANES] = jnp.where(half[hh], hi, lo_swapped).astype(bf16)
            qis_scr[h * qb:(h + 1) * qb, LANES:2 * LANES] = jnp.where(half[hh], hi, 0.0).astype(bf16)
    w_idx = [sm_ref[0, :, IDX_DIM + h:IDX_DIM + h + 1] for h in range(IDX_HEADS)]
    row_pos = i * qb + lax.broadcasted_iota(i32, (qb, kc_w), 0)
    col_iota = lax.broadcasted_iota(i32, (qb, kc_w), 1)

    def score_body(c, carry):
        off = pl.multiple_of(c * kc_w, kc_w)
        d = _nt_dot(qis_scr[...], ki_ref[0, pl.ds(off, kc_w), :])
        idx = w_idx[0] * jnp.maximum(d[0:qb], 0.0)
        for h in range(1, IDX_HEADS):
            idx = idx + w_idx[h] * jnp.maximum(d[h * qb:(h + 1) * qb], 0.0)
        idx = jnp.where(idx == 0.0, 0.0, idx)
        bits = lax.bitcast_convert_type(idx, i32)
        key = bits ^ ((bits >> 31) & 0x7FFFFFFF)
        key_scr[c] = jnp.where(off + col_iota <= row_pos, key, INT32_MIN)
        return carry

    lax.fori_loop(0, n_chunks, score_body, 0)

    def count(pred):
        def body(c, acc):
            for g in range(sub):
                acc = acc + jnp.where(pred(key_scr[c, :, g * LANES:(g + 1) * LANES]), 1.0, 0.0)
            return acc
        acc = lax.fori_loop(0, n_chunks, body, jnp.zeros((qb, LANES), f32))
        return jnp.sum(acc, axis=1, keepdims=True)

    def search_body(it, lo):
        cand = lo + lax.shift_left(jnp.int32(1), 31 - it)
        cand_b = jnp.broadcast_to(cand, (qb, LANES))
        cnt = count(lambda kk: kk >= cand_b)
        return jnp.where(cnt >= float(k_sel), cand, lo)

    thr = lax.fori_loop(0, 32, search_body, jnp.full((qb, 1), INT32_MIN, i32))
    thr_b = jnp.broadcast_to(thr, (qb, LANES))
    need_b = jnp.broadcast_to(float(k_sel) - count(lambda kk: kk > thr_b), (qb, LANES))
    thr_w = jnp.broadcast_to(thr, (qb, kc_w))
    need_w = jnp.tile(need_b, (1, sub))

    r_i = lax.broadcasted_iota(i32, (kc_w, kc_w + LANES), 0)
    c_i = lax.broadcasted_iota(i32, (kc_w, kc_w + LANES), 1)
    rank_scr[...] = jnp.where((r_i <= c_i) | (c_i >= kc_w), 1.0, 0.0).astype(bf16)
    zero_bits = jnp.zeros((qb, kc_w), i32)
    mask_bits = lax.bitcast_convert_type(jnp.full((qb, kc_w), MASK_BIAS, f32), i32)

    def select_body(c, seen):
        kk = key_scr[c]
        eq = kk == thr_w
        pr = jnp.dot(jnp.where(eq, 1.0, 0.0).astype(bf16), rank_scr[...], preferred_element_type=f32)
        rank = jnp.tile(seen, (1, sub)) + pr[:, 0:kc_w]
        sel = ((kk > thr_w) | (eq & (rank <= need_w))) & (kk > INT32_MIN)
        key_scr[c] = jnp.where(sel, zero_bits, mask_bits)
        return seen + pr[:, kc_w:kc_w + LANES]

    lax.fori_loop(0, n_chunks, select_body, jnp.zeros((qb, LANES), f32))

    n_blk = N_HEADS
    scale = 1.0 / math.sqrt(HEAD_DIM)
    for g in range(N_HEADS // N_KV_HEADS):
        src = q_ref[0, :, g * LANES:(g + 1) * LANES] * scale
        for j in range(N_KV_HEADS):
            r = g * N_KV_HEADS + j
            qpad_scr[r * qb:(r + 1) * qb, :] = jnp.where(half[j], src, 0.0).astype(bf16)
    m_scr[...] = jnp.full(m_scr.shape, -jnp.inf, f32)
    l_scr[...] = jnp.zeros(l_scr.shape, f32)
    acc_scr[...] = jnp.zeros(acc_scr.shape, f32)

    def attend_body(c, carry):
        off = pl.multiple_of(c * kc_w, kc_w)
        k_c = k_ref[0, pl.ds(off, kc_w), :]
        v_c = v_ref[0, pl.ds(off, kc_w), :]
        bias = lax.bitcast_convert_type(key_scr[c], f32)
        s_scr[...] = _nt_dot(qpad_scr[...], k_c)
        for r in range(n_blk):
            rows = slice(r * qb, (r + 1) * qb)
            s = s_scr[rows, :] + bias
            m_old = m_scr[rows, :]
            m_new = jnp.maximum(m_old, jnp.broadcast_to(jnp.max(s, axis=1, keepdims=True), (qb, LANES)))
            alpha = jnp.exp(m_old - m_new)
            p = jnp.exp(s - jnp.tile(m_new, (1, sub)))
            l_scr[rows, :] = alpha * l_scr[rows, :] + jnp.broadcast_to(jnp.sum(p, axis=1, keepdims=True), (qb, LANES))
            acc_scr[rows, :] = alpha * acc_scr[rows, :]
            p_scr[rows, :] = p.astype(bf16)
            m_scr[rows, :] = m_new
        acc_scr[...] += jnp.dot(p_scr[...], v_c, preferred_element_type=f32)
        return carry

    lax.fori_loop(0, n_chunks, attend_body, 0)

    for g in range(N_HEADS // N_KV_HEADS):
        outs = []
        for j in range(N_KV_HEADS):
            rows = slice((g * N_KV_HEADS + j) * qb, (g * N_KV_HEADS + j + 1) * qb)
            outs.append(acc_scr[rows, :] / l_scr[rows, :])
        o_ref[0, :, g * LANES:(g + 1) * LANES] = jnp.where(half[0], outs[0], outs[1])


def _split_bf16(x):
    hi = x.astype(jnp.bfloat16)
    return hi, (x - hi.astype(jnp.float32)).astype(jnp.bfloat16)


def dsa_prompt(proj, k, v, ki):
    b, t, _ = proj.shape
    qb = Q_BLOCK
    w_q, w_qi = N_HEADS * HEAD_DIM, IDX_HEADS * IDX_DIM
    assert t % DSA_KEY_CHUNK == 0 and t % qb == 0
    assert IDX_DIM == HEAD_DIM == LANES // 2 and IDX_HEADS % 2 == 0
    k_sel = min(TOPK_MAX, t // 4)
    n_rows = N_HEADS * qb
    ki_hi, ki_lo = _split_bf16(ki)
    ki4 = jnp.concatenate([ki_hi, ki_hi, ki_lo, ki_lo], axis=-1)
    k, v = k.astype(jnp.bfloat16), v.astype(jnp.bfloat16)
    seq_spec = pl.BlockSpec((1, t, LANES), lambda bi, i: (bi, 0, 0))
    return pl.pallas_call(
        functools.partial(_dsa_prompt_kernel, k_sel=k_sel),
        grid=(b, t // qb),
        in_specs=[pl.BlockSpec((1, qb, w_q), lambda bi, i: (bi, i, COL_Q // w_q)),
                  pl.BlockSpec((1, qb, w_qi), lambda bi, i: (bi, i, COL_QI // w_qi)),
                  pl.BlockSpec((1, qb, LANES), lambda bi, i: (bi, i, COL_SMALL // LANES)),
                  seq_spec, seq_spec, pl.BlockSpec((1, t, 2 * LANES), lambda bi, i: (bi, 0, 0))],
        out_specs=pl.BlockSpec((1, qb, N_HEADS * HEAD_DIM), lambda bi, i: (bi, i, 0)),
        out_shape=jax.ShapeDtypeStruct((b, t, N_HEADS * HEAD_DIM), jnp.float32),
        scratch_shapes=[pltpu.VMEM((t // DSA_KEY_CHUNK, qb, DSA_KEY_CHUNK), jnp.int32),
                        pltpu.VMEM((DSA_KEY_CHUNK, DSA_KEY_CHUNK + LANES), jnp.bfloat16),
                        pltpu.VMEM((n_rows, DSA_KEY_CHUNK), jnp.float32),
                        pltpu.VMEM((n_rows, DSA_KEY_CHUNK), jnp.bfloat16),
                        pltpu.VMEM((n_rows, LANES), jnp.bfloat16),
                        pltpu.VMEM((IDX_HEADS * qb, 2 * LANES), jnp.bfloat16),
                        pltpu.VMEM((n_rows, LANES), jnp.float32),
                        pltpu.VMEM((n_rows, LANES), jnp.float32),
                        pltpu.VMEM((n_rows, LANES), jnp.float32)],
        compiler_params=pltpu.CompilerParams(dimension_semantics=("parallel", "arbitrary"),
                                             vmem_limit_bytes=VMEM_LIMIT_BYTES),
        name="dsa_prompt",
    )(proj, proj, proj, k, v, ki4)


DEC_ROWS = 8


def _dsa_decode_kernel(pt_ref, q_ref, qi_ref, sm_ref, kn_ref, vn_ref, *rest, n_pages, k_sel):
    f32, i32, bf16 = jnp.float32, jnp.int32, jnp.bfloat16
    k_pages = rest[0:n_pages]
    v_pages = rest[n_pages:2 * n_pages]
    ik_pages = rest[2 * n_pages:3 * n_pages]
    o_ref = rest[3 * n_pages]
    kall, vall, ikall, new_scr, qis, wpad, qpad, key_scr, p_scr = rest[3 * n_pages + 1:]
    t_new = q_ref.shape[1]
    past = n_pages * PAGE_SIZE
    s_pad = past + PAGE_SIZE
    n_chunks = s_pad // LANES
    rows = DEC_ROWS

    for p in range(n_pages):
        sl = slice(p * PAGE_SIZE, (p + 1) * PAGE_SIZE)
        for j in range(N_KV_HEADS):
            kall[j * HEAD_DIM:(j + 1) * HEAD_DIM, sl] = k_pages[p][j]
            vall[j * HEAD_DIM:(j + 1) * HEAD_DIM, sl] = v_pages[p][j]
        ikall[:, sl] = ik_pages[p][...]
    tail = slice(past, s_pad)
    for src, dst, n_feat in ((kn_ref[0], kall, LANES), (vn_ref[0], vall, LANES),
                             (sm_ref[0], ikall, IDX_DIM)):
        new_scr[...] = jnp.zeros(new_scr.shape, f32)
        new_scr[0:t_new, :] = src
        dst[:, tail] = new_scr[...].T[0:n_feat, :]

    qis[...] = jnp.zeros(qis.shape, f32)
    wpad[...] = jnp.zeros(wpad.shape, f32)
    for h in range(IDX_HEADS):
        qis[h * rows:h * rows + t_new, :] = qi_ref[0, :, h * IDX_DIM:(h + 1) * IDX_DIM]
    wpad[0:t_new, :] = sm_ref[0]
    q_hi, q_lo = _split_bf16(qis[...])
    k_hi, k_lo = _split_bf16(ikall[...])

    def mm_f32(a, b):
        return jnp.dot(a, b, preferred_element_type=f32)

    d = mm_f32(q_hi, k_hi) + mm_f32(q_lo, k_hi) + mm_f32(q_hi, k_lo)
    idx = wpad[:, IDX_DIM:IDX_DIM + 1] * jnp.maximum(d[0:rows], 0.0)
    for h in range(1, IDX_HEADS):
        idx = idx + wpad[:, IDX_DIM + h:IDX_DIM + h + 1] * jnp.maximum(d[h * rows:(h + 1) * rows], 0.0)
    idx = jnp.where(idx == 0.0, 0.0, idx)
    bits = lax.bitcast_convert_type(idx, i32)
    key = bits ^ ((bits >> 31) & 0x7FFFFFFF)
    col = lax.broadcasted_iota(i32, (rows, s_pad), 1)
    q_pos = past + lax.broadcasted_iota(i32, (rows, s_pad), 0)
    key_scr[...] = jnp.where((col <= q_pos) & (col < past + t_new), key, INT32_MIN)

    def search_body(it, lo):
        step = lax.shift_left(jnp.int32(1), 30 - 2 * it)
        keys = key_scr[...]
        for mult in (1, 2, 3):
            cand = lo + step * mult
            cnt = jnp.sum(jnp.where(keys >= cand, 1.0, 0.0), axis=1, keepdims=True)
            best = jnp.where(cnt >= float(k_sel), cand, lo if mult == 1 else best)
        return best

    thr = lax.fori_loop(0, 16, search_body, jnp.full((rows, 1), INT32_MIN, i32))
    need = float(k_sel) - jnp.sum(jnp.where(key_scr[...] > thr, 1.0, 0.0), axis=1, keepdims=True)
    tri = jnp.where(lax.broadcasted_iota(i32, (LANES, LANES), 0) <= lax.broadcasted_iota(i32, (LANES, LANES), 1),
                    1.0, 0.0).astype(bf16)
    eq_f = [jnp.where(key_scr[:, c * LANES:(c + 1) * LANES] == thr, 1.0, 0.0) for c in range(n_chunks)]
    totals = [jnp.sum(e, axis=1, keepdims=True) for e in eq_f]
    seen = jnp.zeros((rows, 1), f32)
    for c in range(n_chunks):
        kk = key_scr[:, c * LANES:(c + 1) * LANES]
        rank = seen + jnp.dot(eq_f[c].astype(bf16), tri, preferred_element_type=f32)
        sel = ((kk > thr) | ((kk == thr) & (rank <= need))) & (kk > INT32_MIN)
        key_scr[:, c * LANES:(c + 1) * LANES] = lax.bitcast_convert_type(jnp.where(sel, 0.0, MASK_BIAS), i32)
        seen = seen + totals[c]
    bias = lax.bitcast_convert_type(key_scr[...], f32)

    lane = lax.broadcasted_iota(i32, (t_new, LANES), 1)
    half = [lane < HEAD_DIM, lane >= HEAD_DIM]
    scale = 1.0 / math.sqrt(HEAD_DIM)
    qpad[...] = jnp.zeros(qpad.shape, f32)
    for g in range(N_HEADS // N_KV_HEADS):
        src = q_ref[0, :, g * LANES:(g + 1) * LANES] * scale
        for j in range(N_KV_HEADS):
            r = g * N_KV_HEADS + j
            qpad[r * rows:r * rows + t_new, :] = jnp.where(half[j], src, 0.0)
    s_all = mm_f32(qpad[...].astype(bf16), kall[...].astype(bf16))
    for r in range(N_HEADS):
        s = s_all[r * rows:(r + 1) * rows] + bias
        p = jnp.exp(s - jnp.max(s, axis=1, keepdims=True))
        p_scr[r * rows:(r + 1) * rows, :] = p / jnp.sum(p, axis=1, keepdims=True)
    o_all = _nt_dot(p_scr[...].astype(bf16), vall[...].astype(bf16))
    for g in range(N_HEADS // N_KV_HEADS):
        r0, r1 = g * N_KV_HEADS * rows, (g * N_KV_HEADS + 1) * rows
        o_ref[0, :, g * LANES:(g + 1) * LANES] = jnp.where(half[0], o_all[r0:r0 + t_new], o_all[r1:r1 + t_new])


def dsa_decode(proj, k_new, v_new, cache_k, cache_v, cache_ik, page_table, layer):
    b, t_new, _ = proj.shape
    n_pages = page_table.shape[1]
    assert t_new <= DEC_ROWS and PAGE_SIZE == LANES
    s_pad = (n_pages + 1) * PAGE_SIZE
    k_sel = min(TOPK_MAX, (n_pages * PAGE_SIZE + t_new) // 4)
    cache_k = jnp.transpose(cache_k, (0, 1, 3, 4, 2))
    cache_v = jnp.transpose(cache_v, (0, 1, 3, 4, 2))
    cache_ik = jnp.transpose(cache_ik, (0, 1, 3, 2))

    def tok_spec(width, col=0):
        return pl.BlockSpec((1, t_new, width), lambda bi, pt: (bi, 0, col // width))

    def page_spec(p):
        return pl.BlockSpec((None, None, IDX_DIM, PAGE_SIZE), lambda bi, pt, p=p: (pt[bi, p], layer, 0, 0))

    def kv_page_spec(p):
        return pl.BlockSpec((None, None, N_KV_HEADS, HEAD_DIM, PAGE_SIZE),
                            lambda bi, pt, p=p: (pt[bi, p], layer, 0, 0, 0))

    grid_spec = pltpu.PrefetchScalarGridSpec(
        num_scalar_prefetch=1,
        grid=(b,),
        in_specs=[tok_spec(N_HEADS * HEAD_DIM, COL_Q), tok_spec(IDX_HEADS * IDX_DIM, COL_QI),
                  tok_spec(LANES, COL_SMALL), tok_spec(LANES), tok_spec(LANES)]
                 + [kv_page_spec(p) for p in range(n_pages)]
                 + [kv_page_spec(p) for p in range(n_pages)]
                 + [page_spec(p) for p in range(n_pages)],
        out_specs=tok_spec(N_HEADS * HEAD_DIM),
        scratch_shapes=[pltpu.VMEM((LANES, s_pad), jnp.float32),
                        pltpu.VMEM((LANES, s_pad), jnp.float32),
                        pltpu.VMEM((IDX_DIM, s_pad), jnp.float32),
                        pltpu.VMEM((LANES, LANES), jnp.float32),
                        pltpu.VMEM((IDX_HEADS * DEC_ROWS, IDX_DIM), jnp.float32),
                        pltpu.VMEM((DEC_ROWS, LANES), jnp.float32),
                        pltpu.VMEM((N_HEADS * DEC_ROWS, LANES), jnp.float32),
                        pltpu.VMEM((DEC_ROWS, s_pad), jnp.int32),
                        pltpu.VMEM((N_HEADS * DEC_ROWS, s_pad), jnp.float32)])
    return pl.pallas_call(
        functools.partial(_dsa_decode_kernel, n_pages=n_pages, k_sel=k_sel),
        grid_spec=grid_spec,
        out_shape=jax.ShapeDtypeStruct((b, t_new, N_HEADS * HEAD_DIM), jnp.float32),
        compiler_params=pltpu.CompilerParams(dimension_semantics=("arbitrary",),
                                             vmem_limit_bytes=VMEM_LIMIT_BYTES),
        name="dsa_decode",
    )(page_table, proj, proj, proj, k_new, v_new,
      *([cache_k] * n_pages), *([cache_v] * n_pages), *([cache_ik] * n_pages))


COL_GATES, COL_XBC, COL_Q, COL_GLU, COL_Z, COL_QI, COL_K, COL_V, COL_SMALL, COL_DT, D_IN_PAD = (
    0, 3072, 4608, 5120, 6144, 7168, 7424, 7552, 7680, 7808, 7936)
ROW_BLOCK = 256


def _ln_rows(x, g, b):
    mu = jnp.mean(x, axis=-1, keepdims=True)
    xc = x - mu
    var = jnp.mean(xc * xc, axis=-1, keepdims=True)
    return xc * lax.rsqrt(var + LN_EPS) * g + b


def _bdot(a, w_ref):
    return jnp.dot(a.astype(jnp.bfloat16), w_ref[...], preferred_element_type=jnp.float32)


def _branch_mix_kernel(cv_ref, o_ref, y_ref, z_ref, gates_ref, x_ref, lncg_ref, lncb_ref, ng_ref,
                       wc_ref, wa_ref, ws_ref, wm_ref, lng_ref, lnb_ref, out_ref):
    ca = _ln_rows(cv_ref[...], lncg_ref[...], lncb_ref[...])
    y_a = _bdot(ca * jax.nn.sigmoid(ca), wc_ref)
    y_b = _bdot(o_ref[...], wa_ref)
    z = z_ref[...]
    t = y_ref[...] * (z * jax.nn.sigmoid(z))
    t = t * lax.rsqrt(jnp.mean(t * t, axis=-1, keepdims=True) + LN_EPS) * ng_ref[...]
    y_c = _bdot(t, ws_ref)
    d = D_MODEL
    mix = (jax.nn.sigmoid(gates_ref[:, 0:d]) * y_a + jax.nn.sigmoid(gates_ref[:, d:2 * d]) * y_b
           + jax.nn.sigmoid(gates_ref[:, 2 * d:3 * d]) * y_c)
    out_ref[...] = _ln_rows(DN_ALPHA * x_ref[...] + _bdot(mix, wm_ref), lng_ref[...], lnb_ref[...])


def branch_mix_ln(cv, o, y, proj, x, lncg, lncb, ng, wc, wa, ws, wm, lng, lnb):
    m = x.shape[0]
    tm = min(ROW_BLOCK, m)
    assert m % tm == 0
    d = D_MODEL

    def rows(width, col_block=0):
        return pl.BlockSpec((tm, width), lambda i, cb=col_block: (i, cb))

    def whole(a):
        return pl.BlockSpec(a.shape, lambda i: (0,) * a.ndim, pipeline_mode=pl.Buffered(1))

    vecs = [a.reshape(1, -1) for a in (lncg, lncb, ng)]
    lnv = [a.reshape(1, -1) for a in (lng, lnb)]
    return pl.pallas_call(
        _branch_mix_kernel,
        grid=(m // tm,),
        in_specs=[rows(CONV_DIM), rows(N_HEADS * HEAD_DIM), rows(SSM_INNER), rows(SSM_INNER, COL_Z // SSM_INNER),
                  rows(3 * d, COL_GATES // (3 * d)), rows(d)]
                 + [whole(a) for a in vecs] + [whole(a) for a in (wc, wa, ws, wm)] + [whole(a) for a in lnv],
        out_specs=rows(d),
        out_shape=jax.ShapeDtypeStruct((m, d), jnp.float32),
        compiler_params=pltpu.CompilerParams(dimension_semantics=("parallel",), vmem_limit_bytes=VMEM_LIMIT_BYTES),
        name="branch_mix_ln",
    )(cv, o, y, proj, proj, x, *vecs, wc, wa, ws, wm, *lnv)


def _mem_attn_kernel(x_ref, mk_ref, mv_ref, wq_ref, wo_ref, lng_ref, lnb_ref, out_ref, o_scr, *, seqs, t_seq):
    bf16 = jnp.bfloat16
    x = x_ref[...]
    q = _bdot(x, wq_ref)
    scale = 1.0 / math.sqrt(MEM_HEAD_DIM)
    for s in range(seqs):
        qs = q[s * t_seq:(s + 1) * t_seq].astype(bf16)
        for h in range(MEM_HEADS):
            cols = slice(h * MEM_HEAD_DIM, (h + 1) * MEM_HEAD_DIM)
            sc = _nt_dot(qs[:, cols], mk_ref[s, :, h, :].astype(bf16)) * scale
            p = jnp.exp(sc - jnp.max(sc, axis=-1, keepdims=True))
            p = p / jnp.sum(p, axis=-1, keepdims=True)
            o_scr[s * t_seq:(s + 1) * t_seq, cols] = jnp.dot(p.astype(bf16), mv_ref[s, :, h, :].astype(bf16),
                                                              preferred_element_type=jnp.float32)
    out_ref[...] = _ln_rows(DN_ALPHA * x + _bdot(o_scr[...], wo_ref), lng_ref[...], lnb_ref[...])


def mem_attn_ln(x, mk, mv, wq, wo, lng, lnb, t_seq, layer=None):
    m, d = x.shape
    hd = MEM_HEADS * MEM_HEAD_DIM
    if t_seq >= ROW_BLOCK:
        seqs, tm = 1, ROW_BLOCK
        assert t_seq % tm == 0
        per_seq = t_seq // tm
        seq_of = lambda i: i // per_seq
    else:
        seqs = max(1, 32 // t_seq)
        tm = seqs * t_seq
        assert m % tm == 0
        seq_of = lambda i: i
    if layer is None:
        mem_spec = pl.BlockSpec((seqs, N_MEM, MEM_HEADS, MEM_HEAD_DIM), lambda i: (seq_of(i), 0, 0, 0))
    else:
        mem_spec = pl.BlockSpec((seqs, None, N_MEM, MEM_HEADS, MEM_HEAD_DIM), lambda i: (seq_of(i), layer, 0, 0, 0))

    def whole(a):
        return pl.BlockSpec(a.shape, lambda i: (0,) * a.ndim, pipeline_mode=pl.Buffered(1))

    lnv = [a.reshape(1, -1) for a in (lng, lnb)]
    return pl.pallas_call(
        functools.partial(_mem_attn_kernel, seqs=seqs, t_seq=min(t_seq, tm)),
        grid=(m // tm,),
        in_specs=[pl.BlockSpec((tm, d), lambda i: (i, 0)), mem_spec, mem_spec, whole(wq), whole(wo)]
                 + [whole(a) for a in lnv],
        out_specs=pl.BlockSpec((tm, d), lambda i: (i, 0)),
        out_shape=jax.ShapeDtypeStruct((m, d), jnp.float32),
        scratch_shapes=[pltpu.VMEM((tm, hd), jnp.float32)],
        compiler_params=pltpu.CompilerParams(dimension_semantics=("parallel",), vmem_limit_bytes=VMEM_LIMIT_BYTES),
        name="mem_attn_ln",
    )(x, mk, mv, wq, wo, *lnv)


MOE_ROW_BLOCK = 1024


def _moe_kernel(x_ref, wr_hi_ref, wr_lo_ref, br_ref, wg_ref, wu_ref, wd_ref, lng_ref, lnb_ref, out_ref,
                xb_scr, comb_scr, acc_scr):
    f32 = jnp.float32
    e = pl.program_id(1)
    tm = x_ref.shape[0]
    lane = lax.broadcasted_iota(jnp.int32, (tm, LANES), 1).astype(f32)

    @pl.when(e == 0)
    def _route():
        x_hi, x_lo = _split_bf16(x_ref[...])
        xb_scr[...] = x_hi
        lg = (jnp.dot(x_hi, wr_hi_ref[...], preferred_element_type=f32)
              + jnp.dot(x_lo, wr_hi_ref[...], preferred_element_type=f32)
              + jnp.dot(x_hi, wr_lo_ref[...], preferred_element_type=f32) + br_ref[...])
        is_g = lane < N_GROUPS
        mg = jnp.max(jnp.where(is_g, lg, -jnp.inf), axis=1, keepdims=True)
        g_prob = 1.0 / jnp.sum(jnp.where(is_g, jnp.exp(lg - mg), 0.0), axis=1, keepdims=True)
        g_idx = jnp.min(jnp.where(is_g & (lg == mg), lane, float(LANES)), axis=1, keepdims=True)
        lo_e = N_GROUPS + EXPERTS_PER_GROUP * g_idx
        is_e = (lane >= lo_e) & (lane < lo_e + EXPERTS_PER_GROUP)
        me = jnp.max(jnp.where(is_e, lg, -jnp.inf), axis=1, keepdims=True)
        ee = jnp.where(is_e, jnp.exp(lg - me), 0.0)
        pe = jnp.where(is_e, ee / jnp.sum(ee, axis=1, keepdims=True), -1.0)
        p1 = jnp.max(pe, axis=1, keepdims=True)
        first = jnp.min(jnp.where(pe == p1, lane, float(LANES)), axis=1, keepdims=True)
        pe2 = jnp.where(lane == first, -1.0, pe)
        p2 = jnp.max(pe2, axis=1, keepdims=True)
        second = jnp.min(jnp.where(pe2 == p2, lane, float(LANES)), axis=1, keepdims=True)
        norm = g_prob / (p1 + p2)
        comb_scr[...] = jnp.where(lane == first, p1 * norm, jnp.where(lane == second, p2 * norm, 0.0))
        acc_scr[...] = jnp.zeros(acc_scr.shape, f32)

    xb = xb_scr[...]
    hg = jnp.dot(xb, wg_ref[0], preferred_element_type=f32)
    hu = jnp.dot(xb, wu_ref[0], preferred_element_type=f32)
    c = jnp.sum(jnp.where(lane == (e + N_GROUPS).astype(f32), comb_scr[...], 0.0), axis=1, keepdims=True)
    hid = hg * jax.nn.sigmoid(hg) * hu * c
    acc_scr[...] += jnp.dot(hid.astype(jnp.bfloat16), wd_ref[0], preferred_element_type=f32)

    @pl.when(e == pl.num_programs(1) - 1)
    def _finish():
        out_ref[...] = _ln_rows(DN_ALPHA * x_ref[...] + acc_scr[...], lng_ref[...], lnb_ref[...])


def moe_ln(x, w_group, b_group, w_router, b_router, wg, wu, wd, lng, lnb):
    m, d = x.shape
    tm = min(MOE_ROW_BLOCK, m)
    assert m % tm == 0 and N_GROUPS + N_EXPERTS <= LANES
    pad = jnp.zeros((d, LANES - N_GROUPS - N_EXPERTS), jnp.float32)
    wr_hi, wr_lo = _split_bf16(jnp.concatenate([w_group, w_router, pad], axis=1))
    br = jnp.concatenate([b_group, b_router, pad[0]]).reshape(1, LANES)
    lnv = [a.reshape(1, -1) for a in (lng, lnb)]

    def whole(a):
        return pl.BlockSpec(a.shape, lambda i, e: (0,) * a.ndim, pipeline_mode=pl.Buffered(1))

    return pl.pallas_call(
        _moe_kernel,
        grid=(m // tm, N_EXPERTS),
        in_specs=[pl.BlockSpec((tm, d), lambda i, e: (i, 0)), whole(wr_hi), whole(wr_lo), whole(br),
                  pl.BlockSpec((1, d, D_EXPERT), lambda i, e: (e, 0, 0)),
                  pl.BlockSpec((1, d, D_EXPERT), lambda i, e: (e, 0, 0)),
                  pl.BlockSpec((1, D_EXPERT, d), lambda i, e: (e, 0, 0))] + [whole(a) for a in lnv],
        out_specs=pl.BlockSpec((tm, d), lambda i, e: (i, 0)),
        out_shape=jax.ShapeDtypeStruct((m, d), jnp.float32),
        scratch_shapes=[pltpu.VMEM((tm, d), jnp.bfloat16), pltpu.VMEM((tm, LANES), jnp.float32),
                        pltpu.VMEM((tm, d), jnp.float32)],
        compiler_params=pltpu.CompilerParams(dimension_semantics=("parallel", "arbitrary"),
                                             vmem_limit_bytes=VMEM_LIMIT_BYTES),
        name="moe_ln",
    )(x, wr_hi, wr_lo, br, wg, wu, wd, *lnv)


CONV_HIST_ROWS = 32
CONV_ROW_BLOCK = 128
CONV_ROW_TILE = 32
SUBLANES = 8


def _conv_glu_kernel(x_ref, hist_ref, w_ref, cv_ref, st_ref, xp_scr, z_scr):
    f32 = jnp.float32
    ti = pl.program_id(1)
    tb = x_ref.shape[1]
    n_hist = CONV_WIDTH - 1
    base = CONV_HIST_ROWS - n_hist

    @pl.when(ti == 0)
    def _init():
        xp_scr[0:CONV_HIST_ROWS, :] = jnp.zeros((CONV_HIST_ROWS, CONV_DIM), f32)
        xp_scr[base:CONV_HIST_ROWS, :] = hist_ref[0]

    xin = x_ref[0]
    xp_scr[CONV_HIST_ROWS:CONV_HIST_ROWS + tb, :] = xin[:, :CONV_DIM] * jax.nn.sigmoid(xin[:, CONV_DIM:])
    if tb % SUBLANES == 0:
        rows_z = tb + SUBLANES
        xp_scr[CONV_HIST_ROWS + tb:CONV_HIST_ROWS + tb + 2 * SUBLANES, :] = jnp.zeros((2 * SUBLANES, CONV_DIM), f32)
        for b in range(SUBLANES):
            taps = [(a, SUBLANES * a + b - base) for a in range(CONV_HIST_ROWS // SUBLANES + 1)]
            taps = [(a, j) for a, j in taps if 0 <= j < CONV_WIDTH]
            for r0 in range(0, rows_z, CONV_ROW_TILE):
                nr = min(CONV_ROW_TILE, rows_z - r0)
                acc = None
                for a, j in taps:
                    term = w_ref[j:j + 1, :] * xp_scr[SUBLANES * a + r0:SUBLANES * a + r0 + nr, :]
                    acc = term if acc is None else acc + term
                z_scr[b, r0:r0 + nr, :] = acc
        out = z_scr[0, 0:tb, :]
        for b in range(1, SUBLANES):
            out = out + z_scr[b, b:b + tb, :]
    else:
        out = w_ref[0:1, :] * xp_scr[base:base + tb, :]
        for j in range(1, CONV_WIDTH):
            out = out + w_ref[j:j + 1, :] * xp_scr[base + j:base + j + tb, :]
    cv_ref[0] = out
    st_ref[0] = xp_scr[base + tb:base + tb + n_hist, :]
    xp_scr[base:CONV_HIST_ROWS, :] = xp_scr[base + tb:CONV_HIST_ROWS + tb, :]


def conv_glu(proj, hist, w):
    b, t, _ = proj.shape
    tb = CONV_ROW_BLOCK if t % CONV_ROW_BLOCK == 0 else t
    n_hist = CONV_WIDTH - 1
    assert tb >= n_hist or t == tb
    return pl.pallas_call(
        _conv_glu_kernel,
        grid=(b, t // tb),
        in_specs=[pl.BlockSpec((1, tb, 2 * CONV_DIM), lambda bi, ti: (bi, ti, COL_GLU // (2 * CONV_DIM))),
                  pl.BlockSpec((1, n_hist, CONV_DIM), lambda bi, ti: (bi, 0, 0)),
                  pl.BlockSpec(w.shape, lambda bi, ti: (0, 0))],
        out_specs=[pl.BlockSpec((1, tb, CONV_DIM), lambda bi, ti: (bi, ti, 0)),
                   pl.BlockSpec((1, n_hist, CONV_DIM), lambda bi, ti: (bi, 0, 0))],
        out_shape=[jax.ShapeDtypeStruct((b, t, CONV_DIM), jnp.float32),
                   jax.ShapeDtypeStruct((b, n_hist, CONV_DIM), jnp.float32)],
        scratch_shapes=[pltpu.VMEM((CONV_HIST_ROWS + tb + 2 * SUBLANES, CONV_DIM), jnp.float32),
                        pltpu.VMEM((SUBLANES, tb + SUBLANES, CONV_DIM), jnp.float32)],
        compiler_params=pltpu.CompilerParams(dimension_semantics=("parallel", "arbitrary"),
                                             vmem_limit_bytes=VMEM_LIMIT_BYTES),
        name="conv_glu",
    )(proj, hist, w)


SSD_HIST_ROWS = 8


def _ssd_kernel(xbc_ref, dt_ref, hist_ref, cw_ref, cb_ref, dtb_ref, alog_ref, dexp_ref, h0_ref,
                y_ref, hT_ref, xp_scr, ht_scr, xe_scr):
    f32, i32, bf16 = jnp.float32, jnp.int32, jnp.bfloat16
    c = pl.program_id(1)
    L = xbc_ref.shape[1]
    n_hist = SSM_CONV - 1
    base = SSD_HIST_ROWS - n_hist
    gw = SSM_STATE
    heads_per_group = SSM_HEADS // SSM_GROUPS
    pairs_per_group = heads_per_group // 2

    @pl.when(c == 0)
    def _init():
        xp_scr[base:SSD_HIST_ROWS, :] = hist_ref[0]
        ht_scr[...] = h0_ref[0]

    xp_scr[SSD_HIST_ROWS:SSD_HIST_ROWS + L, :] = xbc_ref[0]
    acc = cw_ref[0:1, :] * xp_scr[base:base + L, :]
    for j in range(1, SSM_CONV):
        acc = acc + cw_ref[j:j + 1, :] * xp_scr[base + j:base + j + L, :]
    acc = acc + cb_ref[...]
    xbc = acc * jax.nn.sigmoid(acc)
    xp_scr[base:SSD_HIST_ROWS, :] = xp_scr[base + L:SSD_HIST_ROWS + L, :]
    xs = xbc[:, 0:SSM_INNER]
    bm = xbc[:, SSM_INNER:SSM_INNER + SSM_GROUPS * gw]
    cm = xbc[:, SSM_INNER + SSM_GROUPS * gw:SSM_INNER + 2 * SSM_GROUPS * gw]

    x_dt = dt_ref[0] + dtb_ref[...]
    dtf = jnp.maximum(x_dt, 0.0) + jnp.log1p(jnp.exp(-jnp.abs(x_dt)))
    la = dtf * (-jnp.exp(alog_ref[...]))
    row = lax.broadcasted_iota(i32, (L, L), 0)
    colk = lax.broadcasted_iota(i32, (L, L), 1)
    causal = colk <= row
    tril = jnp.where(causal, 1.0, 0.0).astype(bf16)
    la_hi = la.astype(bf16)
    r1 = la - la_hi.astype(f32)
    la_mid = r1.astype(bf16)
    la_lo = (r1 - la_mid.astype(f32)).astype(bf16)
    cs = (jnp.dot(tril, la_hi, preferred_element_type=f32) + jnp.dot(tril, la_mid, preferred_element_type=f32)
          + jnp.dot(tril, la_lo, preferred_element_type=f32))
    cs_t = cs.T
    ecs = jnp.exp(cs)
    to_end = jnp.exp(cs[L - 1:L, :] - cs)

    lane = lax.broadcasted_iota(i32, (L, LANES), 1)
    half0 = lane < SSM_HEAD_DIM

    def pair_cols(a, h0):
        return jnp.where(half0, a[:, h0:h0 + 1], a[:, h0 + 1:h0 + 2])

    for g in range(SSM_GROUPS):
        b_g = bm[:, g * gw:(g + 1) * gw]
        c_g = cm[:, g * gw:(g + 1) * gw].astype(bf16)
        cb = _nt_dot(c_g, b_g.astype(bf16))
        cols = slice(g * heads_per_group * SSM_HEAD_DIM, (g + 1) * heads_per_group * SSM_HEAD_DIM)
        y_off = jnp.dot(c_g, ht_scr[:, cols].astype(bf16), preferred_element_type=f32)
        for i in range(pairs_per_group):
            h0 = g * heads_per_group + 2 * i
            pc = slice((h0 // 2) * LANES, (h0 // 2 + 1) * LANES)
            xs_p = xs[:, pc]
            xdt_p = xs_p * pair_cols(dtf, h0)
            att = []
            for hh in range(2):
                seg = cs[:, h0 + hh:h0 + hh + 1] - cs_t[h0 + hh:h0 + hh + 1, :]
                att.append((cb * jnp.exp(jnp.where(causal, seg, -jnp.inf))).astype(bf16))
            rhs = jnp.concatenate([jnp.where(half0, xdt_p, 0.0), jnp.where(half0, 0.0, xdt_p)], axis=0).astype(bf16)
            y_diag = jnp.dot(jnp.concatenate(att, axis=1), rhs, preferred_element_type=f32)
            xe_scr[:, pc] = xdt_p * pair_cols(to_end, h0)
            y_ref[0, :, pc] = (y_diag + y_off[:, i * LANES:(i + 1) * LANES] * pair_cols(ecs, h0)
                               + dexp_ref[:, pc] * xs_p)
        bt_hi, bt_lo = _split_bf16(b_g.T)
        xe_hi, xe_lo = _split_bf16(xe_scr[:, cols])
        st = (jnp.dot(bt_hi, xe_hi, preferred_element_type=f32) + jnp.dot(bt_lo, xe_hi, preferred_element_type=f32)
              + jnp.dot(bt_hi, xe_lo, preferred_element_type=f32))
        for i in range(pairs_per_group):
            h0 = g * heads_per_group + 2 * i
            pc = slice((h0 // 2) * LANES, (h0 // 2 + 1) * LANES)
            decay = jnp.where(half0[0:1, :], ecs[L - 1:L, h0:h0 + 1], ecs[L - 1:L, h0 + 1:h0 + 2])
            ht_scr[:, pc] = ht_scr[:, pc] * decay + st[:, i * LANES:(i + 1) * LANES]

    @pl.when(c == pl.num_programs(1) - 1)
    def _done():
        hT_ref[0] = ht_scr[...]


def ssd_prompt(proj, hist, conv_w, conv_b, dt_bias, a_log, d_skip, h0_t):
    b, t, _ = proj.shape
    L = SSM_CHUNK
    assert t % L == 0 and SSM_HEAD_DIM * 2 == LANES and SSM_STATE == LANES and SSM_HEADS <= LANES

    def lane_pad(a):
        return jnp.concatenate([a, jnp.zeros((LANES - a.shape[0],), a.dtype)]).reshape(1, LANES)

    def whole(a):
        return pl.BlockSpec(a.shape, lambda bi, c: (0,) * a.ndim)

    params = [conv_w, conv_b.reshape(1, -1), lane_pad(dt_bias), lane_pad(a_log),
              jnp.repeat(d_skip, SSM_HEAD_DIM).reshape(1, SSM_INNER)]
    return pl.pallas_call(
        _ssd_kernel,
        grid=(b, t // L),
        in_specs=[pl.BlockSpec((1, L, SSM_XBC), lambda bi, c: (bi, c, COL_XBC // SSM_XBC)),
                  pl.BlockSpec((1, L, LANES), lambda bi, c: (bi, c, COL_DT // LANES)),
                  pl.BlockSpec((1, SSM_CONV - 1, SSM_XBC), lambda bi, c: (bi, 0, 0))]
                 + [whole(a) for a in params]
                 + [pl.BlockSpec((1, SSM_STATE, SSM_INNER), lambda bi, c: (bi, 0, 0))],
        out_specs=[pl.BlockSpec((1, L, SSM_INNER), lambda bi, c: (bi, c, 0)),
                   pl.BlockSpec((1, SSM_STATE, SSM_INNER), lambda bi, c: (bi, 0, 0))],
        out_shape=[jax.ShapeDtypeStruct((b, t, SSM_INNER), jnp.float32),
                   jax.ShapeDtypeStruct((b, SSM_STATE, SSM_INNER), jnp.float32)],
        scratch_shapes=[pltpu.VMEM((SSD_HIST_ROWS + L, SSM_XBC), jnp.float32),
                        pltpu.VMEM((SSM_STATE, SSM_INNER), jnp.float32),
                        pltpu.VMEM((L, SSM_INNER), jnp.float32)],
        compiler_params=pltpu.CompilerParams(dimension_semantics=("parallel", "arbitrary"),
                                             vmem_limit_bytes=VMEM_LIMIT_BYTES),
        name="ssd_prompt",
    )(proj, proj, hist, *params, h0_t)


def causal_dwconv(x_pad, w):
    return lax.conv_general_dilated(x_pad, w[:, None, :], window_strides=(1,), padding='VALID',
                                    dimension_numbers=('NWC', 'WIO', 'NWC'),
                                    feature_group_count=x_pad.shape[-1])


def ssd_scan(x, dt, a, bm, cm, h0):
    f32 = jnp.float32
    bsz, l, nh, hp = x.shape
    rep = nh // bm.shape[2]
    chunk = min(SSM_CHUNK, l)
    assert l % chunk == 0
    xdt = x.astype(f32) * dt[..., None]
    la = dt * a
    bh = jnp.repeat(bm.astype(f32), rep, axis=2)
    ch = jnp.repeat(cm.astype(f32), rep, axis=2)
    nc = l // chunk
    ns = bh.shape[-1]
    xdt = xdt.reshape(bsz, nc, chunk, nh, hp)
    la = la.reshape(bsz, nc, chunk, nh)
    bh = bh.reshape(bsz, nc, chunk, nh, ns)
    ch = ch.reshape(bsz, nc, chunk, nh, ns)
    cs = jnp.cumsum(la, axis=2)
    causal = jnp.tril(jnp.ones((chunk, chunk), bool))
    seg = cs[:, :, :, None, :] - cs[:, :, None, :, :]
    decay = jnp.exp(jnp.where(causal[None, None, :, :, None], seg, -jnp.inf))
    att = jnp.einsum('bcqhn,bckhn->bcqkh', ch, bh) * decay
    y_diag = jnp.einsum('bcqkh,bckhp->bcqhp', att, xdt)
    to_end = jnp.exp(cs[:, :, -1:, :] - cs)
    states = jnp.einsum('bckhn,bckh,bckhp->bchpn', bh, to_end, xdt)
    chunk_decay = jnp.exp(cs[:, :, -1, :])

    def step(hc, inp):
        dc, st = inp
        return hc * dc[:, :, None, None] + st, hc

    h_last, h_prev = lax.scan(step, h0.astype(f32), (chunk_decay.swapaxes(0, 1), states.swapaxes(0, 1)))
    h_prev = h_prev.swapaxes(0, 1)
    y_off = jnp.einsum('bcqhn,bchpn,bcqh->bcqhp', ch, h_prev, jnp.exp(cs))
    y = (y_diag + y_off).reshape(bsz, nc * chunk, nh, hp)
    return y, h_last


def token_mixer(x, p, conv_buf, ssm_buf, ssm_h0, past):
    (w_in, conv_dw, ln_conv_g, ln_conv_b, w_conv_out, w_attn_out, ssm_conv_w, ssm_conv_b,
     ssm_dt_bias, ssm_a_log, ssm_d, ssm_norm_g, w_ssm_out, w_mix_out, ln_mix_g, ln_mix_b) = p
    b, t, _ = x.shape
    proj = mm_nt(x.reshape(b * t, D_MODEL), w_in).reshape(b, t, D_IN_PAD)
    k = proj[..., COL_K:COL_K + LANES]
    v = proj[..., COL_V:COL_V + LANES]
    small = proj[..., COL_SMALL:COL_SMALL + LANES]
    ki = small[..., 0:IDX_DIM]
    dt = small[..., IDX_DIM + IDX_HEADS:IDX_DIM + IDX_HEADS + SSM_HEADS]
    xbc = proj[..., COL_XBC:COL_XBC + SSM_XBC]
    cv, conv_state = conv_glu(proj, conv_buf, conv_dw)
    if past is None:
        o = dsa_prompt(proj, k, v, ki)
    else:
        cache_k, cache_v, cache_ik, page_table, layer = past
        o = dsa_decode(proj, k, v, cache_k, cache_v, cache_ik, page_table, layer)
    if past is None:
        h0_t = ssm_h0.reshape(b, SSM_INNER, SSM_STATE).swapaxes(1, 2)
        y, h_t = ssd_prompt(proj, ssm_buf, ssm_conv_w, ssm_conv_b, ssm_dt_bias, ssm_a_log, ssm_d, h0_t)
        h_last = h_t.swapaxes(1, 2).reshape(b, SSM_HEADS, SSM_HEAD_DIM, SSM_STATE)
        xbc_pad = xbc[:, -(SSM_CONV - 1):]
    else:
        xbc_pad = jnp.concatenate([ssm_buf, xbc], axis=1)
        xbc_c = jax.nn.silu(causal_dwconv(xbc_pad, ssm_conv_w) + ssm_conv_b)
        xs, bm, cm = jnp.split(xbc_c, [SSM_INNER, SSM_INNER + SSM_GROUPS * SSM_STATE], axis=-1)
        xs = xs.reshape(b, t, SSM_HEADS, SSM_HEAD_DIM)
        dtf = jax.nn.softplus(dt + ssm_dt_bias)
        a = -jnp.exp(ssm_a_log)
        y, h_last = ssd_scan(xs, dtf, a, bm.reshape(b, t, SSM_GROUPS, SSM_STATE),
                             cm.reshape(b, t, SSM_GROUPS, SSM_STATE), ssm_h0)
        y = (y + ssm_d[:, None] * xs).reshape(b, t, SSM_INNER)
    m = b * t
    x_new = branch_mix_ln(cv.reshape(m, CONV_DIM), o.reshape(m, -1), y.reshape(m, SSM_INNER),
                          proj.reshape(m, D_IN_PAD), x.reshape(m, D_MODEL), ln_conv_g, ln_conv_b, ssm_norm_g,
                          w_conv_out, w_attn_out, w_ssm_out, w_mix_out, ln_mix_g, ln_mix_b).reshape(b, t, D_MODEL)
    return (x_new, k.reshape(b, t, N_KV_HEADS, HEAD_DIM), v.reshape(b, t, N_KV_HEADS, HEAD_DIM), ki,
            conv_state, xbc_pad[:, -(SSM_CONV - 1):], h_last)


def _pad_w_in_t(wt):
    sp = (0,) + SPLIT_POINTS + (wt.shape[0],)
    seg = [wt[sp[i]:sp[i + 1]] for i in range(len(SPLIT_SIZES))]
    glu, q, k, v, qi, ki, wi, z, xbc, dt, gates = seg
    pad = jnp.zeros((LANES - IDX_DIM - IDX_HEADS - SSM_HEADS, wt.shape[1]), wt.dtype)
    n_rep = N_HEADS // N_KV_HEADS
    q = q.reshape(N_KV_HEADS, n_rep, HEAD_DIM, -1).swapaxes(0, 1).reshape(q.shape)
    dt_pad = jnp.zeros((LANES - SSM_HEADS, wt.shape[1]), wt.dtype)
    out = jnp.concatenate([gates, xbc, q, glu, z, qi, k, v, ki, wi, dt, pad, dt, dt_pad], axis=0)
    assert out.shape[0] == D_IN_PAD
    return out


def _perm_w_attn_out(w):
    n_rep = N_HEADS // N_KV_HEADS
    return w.reshape(N_KV_HEADS, n_rep, HEAD_DIM, -1).swapaxes(0, 1).reshape(w.shape)


def kernel(x_prompt, x_sample, mem_prompt, cache_k, cache_v, cache_ik, cache_mem_k, cache_mem_v, state_conv, state_ssm_conv, state_ssm, page_table, w_in, conv_dw, ln_conv_g, ln_conv_b, w_conv_out, w_attn_out, ssm_conv_w, ssm_conv_b, ssm_dt_bias, ssm_a_log, ssm_d, ssm_norm_g, w_ssm_out, w_mix_out, ln_mix_g, ln_mix_b, w_mq, w_mk, w_mv, w_mo, ln_mem_g, ln_mem_b, w_group, b_group, w_router, b_router, w_e_gate, w_e_up, w_e_down, ln_ffn_g, ln_ffn_b):
    bf16 = jnp.bfloat16
    bp, tp, _ = x_prompt.shape
    bs, ts, _ = x_sample.shape
    xp, xs = x_prompt, x_sample
    outs_p = [[] for _ in range(8)]
    outs_s = [[] for _ in range(6)]
    for l in range(DEPTH):
        mix_p = (_pad_w_in_t(jnp.transpose(w_in[l])).astype(bf16), conv_dw[l], ln_conv_g[l], ln_conv_b[l],
                 w_conv_out[l].astype(bf16), _perm_w_attn_out(w_attn_out[l]).astype(bf16),
                 ssm_conv_w[l], ssm_conv_b[l], ssm_dt_bias[l], ssm_a_log[l], ssm_d[l], ssm_norm_g[l],
                 w_ssm_out[l].astype(bf16), w_mix_out[l].astype(bf16), ln_mix_g[l], ln_mix_b[l])
        moe_p = (w_group[l], b_group[l], w_router[l], b_router[l],
                 w_e_gate[l].astype(bf16), w_e_up[l].astype(bf16), w_e_down[l].astype(bf16), ln_ffn_g[l], ln_ffn_b[l])
        mem_p = (w_mq[l].astype(bf16), w_mo[l].astype(bf16), ln_mem_g[l], ln_mem_b[l])
        hd_mem = MEM_HEADS * MEM_HEAD_DIM
        xp, kp, vp, kip, cbp, sbp, hp = token_mixer(
            xp, mix_p,
            jnp.zeros((bp, CONV_WIDTH - 1, CONV_DIM), xp.dtype),
            jnp.zeros((bp, SSM_CONV - 1, SSM_XBC), xp.dtype),
            jnp.zeros((bp, SSM_HEADS, SSM_HEAD_DIM, SSM_STATE), jnp.float32),
            None)
        mem_kv = mm(mem_prompt.reshape(bp * N_MEM, D_MODEL),
                    jnp.concatenate([w_mk[l], w_mv[l]], axis=1).astype(bf16)).reshape(bp, N_MEM, 2 * hd_mem)
        mkp = mem_kv[..., :hd_mem].reshape(bp, N_MEM, MEM_HEADS, MEM_HEAD_DIM)
        mvp = mem_kv[..., hd_mem:].reshape(bp, N_MEM, MEM_HEADS, MEM_HEAD_DIM)
        xp = mem_attn_ln(xp.reshape(bp * tp, D_MODEL), mkp, mvp, *mem_p, t_seq=tp)
        xp = moe_ln(xp, *moe_p).reshape(bp, tp, D_MODEL)
        xs, ks_new, vs_new, kis, cbs, sbs, hs = token_mixer(
            xs, mix_p, state_conv[:, l], state_ssm_conv[:, l], state_ssm[:, l],
            (cache_k, cache_v, cache_ik, page_table, l))
        xs = mem_attn_ln(xs.reshape(bs * ts, D_MODEL), cache_mem_k, cache_mem_v, *mem_p, t_seq=ts, layer=l)
        xs = moe_ln(xs, *moe_p).reshape(bs, ts, D_MODEL)
        for lst, arr in zip(outs_p, (kp, vp, kip, mkp, mvp, cbp, sbp, hp)):
            lst.append(arr)
        for lst, arr in zip(outs_s, (ks_new, vs_new, kis, cbs, sbs, hs)):
            lst.append(arr)
    p_k, p_v, p_ik, p_mem_k, p_mem_v, p_conv, p_ssm_conv, p_ssm = [jnp.stack(a, axis=1) for a in outs_p]
    s_k, s_v, s_ik, s_conv, s_ssm_conv, s_ssm = [jnp.stack(a, axis=1) for a in outs_s]
    return (xp, xs, p_k, p_v, p_ik, p_mem_k, p_mem_v, p_conv, p_ssm_conv, p_ssm,
            s_k, s_v, s_ik, s_conv, s_ssm_conv, s_ssm)
```

```python
import functools
import math

import jax
import jax.numpy as jnp
from jax import lax
from jax.experimental import pallas as pl
from jax.experimental.pallas import tpu as pltpu

D_MODEL = 1024
DEPTH = 2
PAGE_SIZE = 128
CONV_DIM = 512
CONV_WIDTH = 31
N_HEADS = 8
N_KV_HEADS = 2
HEAD_DIM = 64
IDX_HEADS = 4
IDX_DIM = 64
TOPK_MAX = 256
Q_BLOCK = 128
SSM_HEADS = 16
SSM_HEAD_DIM = 64
SSM_INNER = SSM_HEADS * SSM_HEAD_DIM
SSM_GROUPS = 2
SSM_STATE = 128
SSM_CONV = 4
SSM_CHUNK = 128
SSM_XBC = SSM_INNER + 2 * SSM_GROUPS * SSM_STATE
N_MEM = 256
MEM_HEADS = 4
MEM_HEAD_DIM = 128
N_GROUPS = 4
EXPERTS_PER_GROUP = 4
N_EXPERTS = N_GROUPS * EXPERTS_PER_GROUP
TOPK_IN_GROUP = 2
D_EXPERT = 512
DN_ALPHA = (2 * DEPTH) ** 0.25
LN_EPS = 1e-5

SPLIT_SIZES = (2 * CONV_DIM, N_HEADS * HEAD_DIM, N_KV_HEADS * HEAD_DIM, N_KV_HEADS * HEAD_DIM,
               IDX_HEADS * IDX_DIM, IDX_DIM, IDX_HEADS, SSM_INNER, SSM_XBC, SSM_HEADS, 3 * D_MODEL)
SPLIT_POINTS = tuple(sum(SPLIT_SIZES[:i + 1]) for i in range(len(SPLIT_SIZES) - 1))

VMEM_LIMIT_BYTES = 56 * 1024 * 1024
MM_COL_CHUNK = 512


def _mm_kernel(x_ref, w_ref, o_ref):
    xb = x_ref[...].astype(jnp.bfloat16)
    n = o_ref.shape[1]
    for c0 in range(0, n, MM_COL_CHUNK):
        c1 = min(n, c0 + MM_COL_CHUNK)
        o_ref[:, c0:c1] = jnp.dot(xb, w_ref[:, c0:c1], preferred_element_type=jnp.float32)


def mm(x, w, tm=256):
    m, k = x.shape
    n = w.shape[1]
    tm = min(tm, m)
    assert m % tm == 0 and n % 128 == 0 and k % 128 == 0
    return pl.pallas_call(
        _mm_kernel,
        grid=(m // tm,),
        in_specs=[pl.BlockSpec((tm, k), lambda i: (i, 0)),
                  pl.BlockSpec((k, n), lambda i: (0, 0), pipeline_mode=pl.Buffered(1))],
        out_specs=pl.BlockSpec((tm, n), lambda i: (i, 0)),
        out_shape=jax.ShapeDtypeStruct((m, n), jnp.float32),
        compiler_params=pltpu.CompilerParams(dimension_semantics=("parallel",),
                                             vmem_limit_bytes=VMEM_LIMIT_BYTES),
    )(x, w)


def _mm_nt_kernel(x_ref, wt_ref, o_ref):
    xb = x_ref[...].astype(jnp.bfloat16)
    n = o_ref.shape[1]
    for c0 in range(0, n, MM_COL_CHUNK):
        c1 = min(n, c0 + MM_COL_CHUNK)
        o_ref[:, c0:c1] = lax.dot_general(xb, wt_ref[c0:c1, :], (((1,), (1,)), ((), ())),
                                          preferred_element_type=jnp.float32)


def mm_nt(x, wt, tm=256):
    m, k = x.shape
    n = wt.shape[0]
    tm = min(tm, m)
    assert m % tm == 0 and n % 128 == 0 and k % 128 == 0
    return pl.pallas_call(
        _mm_nt_kernel,
        grid=(m // tm,),
        in_specs=[pl.BlockSpec((tm, k), lambda i: (i, 0)),
                  pl.BlockSpec((n, k), lambda i: (0, 0), pipeline_mode=pl.Buffered(1))],
        out_specs=pl.BlockSpec((tm, n), lambda i: (i, 0)),
        out_shape=jax.ShapeDtypeStruct((m, n), jnp.float32),
        compiler_params=pltpu.CompilerParams(dimension_semantics=("parallel",),
                                             vmem_limit_bytes=VMEM_LIMIT_BYTES),
        name="in_proj",
    )(x, wt)


INT32_MIN = -2 ** 31
MASK_BIAS = -1e30
DSA_KEY_CHUNK = 512
LANES = 128


def _nt_dot(a, b):
    return lax.dot_general(a, b, (((1,), (1,)), ((), ())), preferred_element_type=jnp.float32)


def _dsa_prompt_kernel(q_ref, qi_ref, sm_ref, k_ref, v_ref, ki_ref, o_ref,
                       key_scr, rank_scr, s_scr, p_scr, qpad_scr, qis_scr, m_scr, l_scr, acc_scr, *, k_sel):
    f32, i32, bf16 = jnp.float32, jnp.int32, jnp.bfloat16
    qb = q_ref.shape[1]
    kc_w = DSA_KEY_CHUNK
    sub = kc_w // LANES
    i = pl.program_id(1)
    n_chunks = (i * qb + qb + kc_w - 1) // kc_w
    lane = lax.broadcasted_iota(i32, (qb, LANES), 1)
    half = [lane < HEAD_DIM, lane >= HEAD_DIM]

    for pair in range(IDX_HEADS // 2):
        src = qi_ref[0, :, pair * LANES:(pair + 1) * LANES]
        hi = src.astype(bf16).astype(f32)
        lo_swapped = pltpu.roll(src - hi, IDX_DIM, axis=1)
        for hh in range(2):
            h = 2 * pair + hh
            qis_scr[h * qb:(h + 1) * qb, 0:LANES] = jnp.where(half[hh], hi, lo_swapped).astype(bf16)
            qis_scr[h * qb:(h + 1) * qb, LANES:2 * LANES] = jnp.where(half[hh], hi, 0.0).astype(bf16)
    w_idx = [sm_ref[0, :, IDX_DIM + h:IDX_DIM + h + 1] for h in range(IDX_HEADS)]
    row_pos = i * qb + lax.broadcasted_iota(i32, (qb, kc_w), 0)
    col_iota = lax.broadcasted_iota(i32, (qb, kc_w), 1)

    def score_body(c, carry):
        off = pl.multiple_of(c * kc_w, kc_w)
        d = _nt_dot(qis_scr[...], ki_ref[0, pl.ds(off, kc_w), :])
        idx = w_idx[0] * jnp.maximum(d[0:qb], 0.0)
        for h in range(1, IDX_HEADS):
            idx = idx + w_idx[h] * jnp.maximum(d[h * qb:(h + 1) * qb], 0.0)
        idx = jnp.where(idx == 0.0, 0.0, idx)
        bits = lax.bitcast_convert_type(idx, i32)
        key = bits ^ ((bits >> 31) & 0x7FFFFFFF)
        key_scr[c] = jnp.where(off + col_iota <= row_pos, key, INT32_MIN)
        return carry

    lax.fori_loop(0, n_chunks, score_body, 0)

    def count(pred):
        def body(c, acc):
            for g in range(sub):
                acc = acc + jnp.where(pred(key_scr[c, :, g * LANES:(g + 1) * LANES]), 1.0, 0.0)
            return acc
        acc = lax.fori_loop(0, n_chunks, body, jnp.zeros((qb, LANES), f32))
        return jnp.sum(acc, axis=1, keepdims=True)

    def search_body(it, lo):
        cand = lo + lax.shift_left(jnp.int32(1), 31 - it)
        cand_b = jnp.broadcast_to(cand, (qb, LANES))
        cnt = count(lambda kk: kk >= cand_b)
        return jnp.where(cnt >= float(k_sel), cand, lo)

    thr = lax.fori_loop(0, 32, search_body, jnp.full((qb, 1), INT32_MIN, i32))
    thr_b = jnp.broadcast_to(thr, (qb, LANES))
    need_b = jnp.broadcast_to(float(k_sel) - count(lambda kk: kk > thr_b), (qb, LANES))
    thr_w = jnp.broadcast_to(thr, (qb, kc_w))
    need_w = jnp.tile(need_b, (1, sub))

    r_i = lax.broadcasted_iota(i32, (kc_w, kc_w + LANES), 0)
    c_i = lax.broadcasted_iota(i32, (kc_w, kc_w + LANES), 1)
    rank_scr[...] = jnp.where((r_i <= c_i) | (c_i >= kc_w), 1.0, 0.0).astype(bf16)
    zero_bits = jnp.zeros((qb, kc_w), i32)
    mask_bits = lax.bitcast_convert_type(jnp.full((qb, kc_w), MASK_BIAS, f32), i32)

    def select_body(c, seen):
        kk = key_scr[c]
        eq = kk == thr_w
        pr = jnp.dot(jnp.where(eq, 1.0, 0.0).astype(bf16), rank_scr[...], preferred_element_type=f32)
        rank = jnp.tile(seen, (1, sub)) + pr[:, 0:kc_w]
        sel = ((kk > thr_w) | (eq & (rank <= need_w))) & (kk > INT32_MIN)
        key_scr[c] = jnp.where(sel, zero_bits, mask_bits)
        return seen + pr[:, kc_w:kc_w + LANES]

    lax.fori_loop(0, n_chunks, select_body, jnp.zeros((qb, LANES), f32))

    n_blk = N_HEADS
    scale = 1.0 / math.sqrt(HEAD_DIM)
    for g in range(N_HEADS // N_KV_HEADS):
        src = q_ref[0, :, g * LANES:(g + 1) * LANES] * scale
        for j in range(N_KV_HEADS):
            r = g * N_KV_HEADS + j
            qpad_scr[r * qb:(r + 1) * qb, :] = jnp.where(half[j], src, 0.0).astype(bf16)
    m_scr[...] = jnp.full(m_scr.shape, -jnp.inf, f32)
    l_scr[...] = jnp.zeros(l_scr.shape, f32)
    acc_scr[...] = jnp.zeros(acc_scr.shape, f32)

    def attend_body(c, carry):
        off = pl.multiple_of(c * kc_w, kc_w)
        k_c = k_ref[0, pl.ds(off, kc_w), :]
        v_c = v_ref[0, pl.ds(off, kc_w), :]
        bias = lax.bitcast_convert_type(key_scr[c], f32)
        s_scr[...] = _nt_dot(qpad_scr[...], k_c)
        for r in range(n_blk):
            rows = slice(r * qb, (r + 1) * qb)
            s = s_scr[rows, :] + bias
            m_old = m_scr[rows, :]
            m_new = jnp.maximum(m_old, jnp.broadcast_to(jnp.max(s, axis=1, keepdims=True), (qb, LANES)))
            alpha = jnp.exp(m_old - m_new)
            p = jnp.exp(s - jnp.tile(m_new, (1, sub)))
            l_scr[rows, :] = alpha * l_scr[rows, :] + jnp.broadcast_to(jnp.sum(p, axis=1, keepdims=True), (qb, LANES))
            acc_scr[rows, :] = alpha * acc_scr[rows, :]
            p_scr[rows, :] = p.astype(bf16)
            m_scr[rows, :] = m_new
        acc_scr[...] += jnp.dot(p_scr[...], v_c, preferred_element_type=f32)
        return carry

    lax.fori_loop(0, n_chunks, attend_body, 0)

    for g in range(N_HEADS // N_KV_HEADS):
        outs = []
        for j in range(N_KV_HEADS):
            rows = slice((g * N_KV_HEADS + j) * qb, (g * N_KV_HEADS + j + 1) * qb)
            outs.append(acc_scr[rows, :] / l_scr[rows, :])
        o_ref[0, :, g * LANES:(g + 1) * LANES] = jnp.where(half[0], outs[0], outs[1])


def _split_bf16(x):
    hi = x.astype(jnp.bfloat16)
    return hi, (x - hi.astype(jnp.float32)).astype(jnp.bfloat16)


def dsa_prompt(proj, k, v, ki):
    b, t, _ = proj.shape
    qb = Q_BLOCK
    w_q, w_qi = N_HEADS * HEAD_DIM, IDX_HEADS * IDX_DIM
    assert t % DSA_KEY_CHUNK == 0 and t % qb == 0
    assert IDX_DIM == HEAD_DIM == LANES // 2 and IDX_HEADS % 2 == 0
    k_sel = min(TOPK_MAX, t // 4)
    n_rows = N_HEADS * qb
    ki_hi, ki_lo = _split_bf16(ki)
    ki4 = jnp.concatenate([ki_hi, ki_hi, ki_lo, ki_lo], axis=-1)
    k, v = k.astype(jnp.bfloat16), v.astype(jnp.bfloat16)
    seq_spec = pl.BlockSpec((1, t, LANES), lambda bi, i: (bi, 0, 0))
    return pl.pallas_call(
        functools.partial(_dsa_prompt_kernel, k_sel=k_sel),
        grid=(b, t // qb),
        in_specs=[pl.BlockSpec((1, qb, w_q), lambda bi, i: (bi, i, COL_Q // w_q)),
                  pl.BlockSpec((1, qb, w_qi), lambda bi, i: (bi, i, COL_QI // w_qi)),
                  pl.BlockSpec((1, qb, LANES), lambda bi, i: (bi, i, COL_SMALL // LANES)),
                  seq_spec, seq_spec, pl.BlockSpec((1, t, 2 * LANES), lambda bi, i: (bi, 0, 0))],
        out_specs=pl.BlockSpec((1, qb, N_HEADS * HEAD_DIM), lambda bi, i: (bi, i, 0)),
        out_shape=jax.ShapeDtypeStruct((b, t, N_HEADS * HEAD_DIM), jnp.float32),
        scratch_shapes=[pltpu.VMEM((t // DSA_KEY_CHUNK, qb, DSA_KEY_CHUNK), jnp.int32),
                        pltpu.VMEM((DSA_KEY_CHUNK, DSA_KEY_CHUNK + LANES), jnp.bfloat16),
                        pltpu.VMEM((n_rows, DSA_KEY_CHUNK), jnp.float32),
                        pltpu.VMEM((n_rows, DSA_KEY_CHUNK), jnp.bfloat16),
                        pltpu.VMEM((n_rows, LANES), jnp.bfloat16),
                        pltpu.VMEM((IDX_HEADS * qb, 2 * LANES), jnp.bfloat16),
                        pltpu.VMEM((n_rows, LANES), jnp.float32),
                        pltpu.VMEM((n_rows, LANES), jnp.float32),
                        pltpu.VMEM((n_rows, LANES), jnp.float32)],
        compiler_params=pltpu.CompilerParams(dimension_semantics=("parallel", "arbitrary"),
                                             vmem_limit_bytes=VMEM_LIMIT_BYTES),
        name="dsa_prompt",
    )(proj, proj, proj, k, v, ki4)


DEC_ROWS = 8


def _dsa_decode_kernel(pt_ref, q_ref, qi_ref, sm_ref, kn_ref, vn_ref, *rest, n_pages, k_sel):
    f32, i32, bf16 = jnp.float32, jnp.int32, jnp.bfloat16
    k_pages = rest[0:n_pages]
    v_pages = rest[n_pages:2 * n_pages]
    ik_pages = rest[2 * n_pages:3 * n_pages]
    o_ref = rest[3 * n_pages]
    kall, vall, ikall, new_scr, qis, wpad, qpad, key_scr, p_scr = rest[3 * n_pages + 1:]
    t_new = q_ref.shape[1]
    past = n_pages * PAGE_SIZE
    s_pad = past + PAGE_SIZE
    n_chunks = s_pad // LANES
    rows = DEC_ROWS

    for p in range(n_pages):
        sl = slice(p * PAGE_SIZE, (p + 1) * PAGE_SIZE)
        for j in range(N_KV_HEADS):
            kall[j * HEAD_DIM:(j + 1) * HEAD_DIM, sl] = k_pages[p][j]
            vall[j * HEAD_DIM:(j + 1) * HEAD_DIM, sl] = v_pages[p][j]
        ikall[:, sl] = ik_pages[p][...]
    tail = slice(past, s_pad)
    for src, dst, n_feat in ((kn_ref[0], kall, LANES), (vn_ref[0], vall, LANES),
                             (sm_ref[0], ikall, IDX_DIM)):
        new_scr[...] = jnp.zeros(new_scr.shape, f32)
        new_scr[0:t_new, :] = src
        dst[:, tail] = new_scr[...].T[0:n_feat, :]

    qis[...] = jnp.zeros(qis.shape, f32)
    wpad[...] = jnp.zeros(wpad.shape, f32)
    for h in range(IDX_HEADS):
        qis[h * rows:h * rows + t_new, :] = qi_ref[0, :, h * IDX_DIM:(h + 1) * IDX_DIM]
    wpad[0:t_new, :] = sm_ref[0]
    q_hi, q_lo = _split_bf16(qis[...])
    k_hi, k_lo = _split_bf16(ikall[...])

    def mm_f32(a, b):
        return jnp.dot(a, b, preferred_element_type=f32)

    d = mm_f32(q_hi, k_hi) + mm_f32(q_lo, k_hi) + mm_f32(q_hi, k_lo)
    idx = wpad[:, IDX_DIM:IDX_DIM + 1] * jnp.maximum(d[0:rows], 0.0)
    for h in range(1, IDX_HEADS):
        idx = idx + wpad[:, IDX_DIM + h:IDX_DIM + h + 1] * jnp.maximum(d[h * rows:(h + 1) * rows], 0.0)
    idx = jnp.where(idx == 0.0, 0.0, idx)
    bits = lax.bitcast_convert_type(idx, i32)
    key = bits ^ ((bits >> 31) & 0x7FFFFFFF)
    col = lax.broadcasted_iota(i32, (rows, s_pad), 1)
    q_pos = past + lax.broadcasted_iota(i32, (rows, s_pad), 0)
    key_scr[...] = jnp.where((col <= q_pos) & (col < past + t_new), key, INT32_MIN)

    def search_body(it, lo):
        step = lax.shift_left(jnp.int32(1), 30 - 2 * it)
        keys = key_scr[...]
        for mult in (1, 2, 3):
            cand = lo + step * mult
            cnt = jnp.sum(jnp.where(keys >= cand, 1.0, 0.0), axis=1, keepdims=True)
            best = jnp.where(cnt >= float(k_sel), cand, lo if mult == 1 else best)
        return best

    thr = lax.fori_loop(0, 16, search_body, jnp.full((rows, 1), INT32_MIN, i32))
    need = float(k_sel) - jnp.sum(jnp.where(key_scr[...] > thr, 1.0, 0.0), axis=1, keepdims=True)
    tri = jnp.where(lax.broadcasted_iota(i32, (LANES, LANES), 0) <= lax.broadcasted_iota(i32, (LANES, LANES), 1),
                    1.0, 0.0).astype(bf16)
    eq_f = [jnp.where(key_scr[:, c * LANES:(c + 1) * LANES] == thr, 1.0, 0.0) for c in range(n_chunks)]
    totals = [jnp.sum(e, axis=1, keepdims=True) for e in eq_f]
    seen = jnp.zeros((rows, 1), f32)
    for c in range(n_chunks):
        kk = key_scr[:, c * LANES:(c + 1) * LANES]
        rank = seen + jnp.dot(eq_f[c].astype(bf16), tri, preferred_element_type=f32)
        sel = ((kk > thr) | ((kk == thr) & (rank <= need))) & (kk > INT32_MIN)
        key_scr[:, c * LANES:(c + 1) * LANES] = lax.bitcast_convert_type(jnp.where(sel, 0.0, MASK_BIAS), i32)
        seen = seen + totals[c]
    bias = lax.bitcast_convert_type(key_scr[...], f32)

    lane = lax.broadcasted_iota(i32, (t_new, LANES), 1)
    half = [lane < HEAD_DIM, lane >= HEAD_DIM]
    scale = 1.0 / math.sqrt(HEAD_DIM)
    qpad[...] = jnp.zeros(qpad.shape, f32)
    for g in range(N_HEADS // N_KV_HEADS):
        src = q_ref[0, :, g * LANES:(g + 1) * LANES] * scale
        for j in range(N_KV_HEADS):
            r = g * N_KV_HEADS + j
            qpad[r * rows:r * rows + t_new, :] = jnp.where(half[j], src, 0.0)
    s_all = mm_f32(qpad[...].astype(bf16), kall[...].astype(bf16))
    for r in range(N_HEADS):
        s = s_all[r * rows:(r + 1) * rows] + bias
        p = jnp.exp(s - jnp.max(s, axis=1, keepdims=True))
        p_scr[r * rows:(r + 1) * rows, :] = p / jnp.sum(p, axis=1, keepdims=True)
    o_all = _nt_dot(p_scr[...].astype(bf16), vall[...].astype(bf16))
    for g in range(N_HEADS // N_KV_HEADS):
        r0, r1 = g * N_KV_HEADS * rows, (g * N_KV_HEADS + 1) * rows
        o_ref[0, :, g * LANES:(g + 1) * LANES] = jnp.where(half[0], o_all[r0:r0 + t_new], o_all[r1:r1 + t_new])


def dsa_decode(proj, k_new, v_new, cache_k, cache_v, cache_ik, page_table, layer):
    b, t_new, _ = proj.shape
    n_pages = page_table.shape[1]
    assert t_new <= DEC_ROWS and PAGE_SIZE == LANES
    s_pad = (n_pages + 1) * PAGE_SIZE
    k_sel = min(TOPK_MAX, (n_pages * PAGE_SIZE + t_new) // 4)
    cache_k = jnp.transpose(cache_k, (0, 1, 3, 4, 2))
    cache_v = jnp.transpose(cache_v, (0, 1, 3, 4, 2))
    cache_ik = jnp.transpose(cache_ik, (0, 1, 3, 2))

    def tok_spec(width, col=0):
        return pl.BlockSpec((1, t_new, width), lambda bi, pt: (bi, 0, col // width))

    def page_spec(p):
        return pl.BlockSpec((None, None, IDX_DIM, PAGE_SIZE), lambda bi, pt, p=p: (pt[bi, p], layer, 0, 0))

    def kv_page_spec(p):
        return pl.BlockSpec((None, None, N_KV_HEADS, HEAD_DIM, PAGE_SIZE),
                            lambda bi, pt, p=p: (pt[bi, p], layer, 0, 0, 0))

    grid_spec = pltpu.PrefetchScalarGridSpec(
        num_scalar_prefetch=1,
        grid=(b,),
        in_specs=[tok_spec(N_HEADS * HEAD_DIM, COL_Q), tok_spec(IDX_HEADS * IDX_DIM, COL_QI),
                  tok_spec(LANES, COL_SMALL), tok_spec(LANES), tok_spec(LANES)]
                 + [kv_page_spec(p) for p in range(n_pages)]
                 + [kv_page_spec(p) for p in range(n_pages)]
                 + [page_spec(p) for p in range(n_pages)],
        out_specs=tok_spec(N_HEADS * HEAD_DIM),
        scratch_shapes=[pltpu.VMEM((LANES, s_pad), jnp.float32),
                        pltpu.VMEM((LANES, s_pad), jnp.float32),
                        pltpu.VMEM((IDX_DIM, s_pad), jnp.float32),
                        pltpu.VMEM((LANES, LANES), jnp.float32),
                        pltpu.VMEM((IDX_HEADS * DEC_ROWS, IDX_DIM), jnp.float32),
                        pltpu.VMEM((DEC_ROWS, LANES), jnp.float32),
                        pltpu.VMEM((N_HEADS * DEC_ROWS, LANES), jnp.float32),
                        pltpu.VMEM((DEC_ROWS, s_pad), jnp.int32),
                        pltpu.VMEM((N_HEADS * DEC_ROWS, s_pad), jnp.float32)])
    return pl.pallas_call(
        functools.partial(_dsa_decode_kernel, n_pages=n_pages, k_sel=k_sel),
        grid_spec=grid_spec,
        out_shape=jax.ShapeDtypeStruct((b, t_new, N_HEADS * HEAD_DIM), jnp.float32),
        compiler_params=pltpu.CompilerParams(dimension_semantics=("arbitrary",),
                                             vmem_limit_bytes=VMEM_LIMIT_BYTES),
        name="dsa_decode",
    )(page_table, proj, proj, proj, k_new, v_new,
      *([cache_k] * n_pages), *([cache_v] * n_pages), *([cache_ik] * n_pages))


COL_GATES, COL_XBC, COL_Q, COL_GLU, COL_Z, COL_QI, COL_K, COL_V, COL_SMALL, COL_DT, D_IN_PAD = (
    0, 3072, 4608, 5120, 6144, 7168, 7424, 7552, 7680, 7808, 7936)
ROW_BLOCK = 256


def _ln_rows(x, g, b):
    mu = jnp.mean(x, axis=-1, keepdims=True)
    xc = x - mu
    var = jnp.mean(xc * xc, axis=-1, keepdims=True)
    return xc * lax.rsqrt(var + LN_EPS) * g + b


def _bdot(a, w_ref):
    return jnp.dot(a.astype(jnp.bfloat16), w_ref[...], preferred_element_type=jnp.float32)


def _branch_mix_kernel(cv_ref, o_ref, y_ref, z_ref, gates_ref, x_ref, lncg_ref, lncb_ref, ng_ref,
                       wc_ref, wa_ref, ws_ref, wm_ref, lng_ref, lnb_ref, out_ref):
    ca = _ln_rows(cv_ref[...], lncg_ref[...], lncb_ref[...])
    y_a = _bdot(ca * jax.nn.sigmoid(ca), wc_ref)
    y_b = _bdot(o_ref[...], wa_ref)
    z = z_ref[...]
    t = y_ref[...] * (z * jax.nn.sigmoid(z))
    t = t * lax.rsqrt(jnp.mean(t * t, axis=-1, keepdims=True) + LN_EPS) * ng_ref[...]
    y_c = _bdot(t, ws_ref)
    d = D_MODEL
    mix = (jax.nn.sigmoid(gates_ref[:, 0:d]) * y_a + jax.nn.sigmoid(gates_ref[:, d:2 * d]) * y_b
           + jax.nn.sigmoid(gates_ref[:, 2 * d:3 * d]) * y_c)
    out_ref[...] = _ln_rows(DN_ALPHA * x_ref[...] + _bdot(mix, wm_ref), lng_ref[...], lnb_ref[...])


def branch_mix_ln(cv, o, y, proj, x, lncg, lncb, ng, wc, wa, ws, wm, lng, lnb):
    m = x.shape[0]
    tm = min(ROW_BLOCK, m)
    assert m % tm == 0
    d = D_MODEL

    def rows(width, col_block=0):
        return pl.BlockSpec((tm, width), lambda i, cb=col_block: (i, cb))

    def whole(a):
        return pl.BlockSpec(a.shape, lambda i: (0,) * a.ndim, pipeline_mode=pl.Buffered(1))

    vecs = [a.reshape(1, -1) for a in (lncg, lncb, ng)]
    lnv = [a.reshape(1, -1) for a in (lng, lnb)]
    return pl.pallas_call(
        _branch_mix_kernel,
        grid=(m // tm,),
        in_specs=[rows(CONV_DIM), rows(N_HEADS * HEAD_DIM), rows(SSM_INNER), rows(SSM_INNER, COL_Z // SSM_INNER),
                  rows(3 * d, COL_GATES // (3 * d)), rows(d)]
                 + [whole(a) for a in vecs] + [whole(a) for a in (wc, wa, ws, wm)] + [whole(a) for a in lnv],
        out_specs=rows(d),
        out_shape=jax.ShapeDtypeStruct((m, d), jnp.float32),
        compiler_params=pltpu.CompilerParams(dimension_semantics=("parallel",), vmem_limit_bytes=VMEM_LIMIT_BYTES),
        name="branch_mix_ln",
    )(cv, o, y, proj, proj, x, *vecs, wc, wa, ws, wm, *lnv)


def _mem_attn_kernel(x_ref, mk_ref, mv_ref, wq_ref, wo_ref, lng_ref, lnb_ref, out_ref, o_scr, *, seqs, t_seq):
    bf16 = jnp.bfloat16
    x = x_ref[...]
    q = _bdot(x, wq_ref)
    scale = 1.0 / math.sqrt(MEM_HEAD_DIM)
    for s in range(seqs):
        qs = q[s * t_seq:(s + 1) * t_seq].astype(bf16)
        for h in range(MEM_HEADS):
            cols = slice(h * MEM_HEAD_DIM, (h + 1) * MEM_HEAD_DIM)
            sc = _nt_dot(qs[:, cols], mk_ref[s, :, h, :].astype(bf16)) * scale
            p = jnp.exp(sc - jnp.max(sc, axis=-1, keepdims=True))
            p = p / jnp.sum(p, axis=-1, keepdims=True)
            o_scr[s * t_seq:(s + 1) * t_seq, cols] = jnp.dot(p.astype(bf16), mv_ref[s, :, h, :].astype(bf16),
                                                              preferred_element_type=jnp.float32)
    out_ref[...] = _ln_rows(DN_ALPHA * x + _bdot(o_scr[...], wo_ref), lng_ref[...], lnb_ref[...])


def mem_attn_ln(x, mk, mv, wq, wo, lng, lnb, t_seq, layer=None):
    m, d = x.shape
    hd = MEM_HEADS * MEM_HEAD_DIM
    if t_seq >= ROW_BLOCK:
        seqs, tm = 1, ROW_BLOCK
        assert t_seq % tm == 0
        per_seq = t_seq // tm
        seq_of = lambda i: i // per_seq
    else:
        seqs = max(1, 32 // t_seq)
        tm = seqs * t_seq
        assert m % tm == 0
        seq_of = lambda i: i
    if layer is None:
        mem_spec = pl.BlockSpec((seqs, N_MEM, MEM_HEADS, MEM_HEAD_DIM), lambda i: (seq_of(i), 0, 0, 0))
    else:
        mem_spec = pl.BlockSpec((seqs, None, N_MEM, MEM_HEADS, MEM_HEAD_DIM), lambda i: (seq_of(i), layer, 0, 0, 0))

    def whole(a):
        return pl.BlockSpec(a.shape, lambda i: (0,) * a.ndim, pipeline_mode=pl.Buffered(1))

    lnv = [a.reshape(1, -1) for a in (lng, lnb)]
    return pl.pallas_call(
        functools.partial(_mem_attn_kernel, seqs=seqs, t_seq=min(t_seq, tm)),
        grid=(m // tm,),
        in_specs=[pl.BlockSpec((tm, d), lambda i: (i, 0)), mem_spec, mem_spec, whole(wq), whole(wo)]
                 + [whole(a) for a in lnv],
        out_specs=pl.BlockSpec((tm, d), lambda i: (i, 0)),
        out_shape=jax.ShapeDtypeStruct((m, d), jnp.float32),
        scratch_shapes=[pltpu.VMEM((tm, hd), jnp.float32)],
        compiler_params=pltpu.CompilerParams(dimension_semantics=("parallel",), vmem_limit_bytes=VMEM_LIMIT_BYTES),
        name="mem_attn_ln",
    )(x, mk, mv, wq, wo, *lnv)


MOE_ROW_BLOCK = 1024


def _moe_kernel(x_ref, wr_hi_ref, wr_lo_ref, br_ref, wg_ref, wu_ref, wd_ref, lng_ref, lnb_ref, out_ref,
                xb_scr, comb_scr, acc_scr):
    f32 = jnp.float32
    e = pl.program_id(1)
    tm = x_ref.shape[0]
    lane = lax.broadcasted_iota(jnp.int32, (tm, LANES), 1).astype(f32)

    @pl.when(e == 0)
    def _route():
        x_hi, x_lo = _split_bf16(x_ref[...])
        xb_scr[...] = x_hi
        lg = (jnp.dot(x_hi, wr_hi_ref[...], preferred_element_type=f32)
              + jnp.dot(x_lo, wr_hi_ref[...], preferred_element_type=f32)
              + jnp.dot(x_hi, wr_lo_ref[...], preferred_element_type=f32) + br_ref[...])
        is_g = lane < N_GROUPS
        mg = jnp.max(jnp.where(is_g, lg, -jnp.inf), axis=1, keepdims=True)
        g_prob = 1.0 / jnp.sum(jnp.where(is_g, jnp.exp(lg - mg), 0.0), axis=1, keepdims=True)
        g_idx = jnp.min(jnp.where(is_g & (lg == mg), lane, float(LANES)), axis=1, keepdims=True)
        lo_e = N_GROUPS + EXPERTS_PER_GROUP * g_idx
        is_e = (lane >= lo_e) & (lane < lo_e + EXPERTS_PER_GROUP)
        me = jnp.max(jnp.where(is_e, lg, -jnp.inf), axis=1, keepdims=True)
        ee = jnp.where(is_e, jnp.exp(lg - me), 0.0)
        pe = jnp.where(is_e, ee / jnp.sum(ee, axis=1, keepdims=True), -1.0)
        p1 = jnp.max(pe, axis=1, keepdims=True)
        first = jnp.min(jnp.where(pe == p1, lane, float(LANES)), axis=1, keepdims=True)
        pe2 = jnp.where(lane == first, -1.0, pe)
        p2 = jnp.max(pe2, axis=1, keepdims=True)
        second = jnp.min(jnp.where(pe2 == p2, lane, float(LANES)), axis=1, keepdims=True)
        norm = g_prob / (p1 + p2)
        comb_scr[...] = jnp.where(lane == first, p1 * norm, jnp.where(lane == second, p2 * norm, 0.0))
        acc_scr[...] = jnp.zeros(acc_scr.shape, f32)

    xb = xb_scr[...]
    hg = jnp.dot(xb, wg_ref[0], preferred_element_type=f32)
    hu = jnp.dot(xb, wu_ref[0], preferred_element_type=f32)
    c = jnp.sum(jnp.where(lane == (e + N_GROUPS).astype(f32), comb_scr[...], 0.0), axis=1, keepdims=True)
    hid = hg * jax.nn.sigmoid(hg) * hu * c
    acc_scr[...] += jnp.dot(hid.astype(jnp.bfloat16), wd_ref[0], preferred_element_type=f32)

    @pl.when(e == pl.num_programs(1) - 1)
    def _finish():
        out_ref[...] = _ln_rows(DN_ALPHA * x_ref[...] + acc_scr[...], lng_ref[...], lnb_ref[...])


def moe_ln(x, w_group, b_group, w_router, b_router, wg, wu, wd, lng, lnb):
    m, d = x.shape
    tm = min(MOE_ROW_BLOCK, m)
    assert m % tm == 0 and N_GROUPS + N_EXPERTS <= LANES
    pad = jnp.zeros((d, LANES - N_GROUPS - N_EXPERTS), jnp.float32)
    wr_hi, wr_lo = _split_bf16(jnp.concatenate([w_group, w_router, pad], axis=1))
    br = jnp.concatenate([b_group, b_router, pad[0]]).reshape(1, LANES)
    lnv = [a.reshape(1, -1) for a in (lng, lnb)]

    def whole(a):
        return pl.BlockSpec(a.shape, lambda i, e: (0,) * a.ndim, pipeline_mode=pl.Buffered(1))

    return pl.pallas_call(
        _moe_kernel,
        grid=(m // tm, N_EXPERTS),
        in_specs=[pl.BlockSpec((tm, d), lambda i, e: (i, 0)), whole(wr_hi), whole(wr_lo), whole(br),
                  pl.BlockSpec((1, d, D_EXPERT), lambda i, e: (e, 0, 0)),
                  pl.BlockSpec((1, d, D_EXPERT), lambda i, e: (e, 0, 0)),
                  pl.BlockSpec((1, D_EXPERT, d), lambda i, e: (e, 0, 0))] + [whole(a) for a in lnv],
        out_specs=pl.BlockSpec((tm, d), lambda i, e: (i, 0)),
        out_shape=jax.ShapeDtypeStruct((m, d), jnp.float32),
        scratch_shapes=[pltpu.VMEM((tm, d), jnp.bfloat16), pltpu.VMEM((tm, LANES), jnp.float32),
                        pltpu.VMEM((tm, d), jnp.float32)],
        compiler_params=pltpu.CompilerParams(dimension_semantics=("parallel", "arbitrary"),
                                             vmem_limit_bytes=VMEM_LIMIT_BYTES),
        name="moe_ln",
    )(x, wr_hi, wr_lo, br, wg, wu, wd, *lnv)


CONV_HIST_ROWS = 32
CONV_ROW_BLOCK = 128
CONV_ROW_TILE = 32
SUBLANES = 8


def _conv_glu_kernel(x_ref, hist_ref, w_ref, cv_ref, st_ref, xp_scr, z_scr):
    f32 = jnp.float32
    ti = pl.program_id(1)
    tb = x_ref.shape[1]
    n_hist = CONV_WIDTH - 1
    base = CONV_HIST_ROWS - n_hist

    @pl.when(ti == 0)
    def _init():
        xp_scr[0:CONV_HIST_ROWS, :] = jnp.zeros((CONV_HIST_ROWS, CONV_DIM), f32)
        xp_scr[base:CONV_HIST_ROWS, :] = hist_ref[0]

    xin = x_ref[0]
    xp_scr[CONV_HIST_ROWS:CONV_HIST_ROWS + tb, :] = xin[:, :CONV_DIM] * jax.nn.sigmoid(xin[:, CONV_DIM:])
    if tb % SUBLANES == 0:
        rows_z = tb + SUBLANES
        xp_scr[CONV_HIST_ROWS + tb:CONV_HIST_ROWS + tb + 2 * SUBLANES, :] = jnp.zeros((2 * SUBLANES, CONV_DIM), f32)
        for b in range(SUBLANES):
            taps = [(a, SUBLANES * a + b - base) for a in range(CONV_HIST_ROWS // SUBLANES + 1)]
            taps = [(a, j) for a, j in taps if 0 <= j < CONV_WIDTH]
            for r0 in range(0, rows_z, CONV_ROW_TILE):
                nr = min(CONV_ROW_TILE, rows_z - r0)
                acc = None
                for a, j in taps:
                    term = w_ref[j:j + 1, :] * xp_scr[SUBLANES * a + r0:SUBLANES * a + r0 + nr, :]
                    acc = term if acc is None else acc + term
                z_scr[b, r0:r0 + nr, :] = acc
        out = z_scr[0, 0:tb, :]
        for b in range(1, SUBLANES):
            out = out + z_scr[b, b:b + tb, :]
    else:
        out = w_ref[0:1, :] * xp_scr[base:base + tb, :]
        for j in range(1, CONV_WIDTH):
            out = out + w_ref[j:j + 1, :] * xp_scr[base + j:base + j + tb, :]
    cv_ref[0] = out
    st_ref[0] = xp_scr[base + tb:base + tb + n_hist, :]
    xp_scr[base:CONV_HIST_ROWS, :] = xp_scr[base + tb:CONV_HIST_ROWS + tb, :]


def conv_glu(proj, hist, w):
    b, t, _ = proj.shape
    tb = CONV_ROW_BLOCK if t % CONV_ROW_BLOCK == 0 else t
    n_hist = CONV_WIDTH - 1
    assert tb >= n_hist or t == tb
    return pl.pallas_call(
        _conv_glu_kernel,
        grid=(b, t // tb),
        in_specs=[pl.BlockSpec((1, tb, 2 * CONV_DIM), lambda bi, ti: (bi, ti, COL_GLU // (2 * CONV_DIM))),
                  pl.BlockSpec((1, n_hist, CONV_DIM), lambda bi, ti: (bi, 0, 0)),
                  pl.BlockSpec(w.shape, lambda bi, ti: (0, 0))],
        out_specs=[pl.BlockSpec((1, tb, CONV_DIM), lambda bi, ti: (bi, ti, 0)),
                   pl.BlockSpec((1, n_hist, CONV_DIM), lambda bi, ti: (bi, 0, 0))],
        out_shape=[jax.ShapeDtypeStruct((b, t, CONV_DIM), jnp.float32),
                   jax.ShapeDtypeStruct((b, n_hist, CONV_DIM), jnp.float32)],
        scratch_shapes=[pltpu.VMEM((CONV_HIST_ROWS + tb + 2 * SUBLANES, CONV_DIM), jnp.float32),
                        pltpu.VMEM((SUBLANES, tb + SUBLANES, CONV_DIM), jnp.float32)],
        compiler_params=pltpu.CompilerParams(dimension_semantics=("parallel", "arbitrary"),
                                             vmem_limit_bytes=VMEM_LIMIT_BYTES),
        name="conv_glu",
    )(proj, hist, w)


SSD_HIST_ROWS = 8


def _ssd_kernel(xbc_ref, dt_ref, hist_ref, cw_ref, cb_ref, dtb_ref, alog_ref, dexp_ref, h0_ref,
                y_ref, hT_ref, xp_scr, dt_scr, ht_scr, xe_scr):
    f32, i32, bf16 = jnp.float32, jnp.int32, jnp.bfloat16
    c = pl.program_id(1)
    l_in = xbc_ref.shape[1]
    L = SSM_CHUNK
    n_hist = SSM_CONV - 1
    base = SSD_HIST_ROWS - n_hist
    gw = SSM_STATE
    heads_per_group = SSM_HEADS // SSM_GROUPS
    pairs_per_group = heads_per_group // 2

    @pl.when(c == 0)
    def _init():
        xp_scr[base:SSD_HIST_ROWS, :] = hist_ref[0]
        ht_scr[...] = h0_ref[0]

    xp_scr[SSD_HIST_ROWS:SSD_HIST_ROWS + l_in, :] = xbc_ref[0]
    dt_scr[0:l_in, :] = dt_ref[0]
    if l_in < L:
        xp_scr[SSD_HIST_ROWS + l_in:SSD_HIST_ROWS + L, :] = jnp.zeros((L - l_in, SSM_XBC), f32)
        dt_scr[l_in:L, :] = jnp.zeros((L - l_in, LANES), f32)
    acc = cw_ref[0:1, :] * xp_scr[base:base + L, :]
    for j in range(1, SSM_CONV):
        acc = acc + cw_ref[j:j + 1, :] * xp_scr[base + j:base + j + L, :]
    acc = acc + cb_ref[...]
    xbc = acc * jax.nn.sigmoid(acc)
    xp_scr[base:SSD_HIST_ROWS, :] = xp_scr[base + L:SSD_HIST_ROWS + L, :]
    xs = xbc[:, 0:SSM_INNER]
    bm = xbc[:, SSM_INNER:SSM_INNER + SSM_GROUPS * gw]
    cm = xbc[:, SSM_INNER + SSM_GROUPS * gw:SSM_INNER + 2 * SSM_GROUPS * gw]

    x_dt = dt_scr[...] + dtb_ref[...]
    dtf = jnp.maximum(x_dt, 0.0) + jnp.log1p(jnp.exp(-jnp.abs(x_dt)))
    if l_in < L:
        dtf = jnp.where(lax.broadcasted_iota(i32, (L, LANES), 0) < l_in, dtf, 0.0)
    la = dtf * (-jnp.exp(alog_ref[...]))
    row = lax.broadcasted_iota(i32, (L, L), 0)
    colk = lax.broadcasted_iota(i32, (L, L), 1)
    causal = colk <= row
    tril = jnp.where(causal, 1.0, 0.0).astype(bf16)
    la_hi = la.astype(bf16)
    r1 = la - la_hi.astype(f32)
    la_mid = r1.astype(bf16)
    la_lo = (r1 - la_mid.astype(f32)).astype(bf16)
    cs = (jnp.dot(tril, la_hi, preferred_element_type=f32) + jnp.dot(tril, la_mid, preferred_element_type=f32)
          + jnp.dot(tril, la_lo, preferred_element_type=f32))
    cs_t = cs.T
    ecs = jnp.exp(cs)
    to_end = jnp.exp(cs[L - 1:L, :] - cs)

    lane = lax.broadcasted_iota(i32, (L, LANES), 1)
    half0 = lane < SSM_HEAD_DIM

    def pair_cols(a, h0):
        return jnp.where(half0, a[:, h0:h0 + 1], a[:, h0 + 1:h0 + 2])

    for g in range(SSM_GROUPS):
        b_g = bm[:, g * gw:(g + 1) * gw]
        c_g = cm[:, g * gw:(g + 1) * gw].astype(bf16)
        cb = _nt_dot(c_g, b_g.astype(bf16))
        cols = slice(g * heads_per_group * SSM_HEAD_DIM, (g + 1) * heads_per_group * SSM_HEAD_DIM)
        y_off = jnp.dot(c_g, ht_scr[:, cols].astype(bf16), preferred_element_type=f32)
        for i in range(pairs_per_group):
            h0 = g * heads_per_group + 2 * i
            pc = slice((h0 // 2) * LANES, (h0 // 2 + 1) * LANES)
            xs_p = xs[:, pc]
            xdt_p = xs_p * pair_cols(dtf, h0)
            att = []
            for hh in range(2):
                seg = cs[:, h0 + hh:h0 + hh + 1] - cs_t[h0 + hh:h0 + hh + 1, :]
                att.append((cb * jnp.exp(jnp.where(causal, seg, -jnp.inf))).astype(bf16))
            rhs = jnp.concatenate([jnp.where(half0, xdt_p, 0.0), jnp.where(half0, 0.0, xdt_p)], axis=0).astype(bf16)
            y_diag = jnp.dot(jnp.concatenate(att, axis=1), rhs, preferred_element_type=f32)
            xe_scr[:, pc] = xdt_p * pair_cols(to_end, h0)
            y_p = y_diag + y_off[:, i * LANES:(i + 1) * LANES] * pair_cols(ecs, h0) + dexp_ref[:, pc] * xs_p
            y_ref[0, :, pc] = y_p[0:l_in]
        bt_hi, bt_lo = _split_bf16(b_g.T)
        xe_hi, xe_lo = _split_bf16(xe_scr[:, cols])
        st = (jnp.dot(bt_hi, xe_hi, preferred_element_type=f32) + jnp.dot(bt_lo, xe_hi, preferred_element_type=f32)
              + jnp.dot(bt_hi, xe_lo, preferred_element_type=f32))
        for i in range(pairs_per_group):
            h0 = g * heads_per_group + 2 * i
            pc = slice((h0 // 2) * LANES, (h0 // 2 + 1) * LANES)
            decay = jnp.where(half0[0:1, :], ecs[L - 1:L, h0:h0 + 1], ecs[L - 1:L, h0 + 1:h0 + 2])
            ht_scr[:, pc] = ht_scr[:, pc] * decay + st[:, i * LANES:(i + 1) * LANES]

    @pl.when(c == pl.num_programs(1) - 1)
    def _done():
        hT_ref[0] = ht_scr[...]


def ssd_prompt(proj, hist, conv_w, conv_b, dt_bias, a_log, d_skip, h0_t):
    b, t, _ = proj.shape
    L = SSM_CHUNK
    l_in = L if t % L == 0 else t
    assert t % l_in == 0 and l_in <= L
    assert SSM_HEAD_DIM * 2 == LANES and SSM_STATE == LANES and SSM_HEADS <= LANES

    def lane_pad(a):
        return jnp.concatenate([a, jnp.zeros((LANES - a.shape[0],), a.dtype)]).reshape(1, LANES)

    def whole(a):
        return pl.BlockSpec(a.shape, lambda bi, c: (0,) * a.ndim)

    params = [conv_w, conv_b.reshape(1, -1), lane_pad(dt_bias), lane_pad(a_log),
              jnp.repeat(d_skip, SSM_HEAD_DIM).reshape(1, SSM_INNER)]
    return pl.pallas_call(
        _ssd_kernel,
        grid=(b, t // l_in),
        in_specs=[pl.BlockSpec((1, l_in, SSM_XBC), lambda bi, c: (bi, c, COL_XBC // SSM_XBC)),
                  pl.BlockSpec((1, l_in, LANES), lambda bi, c: (bi, c, COL_DT // LANES)),
                  pl.BlockSpec((1, SSM_CONV - 1, SSM_XBC), lambda bi, c: (bi, 0, 0))]
                 + [whole(a) for a in params]
                 + [pl.BlockSpec((1, SSM_STATE, SSM_INNER), lambda bi, c: (bi, 0, 0))],
        out_specs=[pl.BlockSpec((1, l_in, SSM_INNER), lambda bi, c: (bi, c, 0)),
                   pl.BlockSpec((1, SSM_STATE, SSM_INNER), lambda bi, c: (bi, 0, 0))],
        out_shape=[jax.ShapeDtypeStruct((b, t, SSM_INNER), jnp.float32),
                   jax.ShapeDtypeStruct((b, SSM_STATE, SSM_INNER), jnp.float32)],
        scratch_shapes=[pltpu.VMEM((SSD_HIST_ROWS + L, SSM_XBC), jnp.float32),
                        pltpu.VMEM((L, LANES), jnp.float32),
                        pltpu.VMEM((SSM_STATE, SSM_INNER), jnp.float32),
                        pltpu.VMEM((L, SSM_INNER), jnp.float32)],
        compiler_params=pltpu.CompilerParams(dimension_semantics=("parallel", "arbitrary"),
                                             vmem_limit_bytes=VMEM_LIMIT_BYTES),
        name="ssd_prompt",
    )(proj, proj, hist, *params, h0_t)


def token_mixer(x, p, conv_buf, ssm_buf, ssm_h0, past):
    (w_in, conv_dw, ln_conv_g, ln_conv_b, w_conv_out, w_attn_out, ssm_conv_w, ssm_conv_b,
     ssm_dt_bias, ssm_a_log, ssm_d, ssm_norm_g, w_ssm_out, w_mix_out, ln_mix_g, ln_mix_b) = p
    b, t, _ = x.shape
    proj = mm_nt(x.reshape(b * t, D_MODEL), w_in).reshape(b, t, D_IN_PAD)
    k = proj[..., COL_K:COL_K + LANES]
    v = proj[..., COL_V:COL_V + LANES]
    ki = proj[..., COL_SMALL:COL_SMALL + IDX_DIM]
    cv, conv_state = conv_glu(proj, conv_buf, conv_dw)
    if past is None:
        o = dsa_prompt(proj, k, v, ki)
    else:
        cache_k, cache_v, cache_ik, page_table, layer = past
        o = dsa_decode(proj, k, v, cache_k, cache_v, cache_ik, page_table, layer)
    h0_t = ssm_h0.reshape(b, SSM_INNER, SSM_STATE).swapaxes(1, 2)
    y, h_t = ssd_prompt(proj, ssm_buf, ssm_conv_w, ssm_conv_b, ssm_dt_bias, ssm_a_log, ssm_d, h0_t)
    h_last = h_t.swapaxes(1, 2).reshape(b, SSM_HEADS, SSM_HEAD_DIM, SSM_STATE)
    n_keep = SSM_CONV - 1
    xbc_tail = proj[:, max(0, t - n_keep):, COL_XBC:COL_XBC + SSM_XBC]
    xbc_pad = jnp.concatenate([ssm_buf, xbc_tail], axis=1)
    m = b * t
    x_new = branch_mix_ln(cv.reshape(m, CONV_DIM), o.reshape(m, -1), y.reshape(m, SSM_INNER),
                          proj.reshape(m, D_IN_PAD), x.reshape(m, D_MODEL), ln_conv_g, ln_conv_b, ssm_norm_g,
                          w_conv_out, w_attn_out, w_ssm_out, w_mix_out, ln_mix_g, ln_mix_b).reshape(b, t, D_MODEL)
    return (x_new, k.reshape(b, t, N_KV_HEADS, HEAD_DIM), v.reshape(b, t, N_KV_HEADS, HEAD_DIM), ki,
            conv_state, xbc_pad[:, -(SSM_CONV - 1):], h_last)


def _pad_w_in_t(wt):
    sp = (0,) + SPLIT_POINTS + (wt.shape[0],)
    seg = [wt[sp[i]:sp[i + 1]] for i in range(len(SPLIT_SIZES))]
    glu, q, k, v, qi, ki, wi, z, xbc, dt, gates = seg
    pad = jnp.zeros((LANES - IDX_DIM - IDX_HEADS - SSM_HEADS, wt.shape[1]), wt.dtype)
    n_rep = N_HEADS // N_KV_HEADS
    q = q.reshape(N_KV_HEADS, n_rep, HEAD_DIM, -1).swapaxes(0, 1).reshape(q.shape)
    dt_pad = jnp.zeros((LANES - SSM_HEADS, wt.shape[1]), wt.dtype)
    out = jnp.concatenate([gates, xbc, q, glu, z, qi, k, v, ki, wi, dt, pad, dt, dt_pad], axis=0)
    assert out.shape[0] == D_IN_PAD
    return out


def _perm_w_attn_out(w):
    n_rep = N_HEADS // N_KV_HEADS
    return w.reshape(N_KV_HEADS, n_rep, HEAD_DIM, -1).swapaxes(0, 1).reshape(w.shape)


def kernel(x_prompt, x_sample, mem_prompt, cache_k, cache_v, cache_ik, cache_mem_k, cache_mem_v, state_conv, state_ssm_conv, state_ssm, page_table, w_in, conv_dw, ln_conv_g, ln_conv_b, w_conv_out, w_attn_out, ssm_conv_w, ssm_conv_b, ssm_dt_bias, ssm_a_log, ssm_d, ssm_norm_g, w_ssm_out, w_mix_out, ln_mix_g, ln_mix_b, w_mq, w_mk, w_mv, w_mo, ln_mem_g, ln_mem_b, w_group, b_group, w_router, b_router, w_e_gate, w_e_up, w_e_down, ln_ffn_g, ln_ffn_b):
    bf16 = jnp.bfloat16
    bp, tp, _ = x_prompt.shape
    bs, ts, _ = x_sample.shape
    xp, xs = x_prompt, x_sample
    outs_p = [[] for _ in range(8)]
    outs_s = [[] for _ in range(6)]
    for l in range(DEPTH):
        mix_p = (_pad_w_in_t(jnp.transpose(w_in[l])).astype(bf16), conv_dw[l], ln_conv_g[l], ln_conv_b[l],
                 w_conv_out[l].astype(bf16), _perm_w_attn_out(w_attn_out[l]).astype(bf16),
                 ssm_conv_w[l], ssm_conv_b[l], ssm_dt_bias[l], ssm_a_log[l], ssm_d[l], ssm_norm_g[l],
                 w_ssm_out[l].astype(bf16), w_mix_out[l].astype(bf16), ln_mix_g[l], ln_mix_b[l])
        moe_p = (w_group[l], b_group[l], w_router[l], b_router[l],
                 w_e_gate[l].astype(bf16), w_e_up[l].astype(bf16), w_e_down[l].astype(bf16), ln_ffn_g[l], ln_ffn_b[l])
        mem_p = (w_mq[l].astype(bf16), w_mo[l].astype(bf16), ln_mem_g[l], ln_mem_b[l])
        hd_mem = MEM_HEADS * MEM_HEAD_DIM
        xp, kp, vp, kip, cbp, sbp, hp = token_mixer(
            xp, mix_p,
            jnp.zeros((bp, CONV_WIDTH - 1, CONV_DIM), xp.dtype),
            jnp.zeros((bp, SSM_CONV - 1, SSM_XBC), xp.dtype),
            jnp.zeros((bp, SSM_HEADS, SSM_HEAD_DIM, SSM_STATE), jnp.float32),
            None)
        mem_kv = mm(mem_prompt.reshape(bp * N_MEM, D_MODEL),
                    jnp.concatenate([w_mk[l], w_mv[l]], axis=1).astype(bf16)).reshape(bp, N_MEM, 2 * hd_mem)
        mkp = mem_kv[..., :hd_mem].reshape(bp, N_MEM, MEM_HEADS, MEM_HEAD_DIM)
        mvp = mem_kv[..., hd_mem:].reshape(bp, N_MEM, MEM_HEADS, MEM_HEAD_DIM)
        xp = mem_attn_ln(xp.reshape(bp * tp, D_MODEL), mkp, mvp, *mem_p, t_seq=tp)
        xp = moe_ln(xp, *moe_p).reshape(bp, tp, D_MODEL)
        xs, ks_new, vs_new, kis, cbs, sbs, hs = token_mixer(
            xs, mix_p, state_conv[:, l], state_ssm_conv[:, l], state_ssm[:, l],
            (cache_k, cache_v, cache_ik, page_table, l))
        xs = mem_attn_ln(xs.reshape(bs * ts, D_MODEL), cache_mem_k, cache_mem_v, *mem_p, t_seq=ts, layer=l)
        xs = moe_ln(xs, *moe_p).reshape(bs, ts, D_MODEL)
        for lst, arr in zip(outs_p, (kp, vp, kip, mkp, mvp, cbp, sbp, hp)):
            lst.append(arr)
        for lst, arr in zip(outs_s, (ks_new, vs_new, kis, cbs, sbs, hs)):
            lst.append(arr)
    p_k, p_v, p_ik, p_mem_k, p_mem_v, p_conv, p_ssm_conv, p_ssm = [jnp.stack(a, axis=1) for a in outs_p]
    s_k, s_v, s_ik, s_conv, s_ssm_conv, s_ssm = [jnp.stack(a, axis=1) for a in outs_s]
    return (xp, xs, p_k, p_v, p_ik, p_mem_k, p_mem_v, p_conv, p_ssm_conv, p_ssm,
            s_k, s_v, s_ik, s_conv, s_ssm_conv, s_ssm)
```

```python
import functools
import math

import jax
import jax.numpy as jnp
from jax import lax
from jax.experimental import pallas as pl
from jax.experimental.pallas import tpu as pltpu

D_MODEL = 1024
DEPTH = 2
PAGE_SIZE = 128
CONV_DIM = 512
CONV_WIDTH = 31
N_HEADS = 8
N_KV_HEADS = 2
HEAD_DIM = 64
IDX_HEADS = 4
IDX_DIM = 64
TOPK_MAX = 256
Q_BLOCK = 128
SSM_HEADS = 16
SSM_HEAD_DIM = 64
SSM_INNER = SSM_HEADS * SSM_HEAD_DIM
SSM_GROUPS = 2
SSM_STATE = 128
SSM_CONV = 4
SSM_CHUNK = 128
SSM_XBC = SSM_INNER + 2 * SSM_GROUPS * SSM_STATE
N_MEM = 256
MEM_HEADS = 4
MEM_HEAD_DIM = 128
N_GROUPS = 4
EXPERTS_PER_GROUP = 4
N_EXPERTS = N_GROUPS * EXPERTS_PER_GROUP
TOPK_IN_GROUP = 2
D_EXPERT = 512
DN_ALPHA = (2 * DEPTH) ** 0.25
LN_EPS = 1e-5

SPLIT_SIZES = (2 * CONV_DIM, N_HEADS * HEAD_DIM, N_KV_HEADS * HEAD_DIM, N_KV_HEADS * HEAD_DIM,
               IDX_HEADS * IDX_DIM, IDX_DIM, IDX_HEADS, SSM_INNER, SSM_XBC, SSM_HEADS, 3 * D_MODEL)
SPLIT_POINTS = tuple(sum(SPLIT_SIZES[:i + 1]) for i in range(len(SPLIT_SIZES) - 1))

VMEM_LIMIT_BYTES = 56 * 1024 * 1024
MM_COL_CHUNK = 512


def _mm_kernel(x_ref, w_ref, o_ref):
    xb = x_ref[...].astype(jnp.bfloat16)
    n = o_ref.shape[1]
    for c0 in range(0, n, MM_COL_CHUNK):
        c1 = min(n, c0 + MM_COL_CHUNK)
        o_ref[:, c0:c1] = jnp.dot(xb, w_ref[:, c0:c1], preferred_element_type=jnp.float32)


def mm(x, w, tm=256):
    m, k = x.shape
    n = w.shape[1]
    tm = min(tm, m)
    assert m % tm == 0 and n % 128 == 0 and k % 128 == 0
    return pl.pallas_call(
        _mm_kernel,
        grid=(m // tm,),
        in_specs=[pl.BlockSpec((tm, k), lambda i: (i, 0)),
                  pl.BlockSpec((k, n), lambda i: (0, 0), pipeline_mode=pl.Buffered(1))],
        out_specs=pl.BlockSpec((tm, n), lambda i: (i, 0)),
        out_shape=jax.ShapeDtypeStruct((m, n), jnp.float32),
        compiler_params=pltpu.CompilerParams(dimension_semantics=("parallel",),
                                             vmem_limit_bytes=VMEM_LIMIT_BYTES),
    )(x, w)


def _mm_nt_kernel(x_ref, wt_ref, o_ref):
    xb = x_ref[...].astype(jnp.bfloat16)
    n = o_ref.shape[1]
    for c0 in range(0, n, MM_COL_CHUNK):
        c1 = min(n, c0 + MM_COL_CHUNK)
        o_ref[:, c0:c1] = lax.dot_general(xb, wt_ref[c0:c1, :], (((1,), (1,)), ((), ())),
                                          preferred_element_type=jnp.float32)


def mm_nt(x, wt, tm=256):
    m, k = x.shape
    n = wt.shape[0]
    tm = min(tm, m)
    assert m % tm == 0 and n % 128 == 0 and k % 128 == 0
    return pl.pallas_call(
        _mm_nt_kernel,
        grid=(m // tm,),
        in_specs=[pl.BlockSpec((tm, k), lambda i: (i, 0)),
                  pl.BlockSpec((n, k), lambda i: (0, 0), pipeline_mode=pl.Buffered(1))],
        out_specs=pl.BlockSpec((tm, n), lambda i: (i, 0)),
        out_shape=jax.ShapeDtypeStruct((m, n), jnp.float32),
        compiler_params=pltpu.CompilerParams(dimension_semantics=("parallel",),
                                             vmem_limit_bytes=VMEM_LIMIT_BYTES),
        name="in_proj",
    )(x, wt)


INT32_MIN = -2 ** 31
MASK_BIAS = -1e30
DSA_KEY_CHUNK = 512
LANES = 128


def _nt_dot(a, b):
    return lax.dot_general(a, b, (((1,), (1,)), ((), ())), preferred_element_type=jnp.float32)


def _dsa_prompt_kernel(q_ref, qi_ref, sm_ref, k_ref, v_ref, ki_ref, o_ref,
                       key_scr, rank_scr, s_scr, p_scr, qpad_scr, qis_scr, m_scr, l_scr, acc_scr, *, k_sel):
    f32, i32, bf16 = jnp.float32, jnp.int32, jnp.bfloat16
    qb = q_ref.shape[1]
    kc_w = DSA_KEY_CHUNK
    sub = kc_w // LANES
    i = pl.program_id(1)
    n_chunks = (i * qb + qb + kc_w - 1) // kc_w
    lane = lax.broadcasted_iota(i32, (qb, LANES), 1)
    half = [lane < HEAD_DIM, lane >= HEAD_DIM]

    for pair in range(IDX_HEADS // 2):
        src = qi_ref[0, :, pair * LANES:(pair + 1) * LANES]
        hi = src.astype(bf16).astype(f32)
        lo_swapped = pltpu.roll(src - hi, IDX_DIM, axis=1)
        for hh in range(2):
            h = 2 * pair + hh
            qis_scr[h * qb:(h + 1) * qb, 0:LANES] = jnp.where(half[hh], hi, lo_swapped).astype(bf16)
            qis_scr[h * qb:(h + 1) * qb, LANES:2 * LANES] = jnp.where(half[hh], hi, 0.0).astype(bf16)
    w_idx = [sm_ref[0, :, IDX_DIM + h:IDX_DIM + h + 1] for h in range(IDX_HEADS)]
    row_pos = i * qb + lax.broadcasted_iota(i32, (qb, kc_w), 0)
    col_iota = lax.broadcasted_iota(i32, (qb, kc_w), 1)

    def score_body(c, carry):
        off = pl.multiple_of(c * kc_w, kc_w)
        d = _nt_dot(qis_scr[...], ki_ref[0, pl.ds(off, kc_w), :])
        idx = w_idx[0] * jnp.maximum(d[0:qb], 0.0)
        for h in range(1, IDX_HEADS):
            idx = idx + w_idx[h] * jnp.maximum(d[h * qb:(h + 1) * qb], 0.0)
        idx = jnp.where(idx == 0.0, 0.0, idx)
        key_scr[c] = jnp.where(off + col_iota <= row_pos, idx, jnp.nan)
        return carry

    lax.fori_loop(0, n_chunks, score_body, 0)

    def count(pred):
        def body(c, acc):
            for g in range(sub):
                acc = acc + jnp.where(pred(key_scr[c, :, g * LANES:(g + 1) * LANES]), 1.0, 0.0)
            return acc
        acc = lax.fori_loop(0, n_chunks, body, jnp.zeros((qb, LANES), f32))
        return jnp.sum(acc, axis=1, keepdims=True)

    def key_to_score(key):
        return lax.bitcast_convert_type(key ^ ((key >> 31) & 0x7FFFFFFF), f32)

    def search_body(it, lo):
        cand = lo + lax.shift_left(jnp.int32(1), 31 - it)
        cand_b = jnp.broadcast_to(key_to_score(cand), (qb, LANES))
        cnt = count(lambda kk: kk >= cand_b)
        return jnp.where(cnt >= float(k_sel), cand, lo)

    thr = lax.fori_loop(0, 32, search_body, jnp.full((qb, 1), INT32_MIN, i32))
    thr_f = key_to_score(thr)
    thr_b = jnp.broadcast_to(thr_f, (qb, LANES))
    need_b = jnp.broadcast_to(float(k_sel) - count(lambda kk: kk > thr_b), (qb, LANES))
    thr_w = jnp.broadcast_to(thr_f, (qb, kc_w))
    take_all = jnp.broadcast_to(jnp.where(thr == INT32_MIN, 1.0, 0.0), (qb, kc_w)) > 0.5
    need_w = jnp.tile(need_b, (1, sub))

    r_i = lax.broadcasted_iota(i32, (kc_w, kc_w + LANES), 0)
    c_i = lax.broadcasted_iota(i32, (kc_w, kc_w + LANES), 1)
    rank_scr[...] = jnp.where((r_i <= c_i) | (c_i >= kc_w), 1.0, 0.0).astype(bf16)

    def select_body(c, seen):
        kk = key_scr[c]
        eq = kk == thr_w
        pr = jnp.dot(jnp.where(eq, 1.0, 0.0).astype(bf16), rank_scr[...], preferred_element_type=f32)
        rank = jnp.tile(seen, (1, sub)) + pr[:, 0:kc_w]
        sel = (take_all & (kk == kk)) | (kk > thr_w) | (eq & (rank <= need_w))
        key_scr[c] = jnp.where(sel, 0.0, MASK_BIAS)
        return seen + pr[:, kc_w:kc_w + LANES]

    lax.fori_loop(0, n_chunks, select_body, jnp.zeros((qb, LANES), f32))

    n_blk = N_HEADS
    scale = 1.0 / math.sqrt(HEAD_DIM)
    for g in range(N_HEADS // N_KV_HEADS):
        src = q_ref[0, :, g * LANES:(g + 1) * LANES] * scale
        for j in range(N_KV_HEADS):
            r = g * N_KV_HEADS + j
            qpad_scr[r * qb:(r + 1) * qb, :] = jnp.where(half[j], src, 0.0).astype(bf16)
    m_scr[...] = jnp.full(m_scr.shape, -jnp.inf, f32)
    l_scr[...] = jnp.zeros(l_scr.shape, f32)
    acc_scr[...] = jnp.zeros(acc_scr.shape, f32)

    def attend_body(c, carry):
        off = pl.multiple_of(c * kc_w, kc_w)
        k_c = k_ref[0, pl.ds(off, kc_w), :]
        v_c = v_ref[0, pl.ds(off, kc_w), :]
        bias = key_scr[c]
        s_scr[...] = _nt_dot(qpad_scr[...], k_c)
        for r in range(n_blk):
            rows = slice(r * qb, (r + 1) * qb)
            s = s_scr[rows, :] + bias
            m_old = m_scr[rows, :]
            m_new = jnp.maximum(m_old, jnp.broadcast_to(jnp.max(s, axis=1, keepdims=True), (qb, LANES)))
            alpha = jnp.exp(m_old - m_new)
            p = jnp.exp(s - jnp.tile(m_new, (1, sub)))
            l_scr[rows, :] = alpha * l_scr[rows, :] + jnp.broadcast_to(jnp.sum(p, axis=1, keepdims=True), (qb, LANES))
            acc_scr[rows, :] = alpha * acc_scr[rows, :]
            p_scr[rows, :] = p.astype(bf16)
            m_scr[rows, :] = m_new
        acc_scr[...] += jnp.dot(p_scr[...], v_c, preferred_element_type=f32)
        return carry

    lax.fori_loop(0, n_chunks, attend_body, 0)

    for g in range(N_HEADS // N_KV_HEADS):
        outs = []
        for j in range(N_KV_HEADS):
            rows = slice((g * N_KV_HEADS + j) * qb, (g * N_KV_HEADS + j + 1) * qb)
            outs.append(acc_scr[rows, :] / l_scr[rows, :])
        o_ref[0, :, g * LANES:(g + 1) * LANES] = jnp.where(half[0], outs[0], outs[1])


def _split_bf16(x):
    hi = x.astype(jnp.bfloat16)
    return hi, (x - hi.astype(jnp.float32)).astype(jnp.bfloat16)


def dsa_prompt(proj, k, v, ki):
    b, t, _ = proj.shape
    qb = Q_BLOCK
    w_q, w_qi = N_HEADS * HEAD_DIM, IDX_HEADS * IDX_DIM
    assert t % DSA_KEY_CHUNK == 0 and t % qb == 0
    assert IDX_DIM == HEAD_DIM == LANES // 2 and IDX_HEADS % 2 == 0
    k_sel = min(TOPK_MAX, t // 4)
    n_rows = N_HEADS * qb
    ki_hi, ki_lo = _split_bf16(ki)
    ki4 = jnp.concatenate([ki_hi, ki_hi, ki_lo, ki_lo], axis=-1)
    k, v = k.astype(jnp.bfloat16), v.astype(jnp.bfloat16)
    seq_spec = pl.BlockSpec((1, t, LANES), lambda bi, i: (bi, 0, 0))
    return pl.pallas_call(
        functools.partial(_dsa_prompt_kernel, k_sel=k_sel),
        grid=(b, t // qb),
        in_specs=[pl.BlockSpec((1, qb, w_q), lambda bi, i: (bi, i, COL_Q // w_q)),
                  pl.BlockSpec((1, qb, w_qi), lambda bi, i: (bi, i, COL_QI // w_qi)),
                  pl.BlockSpec((1, qb, LANES), lambda bi, i: (bi, i, COL_SMALL // LANES)),
                  seq_spec, seq_spec, pl.BlockSpec((1, t, 2 * LANES), lambda bi, i: (bi, 0, 0))],
        out_specs=pl.BlockSpec((1, qb, N_HEADS * HEAD_DIM), lambda bi, i: (bi, i, 0)),
        out_shape=jax.ShapeDtypeStruct((b, t, N_HEADS * HEAD_DIM), jnp.float32),
        scratch_shapes=[pltpu.VMEM((t // DSA_KEY_CHUNK, qb, DSA_KEY_CHUNK), jnp.float32),
                        pltpu.VMEM((DSA_KEY_CHUNK, DSA_KEY_CHUNK + LANES), jnp.bfloat16),
                        pltpu.VMEM((n_rows, DSA_KEY_CHUNK), jnp.float32),
                        pltpu.VMEM((n_rows, DSA_KEY_CHUNK), jnp.bfloat16),
                        pltpu.VMEM((n_rows, LANES), jnp.bfloat16),
                        pltpu.VMEM((IDX_HEADS * qb, 2 * LANES), jnp.bfloat16),
                        pltpu.VMEM((n_rows, LANES), jnp.float32),
                        pltpu.VMEM((n_rows, LANES), jnp.float32),
                        pltpu.VMEM((n_rows, LANES), jnp.float32)],
        compiler_params=pltpu.CompilerParams(dimension_semantics=("parallel", "arbitrary"),
                                             vmem_limit_bytes=VMEM_LIMIT_BYTES),
        name="dsa_prompt",
    )(proj, proj, proj, k, v, ki4)


DEC_ROWS = 8


def _dsa_decode_kernel(pt_ref, q_ref, qi_ref, sm_ref, kn_ref, vn_ref, *rest, n_pages, k_sel):
    f32, i32, bf16 = jnp.float32, jnp.int32, jnp.bfloat16
    k_pages = rest[0:n_pages]
    v_pages = rest[n_pages:2 * n_pages]
    ik_pages = rest[2 * n_pages:3 * n_pages]
    o_ref = rest[3 * n_pages]
    kall, vall, ikall, new_scr, qis, wpad, qpad, key_scr, p_scr = rest[3 * n_pages + 1:]
    t_new = q_ref.shape[1]
    past = n_pages * PAGE_SIZE
    s_pad = past + PAGE_SIZE
    n_chunks = s_pad // LANES
    rows = DEC_ROWS

    for p in range(n_pages):
        sl = slice(p * PAGE_SIZE, (p + 1) * PAGE_SIZE)
        for j in range(N_KV_HEADS):
            kall[j * HEAD_DIM:(j + 1) * HEAD_DIM, sl] = k_pages[p][j]
            vall[j * HEAD_DIM:(j + 1) * HEAD_DIM, sl] = v_pages[p][j]
        ikall[:, sl] = ik_pages[p][...]
    tail = slice(past, s_pad)
    for src, dst, n_feat in ((kn_ref[0], kall, LANES), (vn_ref[0], vall, LANES),
                             (sm_ref[0], ikall, IDX_DIM)):
        new_scr[...] = jnp.zeros(new_scr.shape, f32)
        new_scr[0:t_new, :] = src
        dst[:, tail] = new_scr[...].T[0:n_feat, :]

    qis[...] = jnp.zeros(qis.shape, f32)
    wpad[...] = jnp.zeros(wpad.shape, f32)
    for h in range(IDX_HEADS):
        qis[h * rows:h * rows + t_new, :] = qi_ref[0, :, h * IDX_DIM:(h + 1) * IDX_DIM]
    wpad[0:t_new, :] = sm_ref[0]
    q_hi, q_lo = _split_bf16(qis[...])
    k_hi, k_lo = _split_bf16(ikall[...])

    def mm_f32(a, b):
        return jnp.dot(a, b, preferred_element_type=f32)

    d = mm_f32(q_hi, k_hi) + mm_f32(q_lo, k_hi) + mm_f32(q_hi, k_lo)
    idx = wpad[:, IDX_DIM:IDX_DIM + 1] * jnp.maximum(d[0:rows], 0.0)
    for h in range(1, IDX_HEADS):
        idx = idx + wpad[:, IDX_DIM + h:IDX_DIM + h + 1] * jnp.maximum(d[h * rows:(h + 1) * rows], 0.0)
    idx = jnp.where(idx == 0.0, 0.0, idx)
    col = lax.broadcasted_iota(i32, (rows, s_pad), 1)
    q_pos = past + lax.broadcasted_iota(i32, (rows, s_pad), 0)
    key_scr[...] = jnp.where((col <= q_pos) & (col < past + t_new), idx, jnp.nan)

    def key_to_score(key):
        return lax.bitcast_convert_type(key ^ ((key >> 31) & 0x7FFFFFFF), f32)

    def search_body(it, lo):
        step = lax.shift_left(jnp.int32(1), 30 - 2 * it)
        keys = key_scr[...]
        for mult in (1, 2, 3):
            cand = lo + step * mult
            cnt = jnp.sum(jnp.where(keys >= key_to_score(cand), 1.0, 0.0), axis=1, keepdims=True)
            best = jnp.where(cnt >= float(k_sel), cand, lo if mult == 1 else best)
        return best

    thr = lax.fori_loop(0, 16, search_body, jnp.full((rows, 1), INT32_MIN, i32))
    take_all = jnp.broadcast_to(jnp.where(thr == INT32_MIN, 1.0, 0.0), (rows, LANES)) > 0.5
    thr = key_to_score(thr)
    need = float(k_sel) - jnp.sum(jnp.where(key_scr[...] > thr, 1.0, 0.0), axis=1, keepdims=True)
    tri = jnp.where(lax.broadcasted_iota(i32, (LANES, LANES), 0) <= lax.broadcasted_iota(i32, (LANES, LANES), 1),
                    1.0, 0.0).astype(bf16)
    eq_f = [jnp.where(key_scr[:, c * LANES:(c + 1) * LANES] == thr, 1.0, 0.0) for c in range(n_chunks)]
    totals = [jnp.sum(e, axis=1, keepdims=True) for e in eq_f]
    seen = jnp.zeros((rows, 1), f32)
    for c in range(n_chunks):
        kk = key_scr[:, c * LANES:(c + 1) * LANES]
        rank = seen + jnp.dot(eq_f[c].astype(bf16), tri, preferred_element_type=f32)
        sel = (take_all & (kk == kk)) | (kk > thr) | ((kk == thr) & (rank <= need))
        key_scr[:, c * LANES:(c + 1) * LANES] = jnp.where(sel, 0.0, MASK_BIAS)
        seen = seen + totals[c]
    bias = key_scr[...]

    lane = lax.broadcasted_iota(i32, (t_new, LANES), 1)
    half = [lane < HEAD_DIM, lane >= HEAD_DIM]
    scale = 1.0 / math.sqrt(HEAD_DIM)
    qpad[...] = jnp.zeros(qpad.shape, f32)
    for g in range(N_HEADS // N_KV_HEADS):
        src = q_ref[0, :, g * LANES:(g + 1) * LANES] * scale
        for j in range(N_KV_HEADS):
            r = g * N_KV_HEADS + j
            qpad[r * rows:r * rows + t_new, :] = jnp.where(half[j], src, 0.0)
    s_all = mm_f32(qpad[...].astype(bf16), kall[...].astype(bf16))
    for r in range(N_HEADS):
        s = s_all[r * rows:(r + 1) * rows] + bias
        p = jnp.exp(s - jnp.max(s, axis=1, keepdims=True))
        p_scr[r * rows:(r + 1) * rows, :] = p / jnp.sum(p, axis=1, keepdims=True)
    o_all = _nt_dot(p_scr[...].astype(bf16), vall[...].astype(bf16))
    for g in range(N_HEADS // N_KV_HEADS):
        r0, r1 = g * N_KV_HEADS * rows, (g * N_KV_HEADS + 1) * rows
        o_ref[0, :, g * LANES:(g + 1) * LANES] = jnp.where(half[0], o_all[r0:r0 + t_new], o_all[r1:r1 + t_new])


def dsa_decode(proj, k_new, v_new, cache_k, cache_v, cache_ik, page_table, layer):
    b, t_new, _ = proj.shape
    n_pages = page_table.shape[1]
    assert t_new <= DEC_ROWS and PAGE_SIZE == LANES
    s_pad = (n_pages + 1) * PAGE_SIZE
    k_sel = min(TOPK_MAX, (n_pages * PAGE_SIZE + t_new) // 4)
    cache_k = jnp.transpose(cache_k, (0, 1, 3, 4, 2))
    cache_v = jnp.transpose(cache_v, (0, 1, 3, 4, 2))
    cache_ik = jnp.transpose(cache_ik, (0, 1, 3, 2))

    def tok_spec(width, col=0):
        return pl.BlockSpec((1, t_new, width), lambda bi, pt: (bi, 0, col // width))

    def page_spec(p):
        return pl.BlockSpec((None, None, IDX_DIM, PAGE_SIZE), lambda bi, pt, p=p: (pt[bi, p], layer, 0, 0))

    def kv_page_spec(p):
        return pl.BlockSpec((None, None, N_KV_HEADS, HEAD_DIM, PAGE_SIZE),
                            lambda bi, pt, p=p: (pt[bi, p], layer, 0, 0, 0))

    grid_spec = pltpu.PrefetchScalarGridSpec(
        num_scalar_prefetch=1,
        grid=(b,),
        in_specs=[tok_spec(N_HEADS * HEAD_DIM, COL_Q), tok_spec(IDX_HEADS * IDX_DIM, COL_QI),
                  tok_spec(LANES, COL_SMALL), tok_spec(LANES), tok_spec(LANES)]
                 + [kv_page_spec(p) for p in range(n_pages)]
                 + [kv_page_spec(p) for p in range(n_pages)]
                 + [page_spec(p) for p in range(n_pages)],
        out_specs=tok_spec(N_HEADS * HEAD_DIM),
        scratch_shapes=[pltpu.VMEM((LANES, s_pad), jnp.float32),
                        pltpu.VMEM((LANES, s_pad), jnp.float32),
                        pltpu.VMEM((IDX_DIM, s_pad), jnp.float32),
                        pltpu.VMEM((LANES, LANES), jnp.float32),
                        pltpu.VMEM((IDX_HEADS * DEC_ROWS, IDX_DIM), jnp.float32),
                        pltpu.VMEM((DEC_ROWS, LANES), jnp.float32),
                        pltpu.VMEM((N_HEADS * DEC_ROWS, LANES), jnp.float32),
                        pltpu.VMEM((DEC_ROWS, s_pad), jnp.float32),
                        pltpu.VMEM((N_HEADS * DEC_ROWS, s_pad), jnp.float32)])
    return pl.pallas_call(
        functools.partial(_dsa_decode_kernel, n_pages=n_pages, k_sel=k_sel),
        grid_spec=grid_spec,
        out_shape=jax.ShapeDtypeStruct((b, t_new, N_HEADS * HEAD_DIM), jnp.float32),
        compiler_params=pltpu.CompilerParams(dimension_semantics=("arbitrary",),
                                             vmem_limit_bytes=VMEM_LIMIT_BYTES),
        name="dsa_decode",
    )(page_table, proj, proj, proj, k_new, v_new,
      *([cache_k] * n_pages), *([cache_v] * n_pages), *([cache_ik] * n_pages))


COL_GATES, COL_XBC, COL_Q, COL_GLU, COL_Z, COL_QI, COL_K, COL_V, COL_SMALL, COL_DT, D_IN_PAD = (
    0, 3072, 4608, 5120, 6144, 7168, 7424, 7552, 7680, 7808, 7936)
ROW_BLOCK = 256


def _ln_rows(x, g, b):
    mu = jnp.mean(x, axis=-1, keepdims=True)
    xc = x - mu
    var = jnp.mean(xc * xc, axis=-1, keepdims=True)
    return xc * lax.rsqrt(var + LN_EPS) * g + b


def _bdot(a, w_ref):
    return jnp.dot(a.astype(jnp.bfloat16), w_ref[...], preferred_element_type=jnp.float32)


def _branch_mix_kernel(cv_ref, o_ref, y_ref, z_ref, gates_ref, x_ref, lncg_ref, lncb_ref, ng_ref,
                       wc_ref, wa_ref, ws_ref, wm_ref, lng_ref, lnb_ref, out_ref):
    ca = _ln_rows(cv_ref[...], lncg_ref[...], lncb_ref[...])
    y_a = _bdot(ca * jax.nn.sigmoid(ca), wc_ref)
    y_b = _bdot(o_ref[...], wa_ref)
    z = z_ref[...]
    t = y_ref[...] * (z * jax.nn.sigmoid(z))
    t = t * lax.rsqrt(jnp.mean(t * t, axis=-1, keepdims=True) + LN_EPS) * ng_ref[...]
    y_c = _bdot(t, ws_ref)
    d = D_MODEL
    mix = (jax.nn.sigmoid(gates_ref[:, 0:d]) * y_a + jax.nn.sigmoid(gates_ref[:, d:2 * d]) * y_b
           + jax.nn.sigmoid(gates_ref[:, 2 * d:3 * d]) * y_c)
    out_ref[...] = _ln_rows(DN_ALPHA * x_ref[...] + _bdot(mix, wm_ref), lng_ref[...], lnb_ref[...])


def branch_mix_ln(cv, o, y, proj, x, lncg, lncb, ng, wc, wa, ws, wm, lng, lnb):
    m = x.shape[0]
    tm = min(ROW_BLOCK, m)
    assert m % tm == 0
    d = D_MODEL

    def rows(width, col_block=0):
        return pl.BlockSpec((tm, width), lambda i, cb=col_block: (i, cb))

    def whole(a):
        return pl.BlockSpec(a.shape, lambda i: (0,) * a.ndim, pipeline_mode=pl.Buffered(1))

    vecs = [a.reshape(1, -1) for a in (lncg, lncb, ng)]
    lnv = [a.reshape(1, -1) for a in (lng, lnb)]
    return pl.pallas_call(
        _branch_mix_kernel,
        grid=(m // tm,),
        in_specs=[rows(CONV_DIM), rows(N_HEADS * HEAD_DIM), rows(SSM_INNER), rows(SSM_INNER, COL_Z // SSM_INNER),
                  rows(3 * d, COL_GATES // (3 * d)), rows(d)]
                 + [whole(a) for a in vecs] + [whole(a) for a in (wc, wa, ws, wm)] + [whole(a) for a in lnv],
        out_specs=rows(d),
        out_shape=jax.ShapeDtypeStruct((m, d), jnp.float32),
        compiler_params=pltpu.CompilerParams(dimension_semantics=("parallel",), vmem_limit_bytes=VMEM_LIMIT_BYTES),
        name="branch_mix_ln",
    )(cv, o, y, proj, proj, x, *vecs, wc, wa, ws, wm, *lnv)


def _mem_attn_kernel(x_ref, mk_ref, mv_ref, wq_ref, wo_ref, lng_ref, lnb_ref, out_ref, o_scr, *, seqs, t_seq):
    bf16 = jnp.bfloat16
    x = x_ref[...]
    q = _bdot(x, wq_ref)
    scale = 1.0 / math.sqrt(MEM_HEAD_DIM)
    for s in range(seqs):
        qs = q[s * t_seq:(s + 1) * t_seq].astype(bf16)
        for h in range(MEM_HEADS):
            cols = slice(h * MEM_HEAD_DIM, (h + 1) * MEM_HEAD_DIM)
            sc = _nt_dot(qs[:, cols], mk_ref[s, :, h, :].astype(bf16)) * scale
            p = jnp.exp(sc - jnp.max(sc, axis=-1, keepdims=True))
            p = p / jnp.sum(p, axis=-1, keepdims=True)
            o_scr[s * t_seq:(s + 1) * t_seq, cols] = jnp.dot(p.astype(bf16), mv_ref[s, :, h, :].astype(bf16),
                                                              preferred_element_type=jnp.float32)
    out_ref[...] = _ln_rows(DN_ALPHA * x + _bdot(o_scr[...], wo_ref), lng_ref[...], lnb_ref[...])


def mem_attn_ln(x, mk, mv, wq, wo, lng, lnb, t_seq, layer=None):
    m, d = x.shape
    hd = MEM_HEADS * MEM_HEAD_DIM
    if t_seq >= ROW_BLOCK:
        seqs, tm = 1, ROW_BLOCK
        assert t_seq % tm == 0
        per_seq = t_seq // tm
        seq_of = lambda i: i // per_seq
    else:
        seqs = max(1, 32 // t_seq)
        tm = seqs * t_seq
        assert m % tm == 0
        seq_of = lambda i: i
    if layer is None:
        mem_spec = pl.BlockSpec((seqs, N_MEM, MEM_HEADS, MEM_HEAD_DIM), lambda i: (seq_of(i), 0, 0, 0))
    else:
        mem_spec = pl.BlockSpec((seqs, None, N_MEM, MEM_HEADS, MEM_HEAD_DIM), lambda i: (seq_of(i), layer, 0, 0, 0))

    def whole(a):
        return pl.BlockSpec(a.shape, lambda i: (0,) * a.ndim, pipeline_mode=pl.Buffered(1))

    lnv = [a.reshape(1, -1) for a in (lng, lnb)]
    return pl.pallas_call(
        functools.partial(_mem_attn_kernel, seqs=seqs, t_seq=min(t_seq, tm)),
        grid=(m // tm,),
        in_specs=[pl.BlockSpec((tm, d), lambda i: (i, 0)), mem_spec, mem_spec, whole(wq), whole(wo)]
                 + [whole(a) for a in lnv],
        out_specs=pl.BlockSpec((tm, d), lambda i: (i, 0)),
        out_shape=jax.ShapeDtypeStruct((m, d), jnp.float32),
        scratch_shapes=[pltpu.VMEM((tm, hd), jnp.float32)],
        compiler_params=pltpu.CompilerParams(dimension_semantics=("parallel",), vmem_limit_bytes=VMEM_LIMIT_BYTES),
        name="mem_attn_ln",
    )(x, mk, mv, wq, wo, *lnv)


MOE_ROW_BLOCK = 1024


def _moe_kernel(x_ref, wr_hi_ref, wr_lo_ref, br_ref, wg_ref, wu_ref, wd_ref, lng_ref, lnb_ref, out_ref,
                xb_scr, comb_scr, acc_scr):
    f32 = jnp.float32
    e = pl.program_id(1)
    tm = x_ref.shape[0]
    lane = lax.broadcasted_iota(jnp.int32, (tm, LANES), 1).astype(f32)

    @pl.when(e == 0)
    def _route():
        x_hi, x_lo = _split_bf16(x_ref[...])
        xb_scr[...] = x_hi
        lg = (jnp.dot(x_hi, wr_hi_ref[...], preferred_element_type=f32)
              + jnp.dot(x_lo, wr_hi_ref[...], preferred_element_type=f32)
              + jnp.dot(x_hi, wr_lo_ref[...], preferred_element_type=f32) + br_ref[...])
        is_g = lane < N_GROUPS
        mg = jnp.max(jnp.where(is_g, lg, -jnp.inf), axis=1, keepdims=True)
        g_prob = 1.0 / jnp.sum(jnp.where(is_g, jnp.exp(lg - mg), 0.0), axis=1, keepdims=True)
        g_idx = jnp.min(jnp.where(is_g & (lg == mg), lane, float(LANES)), axis=1, keepdims=True)
        lo_e = N_GROUPS + EXPERTS_PER_GROUP * g_idx
        is_e = (lane >= lo_e) & (lane < lo_e + EXPERTS_PER_GROUP)
        me = jnp.max(jnp.where(is_e, lg, -jnp.inf), axis=1, keepdims=True)
        ee = jnp.where(is_e, jnp.exp(lg - me), 0.0)
        pe = jnp.where(is_e, ee / jnp.sum(ee, axis=1, keepdims=True), -1.0)
        p1 = jnp.max(pe, axis=1, keepdims=True)
        first = jnp.min(jnp.where(pe == p1, lane, float(LANES)), axis=1, keepdims=True)
        pe2 = jnp.where(lane == first, -1.0, pe)
        p2 = jnp.max(pe2, axis=1, keepdims=True)
        second = jnp.min(jnp.where(pe2 == p2, lane, float(LANES)), axis=1, keepdims=True)
        norm = g_prob / (p1 + p2)
        comb_scr[...] = jnp.where(lane == first, p1 * norm, jnp.where(lane == second, p2 * norm, 0.0))
        acc_scr[...] = jnp.zeros(acc_scr.shape, f32)

    xb = xb_scr[...]
    hg = jnp.dot(xb, wg_ref[0], preferred_element_type=f32)
    hu = jnp.dot(xb, wu_ref[0], preferred_element_type=f32)
    c = jnp.sum(jnp.where(lane == (e + N_GROUPS).astype(f32), comb_scr[...], 0.0), axis=1, keepdims=True)
    hid = hg * jax.nn.sigmoid(hg) * hu * c
    acc_scr[...] += jnp.dot(hid.astype(jnp.bfloat16), wd_ref[0], preferred_element_type=f32)

    @pl.when(e == pl.num_programs(1) - 1)
    def _finish():
        out_ref[...] = _ln_rows(DN_ALPHA * x_ref[...] + acc_scr[...], lng_ref[...], lnb_ref[...])


def moe_ln(x, w_group, b_group, w_router, b_router, wg, wu, wd, lng, lnb):
    m, d = x.shape
    tm = min(MOE_ROW_BLOCK, m)
    assert m % tm == 0 and N_GROUPS + N_EXPERTS <= LANES
    pad = jnp.zeros((d, LANES - N_GROUPS - N_EXPERTS), jnp.float32)
    wr_hi, wr_lo = _split_bf16(jnp.concatenate([w_group, w_router, pad], axis=1))
    br = jnp.concatenate([b_group, b_router, pad[0]]).reshape(1, LANES)
    lnv = [a.reshape(1, -1) for a in (lng, lnb)]

    def whole(a):
        return pl.BlockSpec(a.shape, lambda i, e: (0,) * a.ndim, pipeline_mode=pl.Buffered(1))

    return pl.pallas_call(
        _moe_kernel,
        grid=(m // tm, N_EXPERTS),
        in_specs=[pl.BlockSpec((tm, d), lambda i, e: (i, 0)), whole(wr_hi), whole(wr_lo), whole(br),
                  pl.BlockSpec((1, d, D_EXPERT), lambda i, e: (e, 0, 0)),
                  pl.BlockSpec((1, d, D_EXPERT), lambda i, e: (e, 0, 0)),
                  pl.BlockSpec((1, D_EXPERT, d), lambda i, e: (e, 0, 0))] + [whole(a) for a in lnv],
        out_specs=pl.BlockSpec((tm, d), lambda i, e: (i, 0)),
        out_shape=jax.ShapeDtypeStruct((m, d), jnp.float32),
        scratch_shapes=[pltpu.VMEM((tm, d), jnp.bfloat16), pltpu.VMEM((tm, LANES), jnp.float32),
                        pltpu.VMEM((tm, d), jnp.float32)],
        compiler_params=pltpu.CompilerParams(dimension_semantics=("parallel", "arbitrary"),
                                             vmem_limit_bytes=VMEM_LIMIT_BYTES),
        name="moe_ln",
    )(x, wr_hi, wr_lo, br, wg, wu, wd, *lnv)


CONV_HIST_ROWS = 32
CONV_ROW_BLOCK = 128
CONV_ROW_TILE = 32
SUBLANES = 8


def _conv_glu_kernel(x_ref, hist_ref, w_ref, cv_ref, st_ref, xp_scr, z_scr):
    f32 = jnp.float32
    ti = pl.program_id(1)
    tb = x_ref.shape[1]
    n_hist = CONV_WIDTH - 1
    base = CONV_HIST_ROWS - n_hist

    @pl.when(ti == 0)
    def _init():
        xp_scr[0:CONV_HIST_ROWS, :] = jnp.zeros((CONV_HIST_ROWS, CONV_DIM), f32)
        xp_scr[base:CONV_HIST_ROWS, :] = hist_ref[0]

    xin = x_ref[0]
    xp_scr[CONV_HIST_ROWS:CONV_HIST_ROWS + tb, :] = xin[:, :CONV_DIM] * jax.nn.sigmoid(xin[:, CONV_DIM:])
    if tb % SUBLANES == 0:
        rows_z = tb + SUBLANES
        xp_scr[CONV_HIST_ROWS + tb:CONV_HIST_ROWS + tb + 2 * SUBLANES, :] = jnp.zeros((2 * SUBLANES, CONV_DIM), f32)
        for b in range(SUBLANES):
            taps = [(a, SUBLANES * a + b - base) for a in range(CONV_HIST_ROWS // SUBLANES + 1)]
            taps = [(a, j) for a, j in taps if 0 <= j < CONV_WIDTH]
            for r0 in range(0, rows_z, CONV_ROW_TILE):
                nr = min(CONV_ROW_TILE, rows_z - r0)
                acc = None
                for a, j in taps:
                    term = w_ref[j:j + 1, :] * xp_scr[SUBLANES * a + r0:SUBLANES * a + r0 + nr, :]
                    acc = term if acc is None else acc + term
                z_scr[b, r0:r0 + nr, :] = acc
        out = z_scr[0, 0:tb, :]
        for b in range(1, SUBLANES):
            out = out + z_scr[b, b:b + tb, :]
    else:
        out = w_ref[0:1, :] * xp_scr[base:base + tb, :]
        for j in range(1, CONV_WIDTH):
            out = out + w_ref[j:j + 1, :] * xp_scr[base + j:base + j + tb, :]
    cv_ref[0] = out
    st_ref[0] = xp_scr[base + tb:base + tb + n_hist, :]
    xp_scr[base:CONV_HIST_ROWS, :] = xp_scr[base + tb:CONV_HIST_ROWS + tb, :]


def conv_glu(proj, hist, w):
    b, t, _ = proj.shape
    tb = CONV_ROW_BLOCK if t % CONV_ROW_BLOCK == 0 else t
    n_hist = CONV_WIDTH - 1
    assert tb >= n_hist or t == tb
    return pl.pallas_call(
        _conv_glu_kernel,
        grid=(b, t // tb),
        in_specs=[pl.BlockSpec((1, tb, 2 * CONV_DIM), lambda bi, ti: (bi, ti, COL_GLU // (2 * CONV_DIM))),
                  pl.BlockSpec((1, n_hist, CONV_DIM), lambda bi, ti: (bi, 0, 0)),
                  pl.BlockSpec(w.shape, lambda bi, ti: (0, 0))],
        out_specs=[pl.BlockSpec((1, tb, CONV_DIM), lambda bi, ti: (bi, ti, 0)),
                   pl.BlockSpec((1, n_hist, CONV_DIM), lambda bi, ti: (bi, 0, 0))],
        out_shape=[jax.ShapeDtypeStruct((b, t, CONV_DIM), jnp.float32),
                   jax.ShapeDtypeStruct((b, n_hist, CONV_DIM), jnp.float32)],
        scratch_shapes=[pltpu.VMEM((CONV_HIST_ROWS + tb + 2 * SUBLANES, CONV_DIM), jnp.float32),
                        pltpu.VMEM((SUBLANES, tb + SUBLANES, CONV_DIM), jnp.float32)],
        compiler_params=pltpu.CompilerParams(dimension_semantics=("parallel", "arbitrary"),
                                             vmem_limit_bytes=VMEM_LIMIT_BYTES),
        name="conv_glu",
    )(proj, hist, w)


SSD_HIST_ROWS = 8


def _ssd_kernel(xbc_ref, dt_ref, hist_ref, cw_ref, cb_ref, dtb_ref, alog_ref, dexp_ref, h0_ref,
                y_ref, hT_ref, xp_scr, dt_scr, ht_scr, xe_scr):
    f32, i32, bf16 = jnp.float32, jnp.int32, jnp.bfloat16
    c = pl.program_id(1)
    l_in = xbc_ref.shape[1]
    L = SSM_CHUNK
    n_hist = SSM_CONV - 1
    base = SSD_HIST_ROWS - n_hist
    gw = SSM_STATE
    heads_per_group = SSM_HEADS // SSM_GROUPS
    pairs_per_group = heads_per_group // 2

    @pl.when(c == 0)
    def _init():
        xp_scr[base:SSD_HIST_ROWS, :] = hist_ref[0]
        ht_scr[...] = h0_ref[0].T

    xp_scr[SSD_HIST_ROWS:SSD_HIST_ROWS + l_in, :] = xbc_ref[0]
    dt_scr[0:l_in, :] = dt_ref[0]
    if l_in < L:
        xp_scr[SSD_HIST_ROWS + l_in:SSD_HIST_ROWS + L, :] = jnp.zeros((L - l_in, SSM_XBC), f32)
        dt_scr[l_in:L, :] = jnp.zeros((L - l_in, LANES), f32)
    acc = cw_ref[0:1, :] * xp_scr[base:base + L, :]
    for j in range(1, SSM_CONV):
        acc = acc + cw_ref[j:j + 1, :] * xp_scr[base + j:base + j + L, :]
    acc = acc + cb_ref[...]
    xbc = acc * jax.nn.sigmoid(acc)
    xp_scr[base:SSD_HIST_ROWS, :] = xp_scr[base + L:SSD_HIST_ROWS + L, :]
    xs = xbc[:, 0:SSM_INNER]
    bm = xbc[:, SSM_INNER:SSM_INNER + SSM_GROUPS * gw]
    cm = xbc[:, SSM_INNER + SSM_GROUPS * gw:SSM_INNER + 2 * SSM_GROUPS * gw]

    x_dt = dt_scr[...] + dtb_ref[...]
    dtf = jnp.maximum(x_dt, 0.0) + jnp.log1p(jnp.exp(-jnp.abs(x_dt)))
    if l_in < L:
        dtf = jnp.where(lax.broadcasted_iota(i32, (L, LANES), 0) < l_in, dtf, 0.0)
    la = dtf * (-jnp.exp(alog_ref[...]))
    row = lax.broadcasted_iota(i32, (L, L), 0)
    colk = lax.broadcasted_iota(i32, (L, L), 1)
    causal = colk <= row
    tril = jnp.where(causal, 1.0, 0.0).astype(bf16)
    la_hi = la.astype(bf16)
    r1 = la - la_hi.astype(f32)
    la_mid = r1.astype(bf16)
    la_lo = (r1 - la_mid.astype(f32)).astype(bf16)
    cs = (jnp.dot(tril, la_hi, preferred_element_type=f32) + jnp.dot(tril, la_mid, preferred_element_type=f32)
          + jnp.dot(tril, la_lo, preferred_element_type=f32))
    cs_t = cs.T
    ecs = jnp.exp(cs)
    to_end = jnp.exp(cs[L - 1:L, :] - cs)

    lane = lax.broadcasted_iota(i32, (L, LANES), 1)
    half0 = lane < SSM_HEAD_DIM

    def pair_cols(a, h0):
        return jnp.where(half0, a[:, h0:h0 + 1], a[:, h0 + 1:h0 + 2])

    for g in range(SSM_GROUPS):
        b_g = bm[:, g * gw:(g + 1) * gw]
        c_g = cm[:, g * gw:(g + 1) * gw].astype(bf16)
        cb = _nt_dot(c_g, b_g.astype(bf16))
        cols = slice(g * heads_per_group * SSM_HEAD_DIM, (g + 1) * heads_per_group * SSM_HEAD_DIM)
        y_off = jnp.dot(c_g, ht_scr[:, cols].astype(bf16), preferred_element_type=f32)
        for i in range(pairs_per_group):
            h0 = g * heads_per_group + 2 * i
            pc = slice((h0 // 2) * LANES, (h0 // 2 + 1) * LANES)
            xs_p = xs[:, pc]
            xdt_p = xs_p * pair_cols(dtf, h0)
            att = []
            for hh in range(2):
                seg = cs[:, h0 + hh:h0 + hh + 1] - cs_t[h0 + hh:h0 + hh + 1, :]
                att.append((cb * jnp.exp(jnp.where(causal, seg, -jnp.inf))).astype(bf16))
            rhs = jnp.concatenate([jnp.where(half0, xdt_p, 0.0), jnp.where(half0, 0.0, xdt_p)], axis=0).astype(bf16)
            y_diag = jnp.dot(jnp.concatenate(att, axis=1), rhs, preferred_element_type=f32)
            xe_scr[:, pc] = xdt_p * pair_cols(to_end, h0)
            y_p = y_diag + y_off[:, i * LANES:(i + 1) * LANES] * pair_cols(ecs, h0) + dexp_ref[:, pc] * xs_p
            y_ref[0, :, pc] = y_p[0:l_in]
        bt_hi, bt_lo = _split_bf16(b_g.T)
        xe_hi, xe_lo = _split_bf16(xe_scr[:, cols])
        st = (jnp.dot(bt_hi, xe_hi, preferred_element_type=f32) + jnp.dot(bt_lo, xe_hi, preferred_element_type=f32)
              + jnp.dot(bt_hi, xe_lo, preferred_element_type=f32))
        for i in range(pairs_per_group):
            h0 = g * heads_per_group + 2 * i
            pc = slice((h0 // 2) * LANES, (h0 // 2 + 1) * LANES)
            decay = jnp.where(half0[0:1, :], ecs[L - 1:L, h0:h0 + 1], ecs[L - 1:L, h0 + 1:h0 + 2])
            ht_scr[:, pc] = ht_scr[:, pc] * decay + st[:, i * LANES:(i + 1) * LANES]

    @pl.when(c == pl.num_programs(1) - 1)
    def _done():
        hT_ref[0] = ht_scr[...].T


def ssd_prompt(proj, hist, conv_w, conv_b, dt_bias, a_log, d_skip, h0_t):
    b, t, _ = proj.shape
    L = SSM_CHUNK
    l_in = L if t % L == 0 else t
    assert t % l_in == 0 and l_in <= L
    assert SSM_HEAD_DIM * 2 == LANES and SSM_STATE == LANES and SSM_HEADS <= LANES

    def lane_pad(a):
        return jnp.concatenate([a, jnp.zeros((LANES - a.shape[0],), a.dtype)]).reshape(1, LANES)

    def whole(a):
        return pl.BlockSpec(a.shape, lambda bi, c: (0,) * a.ndim)

    params = [conv_w, conv_b.reshape(1, -1), lane_pad(dt_bias), lane_pad(a_log),
              jnp.repeat(d_skip, SSM_HEAD_DIM).reshape(1, SSM_INNER)]
    return pl.pallas_call(
        _ssd_kernel,
        grid=(b, t // l_in),
        in_specs=[pl.BlockSpec((1, l_in, SSM_XBC), lambda bi, c: (bi, c, COL_XBC // SSM_XBC)),
                  pl.BlockSpec((1, l_in, LANES), lambda bi, c: (bi, c, COL_DT // LANES)),
                  pl.BlockSpec((1, SSM_CONV - 1, SSM_XBC), lambda bi, c: (bi, 0, 0))]
                 + [whole(a) for a in params]
                 + [pl.BlockSpec((1, SSM_INNER, SSM_STATE), lambda bi, c: (bi, 0, 0))],
        out_specs=[pl.BlockSpec((1, l_in, SSM_INNER), lambda bi, c: (bi, c, 0)),
                   pl.BlockSpec((1, SSM_INNER, SSM_STATE), lambda bi, c: (bi, 0, 0))],
        out_shape=[jax.ShapeDtypeStruct((b, t, SSM_INNER), jnp.float32),
                   jax.ShapeDtypeStruct((b, SSM_INNER, SSM_STATE), jnp.float32)],
        scratch_shapes=[pltpu.VMEM((SSD_HIST_ROWS + L, SSM_XBC), jnp.float32),
                        pltpu.VMEM((L, LANES), jnp.float32),
                        pltpu.VMEM((SSM_STATE, SSM_INNER), jnp.float32),
                        pltpu.VMEM((L, SSM_INNER), jnp.float32)],
        compiler_params=pltpu.CompilerParams(dimension_semantics=("parallel", "arbitrary"),
                                             vmem_limit_bytes=VMEM_LIMIT_BYTES),
        name="ssd_prompt",
    )(proj, proj, hist, *params, h0_t)


def token_mixer(x, p, conv_buf, ssm_buf, ssm_h0, past):
    (w_in, conv_dw, ln_conv_g, ln_conv_b, w_conv_out, w_attn_out, ssm_conv_w, ssm_conv_b,
     ssm_dt_bias, ssm_a_log, ssm_d, ssm_norm_g, w_ssm_out, w_mix_out, ln_mix_g, ln_mix_b) = p
    b, t, _ = x.shape
    proj = mm_nt(x.reshape(b * t, D_MODEL), w_in).reshape(b, t, D_IN_PAD)
    k = proj[..., COL_K:COL_K + LANES]
    v = proj[..., COL_V:COL_V + LANES]
    ki = proj[..., COL_SMALL:COL_SMALL + IDX_DIM]
    cv, conv_state = conv_glu(proj, conv_buf, conv_dw)
    if past is None:
        o = dsa_prompt(proj, k, v, ki)
    else:
        cache_k, cache_v, cache_ik, page_table, layer = past
        o = dsa_decode(proj, k, v, cache_k, cache_v, cache_ik, page_table, layer)
    h0_t = ssm_h0.reshape(b, SSM_INNER, SSM_STATE)
    y, h_t = ssd_prompt(proj, ssm_buf, ssm_conv_w, ssm_conv_b, ssm_dt_bias, ssm_a_log, ssm_d, h0_t)
    h_last = h_t.reshape(b, SSM_HEADS, SSM_HEAD_DIM, SSM_STATE)
    n_keep = SSM_CONV - 1
    xbc_tail = proj[:, max(0, t - n_keep):, COL_XBC:COL_XBC + SSM_XBC]
    xbc_pad = jnp.concatenate([ssm_buf, xbc_tail], axis=1)
    m = b * t
    x_new = branch_mix_ln(cv.reshape(m, CONV_DIM), o.reshape(m, -1), y.reshape(m, SSM_INNER),
                          proj.reshape(m, D_IN_PAD), x.reshape(m, D_MODEL), ln_conv_g, ln_conv_b, ssm_norm_g,
                          w_conv_out, w_attn_out, w_ssm_out, w_mix_out, ln_mix_g, ln_mix_b).reshape(b, t, D_MODEL)
    return (x_new, k.reshape(b, t, N_KV_HEADS, HEAD_DIM), v.reshape(b, t, N_KV_HEADS, HEAD_DIM), ki,
            conv_state, xbc_pad[:, -(SSM_CONV - 1):], h_last)


def _pad_w_in_t(wt):
    sp = (0,) + SPLIT_POINTS + (wt.shape[0],)
    seg = [wt[sp[i]:sp[i + 1]] for i in range(len(SPLIT_SIZES))]
    glu, q, k, v, qi, ki, wi, z, xbc, dt, gates = seg
    pad = jnp.zeros((LANES - IDX_DIM - IDX_HEADS - SSM_HEADS, wt.shape[1]), wt.dtype)
    n_rep = N_HEADS // N_KV_HEADS
    q = q.reshape(N_KV_HEADS, n_rep, HEAD_DIM, -1).swapaxes(0, 1).reshape(q.shape)
    dt_pad = jnp.zeros((LANES - SSM_HEADS, wt.shape[1]), wt.dtype)
    out = jnp.concatenate([gates, xbc, q, glu, z, qi, k, v, ki, wi, dt, pad, dt, dt_pad], axis=0)
    assert out.shape[0] == D_IN_PAD
    return out


def _perm_w_attn_out(w):
    n_rep = N_HEADS // N_KV_HEADS
    return w.reshape(N_KV_HEADS, n_rep, HEAD_DIM, -1).swapaxes(0, 1).reshape(w.shape)


def kernel(x_prompt, x_sample, mem_prompt, cache_k, cache_v, cache_ik, cache_mem_k, cache_mem_v, state_conv, state_ssm_conv, state_ssm, page_table, w_in, conv_dw, ln_conv_g, ln_conv_b, w_conv_out, w_attn_out, ssm_conv_w, ssm_conv_b, ssm_dt_bias, ssm_a_log, ssm_d, ssm_norm_g, w_ssm_out, w_mix_out, ln_mix_g, ln_mix_b, w_mq, w_mk, w_mv, w_mo, ln_mem_g, ln_mem_b, w_group, b_group, w_router, b_router, w_e_gate, w_e_up, w_e_down, ln_ffn_g, ln_ffn_b):
    bf16 = jnp.bfloat16
    bp, tp, _ = x_prompt.shape
    bs, ts, _ = x_sample.shape
    xp, xs = x_prompt, x_sample
    outs_p = [[] for _ in range(8)]
    outs_s = [[] for _ in range(6)]
    for l in range(DEPTH):
        mix_p = (_pad_w_in_t(jnp.transpose(w_in[l])).astype(bf16), conv_dw[l], ln_conv_g[l], ln_conv_b[l],
                 w_conv_out[l].astype(bf16), _perm_w_attn_out(w_attn_out[l]).astype(bf16),
                 ssm_conv_w[l], ssm_conv_b[l], ssm_dt_bias[l], ssm_a_log[l], ssm_d[l], ssm_norm_g[l],
                 w_ssm_out[l].astype(bf16), w_mix_out[l].astype(bf16), ln_mix_g[l], ln_mix_b[l])
        moe_p = (w_group[l], b_group[l], w_router[l], b_router[l],
                 w_e_gate[l].astype(bf16), w_e_up[l].astype(bf16), w_e_down[l].astype(bf16), ln_ffn_g[l], ln_ffn_b[l])
        mem_p = (w_mq[l].astype(bf16), w_mo[l].astype(bf16), ln_mem_g[l], ln_mem_b[l])
        hd_mem = MEM_HEADS * MEM_HEAD_DIM
        xp, kp, vp, kip, cbp, sbp, hp = token_mixer(
            xp, mix_p,
            jnp.zeros((bp, CONV_WIDTH - 1, CONV_DIM), xp.dtype),
            jnp.zeros((bp, SSM_CONV - 1, SSM_XBC), xp.dtype),
            jnp.zeros((bp, SSM_HEADS, SSM_HEAD_DIM, SSM_STATE), jnp.float32),
            None)
        mem_kv = mm(mem_prompt.reshape(bp * N_MEM, D_MODEL),
                    jnp.concatenate([w_mk[l], w_mv[l]], axis=1).astype(bf16)).reshape(bp, N_MEM, 2 * hd_mem)
        mkp = mem_kv[..., :hd_mem].reshape(bp, N_MEM, MEM_HEADS, MEM_HEAD_DIM)
        mvp = mem_kv[..., hd_mem:].reshape(bp, N_MEM, MEM_HEADS, MEM_HEAD_DIM)
        xp = mem_attn_ln(xp.reshape(bp * tp, D_MODEL), mkp, mvp, *mem_p, t_seq=tp)
        xp = moe_ln(xp, *moe_p).reshape(bp, tp, D_MODEL)
        xs, ks_new, vs_new, kis, cbs, sbs, hs = token_mixer(
            xs, mix_p, state_conv[:, l], state_ssm_conv[:, l], state_ssm[:, l],
            (cache_k, cache_v, cache_ik, page_table, l))
        xs = mem_attn_ln(xs.reshape(bs * ts, D_MODEL), cache_mem_k, cache_mem_v, *mem_p, t_seq=ts, layer=l)
        xs = moe_ln(xs, *moe_p).reshape(bs, ts, D_MODEL)
        for lst, arr in zip(outs_p, (kp, vp, kip, mkp, mvp, cbp, sbp, hp)):
            lst.append(arr)
        for lst, arr in zip(outs_s, (ks_new, vs_new, kis, cbs, sbs, hs)):
            lst.append(arr)
    p_k, p_v, p_ik, p_mem_k, p_mem_v, p_conv, p_ssm_conv, p_ssm = [jnp.stack(a, axis=1) for a in outs_p]
    s_k, s_v, s_ik, s_conv, s_ssm_conv, s_ssm = [jnp.stack(a, axis=1) for a in outs_s]
    return (xp, xs, p_k, p_v, p_ik, p_mem_k, p_mem_v, p_conv, p_ssm_conv, p_ssm,
            s_k, s_v, s_ik, s_conv, s_ssm_conv, s_ssm)
```
